```python
import math
import jax
import jax.numpy as jnp
from jax import lax
import numpy as np

D_MODEL = 1024
BATCH = 16
SEQ = 2048
DEPTH = 4

N_MIXERS = 3
D_FF = 2816
NORM_EPS = 1e-6
HALF_STEP = 0.5

ML_HEADS = 4
ML_DQK = D_MODEL // 8
ML_DV = D_MODEL // 4
ML_CHUNK = 64
ML_CONV = 4
ML_GATE_CAP = 15.0
ML_IN = 2 * ML_HEADS * ML_DQK + 2 * ML_HEADS * ML_DV + 2 * ML_HEADS

RW_HEAD = 64
RW_HEADS = D_MODEL // RW_HEAD
RW_LORA_W = 64
RW_LORA_A = 64
RW_LORA_G = 128
RW_LN_EPS = 64e-5

RT_HEADS = 4
RT_DK = D_MODEL // RT_HEADS
RT_DV = 2 * RT_DK
RT_CHUNK = 128
RT_ROPE_BASE = 10000.0
RT_IN = 2 * RT_HEADS * RT_DK + 2 * RT_HEADS * RT_DV

N_ML_LAYERS = (DEPTH + 2) // 3
N_RW_LAYERS = (DEPTH + 1) // 3
N_RT_LAYERS = DEPTH // 3

kernel_name = 'hybrid_mlstm_rwkv7_retention_macaron'

F32 = jnp.float32


def rms_norm(x, g, eps=NORM_EPS):
    xf = x.astype(F32)
    y = xf * lax.rsqrt(jnp.mean(xf * xf, axis=-1, keepdims=True) + eps)
    return (y * g.astype(F32)).astype(x.dtype)


def head_rms(y, eps=NORM_EPS):
    return y * lax.rsqrt(jnp.mean(y * y, axis=-1, keepdims=True) + eps)


def swiglu_ffn(x, w_gu, w_down):
    gate, up = jnp.split(x @ w_gu, 2, axis=-1)
    return (jax.nn.silu(gate) * up) @ w_down


def causal_depthwise_conv(x, w):
    K, C = w.shape
    return lax.conv_general_dilated(x, w[:, None, :].astype(x.dtype), window_strides=(1,),
                                    padding=[(K - 1, 0)], dimension_numbers=('NWC', 'WIO', 'NWC'),
                                    feature_group_count=C)


def to_chunks(a, L):
    B, T, H = a.shape[:3]
    a = a.reshape((B, T // L, L, H) + a.shape[3:])
    perm = (1, 0, 3, 2) + tuple(range(4, a.ndim))
    return jnp.transpose(a, perm)


def from_chunks(a):
    NC, B, H, L, d = a.shape
    return jnp.transpose(a, (1, 0, 3, 2, 4)).reshape(B, NC * L, H, d)


def mlstm_chunkwise(q, k, v, i_log, f_log):
    B, T, H, dk = q.shape
    dv = v.shape[-1]
    L = ML_CHUNK
    mask = jnp.tril(jnp.ones((L, L), dtype=bool))
    xs = tuple(to_chunks(a.astype(F32), L) for a in (q, k, v, i_log, f_log))

    def step(carry, xc):
        C, n, m = carry
        qc, kc, vc, ic, fc = xc
        b = jnp.cumsum(fc, axis=-1)
        log_d = jnp.where(mask, b[..., :, None] - b[..., None, :] + ic[..., None, :], -jnp.inf)
        log_inter = b + m[..., None]
        m_t = jnp.maximum(log_inter, jnp.max(log_d, axis=-1))
        s = jnp.einsum('bhtd,bhsd->bhts', qc, kc) * jnp.exp(log_d - m_t[..., None])
        inter = jnp.exp(log_inter - m_t)
        num = jnp.einsum('bhts,bhsv->bhtv', s, vc) + inter[..., None] * jnp.einsum('bhtd,bhdv->bhtv', qc, C)
        den = jnp.sum(s, axis=-1) + inter * jnp.einsum('bhtd,bhd->bht', qc, n)
        h = num / jnp.maximum(jnp.abs(den), jnp.exp(-m_t))[..., None]
        b_end = b[..., -1]
        log_w = b_end[..., None] - b + ic
        m_new = jnp.maximum(b_end + m, jnp.max(log_w, axis=-1))
        kw = kc * jnp.exp(log_w - m_new[..., None])[..., None]
        carry_decay = jnp.exp(b_end + m - m_new)
        C = carry_decay[..., None, None] * C + jnp.einsum('bhsd,bhsv->bhdv', kw, vc)
        n = carry_decay[..., None] * n + jnp.sum(kw, axis=-2)
        return (C, n, m_new), h

    init = (jnp.zeros((B, H, dk, dv), F32), jnp.zeros((B, H, dk), F32), jnp.zeros((B, H), F32))
    _, h = lax.scan(step, init, xs)
    return from_chunks(h)


def mlstm_mixer(x, w_in, b_if, conv_w, norm_g, w_out):
    B, T, _ = x.shape
    H, dk, dv = ML_HEADS, ML_DQK, ML_DV
    proj = x @ w_in
    qk, v, o, if_pre = jnp.split(proj, [2 * H * dk, 2 * H * dk + H * dv, 2 * H * dk + 2 * H * dv], axis=-1)
    qk = jax.nn.silu(causal_depthwise_conv(qk, conv_w))
    q, k = jnp.split(qk, 2, axis=-1)
    if_pre = if_pre.astype(F32) + b_if.astype(F32)
    if_pre = ML_GATE_CAP * jnp.tanh(if_pre / ML_GATE_CAP)
    i_log, f_pre = jnp.split(if_pre, 2, axis=-1)
    f_log = jax.nn.log_sigmoid(f_pre)
    q = q.reshape(B, T, H, dk).astype(F32) * (dk ** -0.5)
    h = mlstm_chunkwise(q, k.reshape(B, T, H, dk), v.reshape(B, T, H, dv), i_log, f_log)
    h = head_rms(h).reshape(B, T, H * dv) * norm_g.astype(F32)
    h = (h * jax.nn.sigmoid(o.astype(F32))).astype(x.dtype)
    return h @ w_out


def rwkv7_mixer(x, mu, w_rkv, w0, w1, w2, a0, a1, a2, g1, g2, k_k, k_a, r_k, ln_g, ln_b, w_out):
    B, T, D = x.shape
    H, N = RW_HEADS, RW_HEAD
    x_prev = jnp.pad(x, ((0, 0), (1, 0), (0, 0)))[:, :-1]
    xx = x_prev - x
    mix = lambda i: x + xx * mu[i]
    r = (mix(0) @ w_rkv[0]).astype(F32)
    k = (mix(1) @ w_rkv[1]).astype(F32)
    v = (mix(2) @ w_rkv[2]).astype(F32)
    w_log = -jax.nn.softplus(-(w0 + jnp.tanh(mix(3) @ w1) @ w2).astype(F32)) - 0.5
    decay = jnp.exp(-jnp.exp(w_log))
    a = jax.nn.sigmoid((a0 + (mix(4) @ a1) @ a2).astype(F32))
    g = (jax.nn.sigmoid(mix(5) @ g1) @ g2).astype(F32)
    heads = lambda t: t.reshape(B, T, H, N)
    kk = heads(k * k_k.astype(F32))
    kk = kk / jnp.maximum(jnp.sqrt(jnp.sum(kk * kk, axis=-1, keepdims=True)), 1e-12)
    k = k * (1.0 + (a - 1.0) * k_a.astype(F32))
    rh, kh, vh, ah = heads(r), heads(k), heads(v), heads(a)
    tm = lambda t: jnp.swapaxes(t, 0, 1)
    xs = (tm(rh), tm(heads(decay)), tm(kh), tm(vh), tm(-kk), tm(kk * ah))

    def step(S, xt):
        r_t, w_t, k_t, v_t, a_t, b_t = xt
        sa = jnp.einsum('bhij,bhj->bhi', S, a_t)
        S = S * w_t[:, :, None, :] + sa[..., :, None] * b_t[..., None, :] + v_t[..., :, None] * k_t[..., None, :]
        return S, jnp.einsum('bhij,bhj->bhi', S, r_t)

    _, ys = lax.scan(step, jnp.zeros((B, H, N, N), F32), xs)
    y = jnp.swapaxes(ys, 0, 1)
    y_mean = jnp.mean(y, axis=-1, keepdims=True)
    y_c = y - y_mean
    y = y_c * lax.rsqrt(jnp.mean(y_c * y_c, axis=-1, keepdims=True) + RW_LN_EPS)
    y = y.reshape(B, T, D) * ln_g.astype(F32) + ln_b.astype(F32)
    bonus = (jnp.sum(rh * kh * r_k.astype(F32), axis=-1, keepdims=True) * vh).reshape(B, T, D)
    return ((y + bonus) * g).astype(x.dtype) @ w_out


def retention_rotate(u, positions):
    d = u.shape[-1]
    inv_freq = 1.0 / (RT_ROPE_BASE ** jnp.linspace(0.0, 1.0, d // 2, dtype=F32))
    ang = positions.astype(F32)[:, :, None, None] * inv_freq
    cos, sin = jnp.cos(ang), jnp.sin(ang)
    u1, u2 = jnp.split(u, 2, axis=-1)
    return jnp.concatenate([u1 * cos - u2 * sin, u1 * sin + u2 * cos], axis=-1)


def retention_chunkwise(q, k, v):
    B, T, H, dk = q.shape
    dv = v.shape[-1]
    L = RT_CHUNK
    log_gamma = jnp.log(1.0 - 2.0 ** (-5.0 - jnp.arange(H, dtype=F32)))
    idx = jnp.arange(L, dtype=F32)
    diff = idx[:, None] - idx[None, :]
    d_mask = jnp.where(diff >= 0, jnp.exp(log_gamma[:, None, None] * jnp.maximum(diff, 0.0)), 0.0)
    q_decay = jnp.exp(log_gamma[:, None] * (idx[None, :] + 1.0))[None, :, :, None]
    k_decay = jnp.exp(log_gamma[:, None] * (L - 1.0 - idx[None, :]))[None, :, :, None]
    chunk_decay = jnp.exp(log_gamma * L)[None, :, None, None]
    xs = tuple(to_chunks(a, L) for a in (q, k, v))

    def step(R, xc):
        qc, kc, vc = xc
        inner = jnp.einsum('bhts,bhsv->bhtv', jnp.einsum('bhtd,bhsd->bhts', qc, kc) * d_mask, vc)
        cross = jnp.einsum('bhtd,bhdv->bhtv', qc, R) * q_decay
        R = chunk_decay * R + jnp.einsum('bhsd,bhsv->bhdv', kc * k_decay, vc)
        return R, inner + cross

    _, y = lax.scan(step, jnp.zeros((B, H, dk, dv), F32), xs)
    return from_chunks(y)


def retention_mixer(x, positions, w_in, w_out):
    B, T, _ = x.shape
    H, dk, dv = RT_HEADS, RT_DK, RT_DV
    q, k, v, g = jnp.split(x @ w_in, [H * dk, 2 * H * dk, 2 * H * dk + H * dv], axis=-1)
    q = retention_rotate(q.reshape(B, T, H, dk).astype(F32), positions)
    k = retention_rotate(k.reshape(B, T, H, dk).astype(F32), positions) * (dk ** -0.5)
    y = retention_chunkwise(q, k, v.reshape(B, T, H, dv).astype(F32))
    y = head_rms(y).reshape(B, T, H * dv)
    return (jax.nn.silu(g.astype(F32)) * y).astype(x.dtype) @ w_out


def _dense(k, shape, fan_in, scale=1.0):
    return jax.random.normal(k, shape, F32) * (scale * fan_in ** -0.5)


def setup_inputs(seed: int = 0) -> dict:
    key = jax.random.key(seed)
    ks = jax.random.split(key, 40)
    D = D_MODEL
    nm, nr, nt = N_ML_LAYERS, N_RW_LAYERS, N_RT_LAYERS
    nrm = lambda k, shape, s: jax.random.normal(k, shape, F32) * s
    x = jax.random.normal(ks[0], (BATCH, SEQ, D), F32)
    positions = (jnp.arange(SEQ, dtype=jnp.int32)[None, :]
                 + jax.random.randint(ks[1], (BATCH, 1), 0, SEQ, dtype=jnp.int32))
    norm_g = 1.0 + nrm(ks[2], (DEPTH, 6, D), 0.01)
    ffn_w_gu = _dense(ks[3], (DEPTH, 2, D, 2 * D_FF), D)
    ffn_w_down = _dense(ks[4], (DEPTH, 2, D_FF, D), D_FF)
    ml_w_in = _dense(ks[5], (nm, D, ML_IN), D)
    ml_b_if = jnp.concatenate([nrm(ks[6], (nm, ML_HEADS), 0.01),
                               jnp.linspace(3.0, 6.0, ML_HEADS, dtype=F32)[None, :] + nrm(ks[7], (nm, ML_HEADS), 0.01)],
                              axis=-1)
    ml_conv_w = _dense(ks[8], (nm, ML_CONV, 2 * ML_HEADS * ML_DQK), ML_CONV)
    ml_norm_g = 1.0 + nrm(ks[9], (nm, ML_HEADS * ML_DV), 0.01)
    ml_w_out = _dense(ks[10], (nm, ML_HEADS * ML_DV, D), ML_HEADS * ML_DV)
    rw_mu = jax.random.uniform(ks[11], (nr, 6, D), F32)
    rw_w_rkv = _dense(ks[12], (nr, 3, D, D), D)
    rw_w0 = jnp.linspace(-6.5, -1.5, D, dtype=F32)[None, :] + nrm(ks[13], (nr, D), 0.1)
    rw_w1 = _dense(ks[14], (nr, D, RW_LORA_W), D)
    rw_w2 = _dense(ks[15], (nr, RW_LORA_W, D), RW_LORA_W, 0.1)
    rw_a0 = nrm(ks[16], (nr, D), 0.01)
    rw_a1 = _dense(ks[17], (nr, D, RW_LORA_A), D)
    rw_a2 = _dense(ks[18], (nr, RW_LORA_A, D), RW_LORA_A, 0.1)
    rw_g1 = _dense(ks[19], (nr, D, RW_LORA_G), D)
    rw_g2 = _dense(ks[20], (nr, RW_LORA_G, D), RW_LORA_G)
    rw_k_k = 0.85 + nrm(ks[21], (nr, D), 0.01)
    rw_k_a = 1.0 + nrm(ks[22], (nr, D), 0.01)
    rw_r_k = nrm(ks[23], (nr, RW_HEADS, RW_HEAD), 0.1)
    rw_ln_g = 1.0 + nrm(ks[24], (nr, D), 0.01)
    rw_ln_b = nrm(ks[25], (nr, D), 0.01)
    rw_w_out = _dense(ks[26], (nr, D, D), D)
    rt_w_in = _dense(ks[27], (nt, D, RT_IN), D)
    rt_w_out = _dense(ks[28], (nt, RT_HEADS * RT_DV, D), RT_HEADS * RT_DV)
    return {'x': x, 'positions': positions, 'norm_g': norm_g, 'ffn_w_gu': ffn_w_gu, 'ffn_w_down': ffn_w_down,
            'ml_w_in': ml_w_in, 'ml_b_if': ml_b_if, 'ml_conv_w': ml_conv_w, 'ml_norm_g': ml_norm_g, 'ml_w_out': ml_w_out,
            'rw_mu': rw_mu, 'rw_w_rkv': rw_w_rkv, 'rw_w0': rw_w0, 'rw_w1': rw_w1, 'rw_w2': rw_w2,
            'rw_a0': rw_a0, 'rw_a1': rw_a1, 'rw_a2': rw_a2, 'rw_g1': rw_g1, 'rw_g2': rw_g2,
            'rw_k_k': rw_k_k, 'rw_k_a': rw_k_a, 'rw_r_k': rw_r_k, 'rw_ln_g': rw_ln_g, 'rw_ln_b': rw_ln_b,
            'rw_w_out': rw_w_out, 'rt_w_in': rt_w_in, 'rt_w_out': rt_w_out}


def reference(x, positions, norm_g, ffn_w_gu, ffn_w_down,
              ml_w_in, ml_b_if, ml_conv_w, ml_norm_g, ml_w_out,
              rw_mu, rw_w_rkv, rw_w0, rw_w1, rw_w2, rw_a0, rw_a1, rw_a2, rw_g1, rw_g2,
              rw_k_k, rw_k_a, rw_r_k, rw_ln_g, rw_ln_b, rw_w_out, rt_w_in, rt_w_out):
    for layer in range(DEPTH):
        g = norm_g[layer]
        h = swiglu_ffn(rms_norm(x, g[0]), ffn_w_gu[layer, 0], ffn_w_down[layer, 0])
        x = x + HALF_STEP * rms_norm(h, g[1])
        h = rms_norm(x, g[2])
        kind, j = layer % N_MIXERS, layer // N_MIXERS
        if kind == 0:
            h = mlstm_mixer(h, ml_w_in[j], ml_b_if[j], ml_conv_w[j], ml_norm_g[j], ml_w_out[j])
        elif kind == 1:
            h = rwkv7_mixer(h, rw_mu[j], rw_w_rkv[j], rw_w0[j], rw_w1[j], rw_w2[j], rw_a0[j], rw_a1[j], rw_a2[j],
                            rw_g1[j], rw_g2[j], rw_k_k[j], rw_k_a[j], rw_r_k[j], rw_ln_g[j], rw_ln_b[j], rw_w_out[j])
        else:
            h = retention_mixer(h, positions, rt_w_in[j], rt_w_out[j])
        x = x + rms_norm(h, g[3])
        h = swiglu_ffn(rms_norm(x, g[4]), ffn_w_gu[layer, 1], ffn_w_down[layer, 1])
        x = x + HALF_STEP * rms_norm(h, g[5])
    return x
```

```python
import functools
import math

import jax
import jax.numpy as jnp
from jax import lax
from jax.experimental import pallas as pl
from jax.experimental.pallas import tpu as pltpu

F32 = jnp.float32
BF16 = jnp.bfloat16

D_MODEL = 1024
D_FF = 2816
NORM_EPS = 1e-6
HALF_STEP = 0.5

ML_HEADS = 4
ML_DQK = 128
ML_DV = 256
ML_CHUNK = 64
ML_GATE_CAP = 15.0

RW_HEAD = 64
RW_PAIRS = D_MODEL // (2 * RW_HEAD)
RW_CHUNK = 64
RW_LN_EPS = 64e-5

RT_HEADS = 4
RT_DK = 256
RT_DV = 512
RT_CHUNK = 128
RT_ROPE_BASE = 10000.0

LANES = 128
VMEM_LIMIT = 56 * 1024 * 1024


def _cparams(sem):
    return pltpu.CompilerParams(dimension_semantics=sem, vmem_limit_bytes=VMEM_LIMIT)


def _rms(x, g):
    return x * lax.rsqrt(jnp.mean(x * x, axis=-1, keepdims=True) + NORM_EPS) * g


def _dot(a, b):
    return jnp.dot(a, b, preferred_element_type=F32)


def _dot_nt(a, b):
    return lax.dot_general(a, b, (((1,), (1,)), ((), ())), preferred_element_type=F32)


def _dot_tn(a, b):
    return lax.dot_general(a, b, (((0,), (0,)), ((), ())), preferred_element_type=F32)


def _split3(x):
    hi = x.astype(BF16)
    r1 = x - hi.astype(F32)
    mid = r1.astype(BF16)
    lo = (r1 - mid.astype(F32)).astype(BF16)
    return hi, mid, lo


def _sigmoid(x):
    return 1.0 / (1.0 + jnp.exp(-x))


def _softplus(x):
    return jnp.maximum(x, 0.0) + jnp.log(1.0 + jnp.exp(-jnp.abs(x)))


def _ffn_kernel(x_ref, gpre_ref, wg_ref, wu_ref, wd_ref, gpost_ref, o_ref, xn_ref, acc_ref):
    j = pl.program_id(1)

    @pl.when(j == 0)
    def _():
        xn_ref[...] = _rms(x_ref[...], gpre_ref[...]).astype(BF16)

    xn = xn_ref[...]
    gate = _dot(xn, wg_ref[...])
    up = _dot(xn, wu_ref[...])
    act = (gate * _sigmoid(gate) * up).astype(BF16)
    contrib = _dot(act, wd_ref[...])

    @pl.when(j == 0)
    def _():
        acc_ref[...] = contrib

    @pl.when(j > 0)
    def _():
        acc_ref[...] += contrib

    @pl.when(j == pl.num_programs(1) - 1)
    def _():
        o_ref[...] = x_ref[...] + HALF_STEP * _rms(acc_ref[...], gpost_ref[...])


def ffn_sublayer(x, g_pre, w_gu, w_down, g_post, *, tm=1024, tf=256):
    n, d = x.shape
    tm = min(tm, n)
    f = w_down.shape[0]
    nf = f // tf
    return pl.pallas_call(
        _ffn_kernel,
        grid=(n // tm, nf),
        in_specs=[
            pl.BlockSpec((tm, d), lambda i, j: (i, 0)),
            pl.BlockSpec((1, d), lambda i, j: (0, 0)),
            pl.BlockSpec((d, tf), lambda i, j: (0, j)),
            pl.BlockSpec((d, tf), lambda i, j: (0, j + nf)),
            pl.BlockSpec((tf, d), lambda i, j: (j, 0)),
            pl.BlockSpec((1, d), lambda i, j: (0, 0)),
        ],
        out_specs=pl.BlockSpec((tm, d), lambda i, j: (i, 0)),
        out_shape=jax.ShapeDtypeStruct((n, d), F32),
        scratch_shapes=[pltpu.VMEM((tm, d), BF16), pltpu.VMEM((tm, d), F32)],
        compiler_params=_cparams(("parallel", "arbitrary")),
        name="ffn",
    )(x, g_pre, w_gu, w_gu, w_down, g_post)


def _norm_matmul_kernel(x_ref, g_ref, w_ref, o_ref, xn_ref):
    @pl.when(pl.program_id(1) == 0)
    def _():
        xn_ref[...] = _rms(x_ref[...], g_ref[...]).astype(BF16)

    o_ref[...] = _dot(xn_ref[...], w_ref[...]).astype(o_ref.dtype)


def norm_matmul(x, g, w, out_dtype, *, tm=1024, tn=512):
    n, d = x.shape
    nout = w.shape[1]
    tm = min(tm, n)
    tn = min(tn, nout)
    return pl.pallas_call(
        _norm_matmul_kernel,
        grid=(n // tm, nout // tn),
        in_specs=[
            pl.BlockSpec((tm, d), lambda i, j: (i, 0)),
            pl.BlockSpec((1, d), lambda i, j: (0, 0)),
            pl.BlockSpec((d, tn), lambda i, j: (0, j)),
        ],
        out_specs=pl.BlockSpec((tm, tn), lambda i, j: (i, j)),
        out_shape=jax.ShapeDtypeStruct((n, nout), out_dtype),
        scratch_shapes=[pltpu.VMEM((tm, d), BF16)],
        compiler_params=_cparams(("parallel", "arbitrary")),
        name="norm_matmul",
    )(x, g, w)


def _out_proj_kernel(h_ref, w_ref, x_ref, g_ref, o_ref):
    y = _dot(h_ref[...], w_ref[...])
    o_ref[...] = x_ref[...] + _rms(y, g_ref[...])


def out_proj_residual(h, w, x, g, *, tm=1024):
    n, k = h.shape
    tm = min(tm, n)
    d = w.shape[1]
    return pl.pallas_call(
        _out_proj_kernel,
        grid=(n // tm,),
        in_specs=[
            pl.BlockSpec((tm, k), lambda i: (i, 0)),
            pl.BlockSpec((k, d), lambda i: (0, 0)),
            pl.BlockSpec((tm, d), lambda i: (i, 0)),
            pl.BlockSpec((1, d), lambda i: (0, 0)),
        ],
        out_specs=pl.BlockSpec((tm, d), lambda i: (i, 0)),
        out_shape=jax.ShapeDtypeStruct((n, d), F32),
        compiler_params=_cparams(("parallel",)),
        name="out_proj",
    )(h, w, x, g)


def _out_proj_pairs_kernel(h_ref, w_ref, x_ref, g_ref, o_ref):
    h = jnp.concatenate([h_ref[p] for p in range(h_ref.shape[0])], axis=-1)
    y = _dot(h, w_ref[...])
    o_ref[...] = x_ref[...] + _rms(y, g_ref[...])


def out_proj_pairs_residual(h, w, x, g, *, tm=1024):
    p, n, l = h.shape
    tm = min(tm, n)
    d = w.shape[1]
    return pl.pallas_call(
        _out_proj_pairs_kernel,
        grid=(n // tm,),
        in_specs=[
            pl.BlockSpec((p, tm, l), lambda i: (0, i, 0)),
            pl.BlockSpec((p * l, d), lambda i: (0, 0)),
            pl.BlockSpec((tm, d), lambda i: (i, 0)),
            pl.BlockSpec((1, d), lambda i: (0, 0)),
        ],
        out_specs=pl.BlockSpec((tm, d), lambda i: (i, 0)),
        out_shape=jax.ShapeDtypeStruct((n, d), F32),
        compiler_params=_cparams(("parallel",)),
        name="out_proj_pairs",
    )(h, w, x, g)


def _mlstm_kernel(q_ref, k_ref, v_ref, o_ref, ig_ref, fg_ref, ng_ref, out_ref, c_ref, n_ref):
    L = ML_CHUNK
    nc = ig_ref.shape[2]
    c_ref[...] = jnp.zeros_like(c_ref)
    n_ref[...] = jnp.zeros_like(n_ref)
    row = lax.broadcasted_iota(jnp.int32, (L, L), 0)
    col = lax.broadcasted_iota(jnp.int32, (L, L), 1)
    tri = row >= col
    eye = row == col
    ones = jnp.ones((L, L), BF16)
    ng = ng_ref[...]

    def body(c, m):
        t0 = pl.multiple_of(c * L, L)
        qc = q_ref[0, pl.ds(t0, L), :]
        kc = k_ref[0, pl.ds(t0, L), :]
        vc = v_ref[0, pl.ds(t0, L), :]
        irow = ig_ref[0, 0, pl.ds(c, 1), :]
        frow = fg_ref[0, 0, pl.ds(c, 1), :]
        fh, fm, fl = _split3(jnp.where(tri, frow, 0.0))
        bmat = _dot(fh, ones) + _dot(fm, ones) + _dot(fl, ones)
        brow = _dot_nt(ones, fh) + _dot_nt(ones, fm) + _dot_nt(ones, fl)
        log_d = jnp.where(tri, bmat - brow + irow, -jnp.inf)
        bcol = bmat[:, 0:1]
        log_inter = bcol + m
        m_t = jnp.maximum(log_inter, jnp.max(log_d, axis=-1, keepdims=True))
        s = _dot_nt(qc, kc) * jnp.exp(log_d - m_t)
        inter = jnp.exp(log_inter - m_t)
        cmat = c_ref[...]
        num = _dot(s.astype(BF16), vc) + inter * _dot(qc, cmat.astype(BF16))
        qn = jnp.sum(qc.astype(F32) * n_ref[...], axis=-1, keepdims=True)
        den = jnp.sum(s, axis=-1, keepdims=True) + inter * qn
        h = num / jnp.maximum(jnp.abs(den), jnp.exp(-m_t))
        b_end = bmat[L - 1:L, 0:1]
        log_w = b_end - brow[0:1, :] + irow
        m_new = jnp.maximum(b_end + m, jnp.max(log_w, axis=-1, keepdims=True))
        w_row = jnp.exp(log_w - m_new)
        carry_decay = jnp.exp(b_end + m - m_new)
        wdiag = jnp.where(eye, w_row, 0.0).astype(BF16)
        wv = _dot(wdiag, vc).astype(BF16)
        c_ref[...] = carry_decay * cmat + _dot_tn(kc, wv)
        n_add = _dot(jnp.broadcast_to(w_row, (8, L)).astype(BF16), kc)[0:1, :]
        n_ref[...] = carry_decay * n_ref[...] + n_add
        hn = h * lax.rsqrt(jnp.mean(h * h, axis=-1, keepdims=True) + NORM_EPS) * ng
        og = o_ref[0, pl.ds(t0, L), :].astype(F32)
        out_ref[0, pl.ds(t0, L), :] = (hn * _sigmoid(og)).astype(out_ref.dtype)
        return m_new

    lax.fori_loop(0, nc, body, jnp.zeros((1, 1), F32))


def mlstm_scan(q, k, v, o, ig, fg, ng):
    b, t, _ = q.shape
    h, dk, dv, L = ML_HEADS, ML_DQK, ML_DV, ML_CHUNK
    nc = t // L
    return pl.pallas_call(
        _mlstm_kernel,
        grid=(b, h),
        in_specs=[
            pl.BlockSpec((1, t, dk), lambda i, j: (i, 0, j)),
            pl.BlockSpec((1, t, dk), lambda i, j: (i, 0, j)),
            pl.BlockSpec((1, t, dv), lambda i, j: (i, 0, j)),
            pl.BlockSpec((1, t, dv), lambda i, j: (i, 0, j)),
            pl.BlockSpec((1, 1, nc, L), lambda i, j: (i, j, 0, 0)),
            pl.BlockSpec((1, 1, nc, L), lambda i, j: (i, j, 0, 0)),
            pl.BlockSpec((1, dv), lambda i, j: (0, j)),
        ],
        out_specs=pl.BlockSpec((1, t, dv), lambda i, j: (i, 0, j)),
        out_shape=jax.ShapeDtypeStruct((b, t, h * dv), BF16),
        scratch_shapes=[pltpu.VMEM((dk, dv), F32), pltpu.VMEM((1, dk), F32)],
        compiler_params=_cparams(("parallel", "parallel")),
        name="mlstm_scan",
    )(q, k, v, o, ig, fg, ng)


def mlstm_sublayer(x, g_pre, w_in, b_if, conv_w, norm_g, w_out, g_post, b, t):
    h, dk, dv, L = ML_HEADS, ML_DQK, ML_DV, ML_CHUNK
    n_main = 2 * h * dk + 2 * h * dv
    w_main = w_in[:, :n_main].astype(BF16)
    w_if = jnp.pad(w_in[:, n_main:], ((0, 0), (0, LANES - 2 * h))).astype(BF16)
    proj = norm_matmul(x, g_pre, w_main, BF16)
    if_pre = norm_matmul(x, g_pre, w_if, F32)[:, :2 * h]
    qk = proj[:, :2 * h * dk].astype(F32).reshape(b, t, 2 * h * dk)
    kw = conv_w.shape[0]
    qk_pad = jnp.pad(qk, ((0, 0), (kw - 1, 0), (0, 0)))
    conv = sum(qk_pad[:, i:i + t, :] * conv_w[i] for i in range(kw))
    conv = conv * jax.nn.sigmoid(conv)
    q = (conv[..., :h * dk] * (dk ** -0.5)).astype(BF16)
    k = conv[..., h * dk:].astype(BF16)
    v = proj[:, 2 * h * dk:2 * h * dk + h * dv].reshape(b, t, h * dv)
    o = proj[:, 2 * h * dk + h * dv:].reshape(b, t, h * dv)
    if_pre = ML_GATE_CAP * jnp.tanh((if_pre + b_if) / ML_GATE_CAP)
    i_log = if_pre[:, :h]
    f_log = jax.nn.log_sigmoid(if_pre[:, h:])
    to_rows = lambda a: jnp.transpose(a.reshape(b, t, h), (0, 2, 1)).reshape(b, h, t // L, L)
    hs = mlstm_scan(q, k, v, o, to_rows(i_log), to_rows(f_log), norm_g[None, :])
    return out_proj_residual(hs.reshape(b * t, h * dv), w_out.astype(BF16), x, g_post)


def _retention_kernel(lg_ref, q_ref, k_ref, v_ref, g_ref, cos_ref, sin_ref, out_ref, r_ref):
    L = RT_CHUNK
    half = RT_DK // 2
    nc = q_ref.shape[1] // L
    log_gamma = lg_ref[pl.program_id(1)]
    r_ref[...] = jnp.zeros_like(r_ref)
    row = lax.broadcasted_iota(jnp.int32, (L, L), 0)
    col = lax.broadcasted_iota(jnp.int32, (L, L), 1)
    diff = (row - col).astype(F32)
    d_mask = jnp.where(row >= col, jnp.exp(log_gamma * jnp.maximum(diff, 0.0)), 0.0)
    idx = lax.broadcasted_iota(jnp.int32, (L, 1), 0).astype(F32)
    q_decay = jnp.exp(log_gamma * (idx + 1.0))
    k_decay = jnp.exp(log_gamma * (L - 1.0 - idx))
    chunk_decay = jnp.exp(jnp.full((1, 1), L, F32) * log_gamma)

    def rotate(u, cos, sin):
        u1, u2 = u[:, :half], u[:, half:]
        return jnp.concatenate([u1 * cos - u2 * sin, u1 * sin + u2 * cos], axis=-1)

    def body(c, carry):
        t0 = pl.multiple_of(c * L, L)
        cos = cos_ref[0, pl.ds(t0, L), :]
        sin = sin_ref[0, pl.ds(t0, L), :]
        qr = rotate(q_ref[0, pl.ds(t0, L), :].astype(F32), cos, sin)
        kr = rotate(k_ref[0, pl.ds(t0, L), :].astype(F32), cos, sin) * (RT_DK ** -0.5)
        vc = v_ref[0, pl.ds(t0, L), :]
        qb = qr.astype(BF16)
        s = _dot_nt(qb, kr.astype(BF16)) * d_mask
        rmat = r_ref[...]
        y = _dot(s.astype(BF16), vc) + _dot(qb, rmat.astype(BF16)) * q_decay
        r_ref[...] = chunk_decay * rmat + _dot_tn((kr * k_decay).astype(BF16), vc)
        yn = y * lax.rsqrt(jnp.mean(y * y, axis=-1, keepdims=True) + NORM_EPS)
        gc = g_ref[0, pl.ds(t0, L), :].astype(F32)
        out_ref[0, pl.ds(t0, L), :] = (gc * _sigmoid(gc) * yn).astype(out_ref.dtype)
        return carry

    lax.fori_loop(0, nc, body, 0)


def retention_scan(proj, cos, sin, log_gamma):
    b, t, _ = proj.shape
    h, dk, dv = RT_HEADS, RT_DK, RT_DV
    kq = (h * dk) // dk
    kv = (2 * h * dk) // dv
    return pl.pallas_call(
        _retention_kernel,
        grid=(b, h),
        in_specs=[
            pl.BlockSpec(memory_space=pltpu.SMEM),
            pl.BlockSpec((1, t, dk), lambda i, j: (i, 0, j)),
            pl.BlockSpec((1, t, dk), lambda i, j: (i, 0, kq + j)),
            pl.BlockSpec((1, t, dv), lambda i, j: (i, 0, kv + j)),
            pl.BlockSpec((1, t, dv), lambda i, j: (i, 0, kv + h + j)),
            pl.BlockSpec((1, t, dk // 2), lambda i, j: (i, 0, 0)),
            pl.BlockSpec((1, t, dk // 2), lambda i, j: (i, 0, 0)),
        ],
        out_specs=pl.BlockSpec((1, t, dv), lambda i, j: (i, 0, j)),
        out_shape=jax.ShapeDtypeStruct((b, t, h * dv), BF16),
        scratch_shapes=[pltpu.VMEM((dk, dv), F32)],
        compiler_params=_cparams(("parallel", "parallel")),
        name="retention_scan",
    )(log_gamma, proj, proj, proj, proj, cos, sin)


def retention_sublayer(x, positions, g_pre, w_in, w_out, g_post, b, t):
    h, dk, dv = RT_HEADS, RT_DK, RT_DV
    proj = norm_matmul(x, g_pre, w_in.astype(BF16), BF16).reshape(b, t, -1)
    inv_freq = 1.0 / (RT_ROPE_BASE ** jnp.linspace(0.0, 1.0, dk // 2, dtype=F32))
    ang = positions.astype(F32)[:, :, None] * inv_freq
    log_gamma = jnp.log(1.0 - 2.0 ** (-5.0 - jnp.arange(h, dtype=F32)))
    y = retention_scan(proj, jnp.cos(ang), jnp.sin(ang), log_gamma)
    return out_proj_residual(y.reshape(b * t, h * dv), w_out.astype(BF16), x, g_post)


def _rwkv_proj_kernel(x_ref, xp_ref, g_ref, mu_ref, wrkv_ref, w0_ref, w1_ref, w2_ref,
                      a0_ref, a1_ref, a2_ref, g1_ref, g2_ref,
                      r_ref, k_ref, v_ref, lw_ref, a_ref, gate_ref, *, seq_tiles):
    i = pl.program_id(0)
    tm = x_ref.shape[0]
    g = g_ref[...]
    h = _rms(x_ref[...], g)
    prev = _rms(xp_ref[...], g)[7:8, :]
    prev = prev * (i % seq_tiles != 0).astype(F32)
    rid = lax.broadcasted_iota(jnp.int32, (tm, 1), 0)
    h_prev = jnp.where(rid == 0, prev, pltpu.roll(h, 1, axis=0))
    xx = h_prev - h
    mix = lambda j: (h + xx * mu_ref[j:j + 1, :]).astype(BF16)

    def put(ref, val):
        for p in range(ref.shape[0]):
            ref[p] = val[:, p * LANES:(p + 1) * LANES].astype(ref.dtype)

    put(r_ref, _dot(mix(0), wrkv_ref[0]))
    put(k_ref, _dot(mix(1), wrkv_ref[1]))
    put(v_ref, _dot(mix(2), wrkv_ref[2]))
    wl = jnp.tanh(_dot(mix(3), w1_ref[...])).astype(BF16)
    w_log = -_softplus(-(w0_ref[...] + _dot(wl, w2_ref[...]))) - 0.5
    put(lw_ref, -jnp.exp(w_log))
    al = _dot(mix(4), a1_ref[...]).astype(BF16)
    put(a_ref, _sigmoid(a0_ref[...] + _dot(al, a2_ref[...])))
    gl = _sigmoid(_dot(mix(5), g1_ref[...])).astype(BF16)
    put(gate_ref, _dot(gl, g2_ref[...]))


def rwkv_proj(x, g_pre, mu, w_rkv, w0, w1, w2, a0, a1, a2, g1, g2, t, *, tm=512):
    n, d = x.shape
    tm = min(tm, t)
    p = d // LANES
    full = lambda a: pl.BlockSpec(a.shape, lambda i: (0,) * a.ndim)
    outs = pl.BlockSpec((p, tm, LANES), lambda i: (0, i, 0))
    args = (g_pre, mu, w_rkv, w0, w1, w2, a0, a1, a2, g1, g2)
    shp = lambda dt: jax.ShapeDtypeStruct((p, n, LANES), dt)
    return pl.pallas_call(
        functools.partial(_rwkv_proj_kernel, seq_tiles=t // tm),
        grid=(n // tm,),
        in_specs=[
            pl.BlockSpec((tm, d), lambda i: (i, 0)),
            pl.BlockSpec((8, d), lambda i: (jnp.maximum(i * (tm // 8) - 1, 0), 0)),
        ] + [full(a) for a in args],
        out_specs=[outs] * 6,
        out_shape=[shp(BF16), shp(BF16), shp(BF16), shp(F32), shp(BF16), shp(BF16)],
        compiler_params=_cparams(("parallel",)),
        name="rwkv_proj",
    )(x, x, *args)


def _rwkv_scan_kernel(r_ref, k_ref, v_ref, lw_ref, a_ref, gate_ref, kk_ref, ka_ref, rk_ref,
                      lng_ref, lnb_ref, out_ref, s_ref):
    C = RW_CHUNK
    N = RW_HEAD
    C2 = 2 * C
    nc = r_ref.shape[1] // C
    s_ref[...] = jnp.zeros_like(s_ref)
    lane = lax.broadcasted_iota(jnp.int32, (1, LANES), 1)
    in_a = lane < N
    m_a = in_a.astype(F32)
    m_b = 1.0 - m_a
    rr = lax.broadcasted_iota(jnp.int32, (C2, C2), 0)
    cc = lax.broadcasted_iota(jnp.int32, (C2, C2), 1)
    same = (rr // C) == (cc // C)
    strict = same & ((rr % C) > (cc % C))
    incl = same & ((rr % C) >= (cc % C))
    eye2 = (rr == cc).astype(F32)
    level_masks = []
    for lvl in range(int(math.log2(C))):
        rb, cb = (rr % C) >> lvl, (cc % C) >> lvl
        level_masks.append(same & ((rb & 1) == 1) & (cb == rb - 1))
    tr =lax.broadcasted_iota(jnp.int32, (C, C), 0)
    tc = lax.broadcasted_iota(jnp.int32, (C, C), 1)
    tril = (tr >= tc).astype(BF16)
    kk_w, ka_w, rk_w = kk_ref[0], ka_ref[0], rk_ref[0]
    ln_g, ln_b = lng_ref[0], lnb_ref[0]

    def seg_sum(z):
        sa = jnp.sum(z * m_a, axis=-1, keepdims=True)
        sb = jnp.sum(z * m_b, axis=-1, keepdims=True)
        return jnp.where(in_a, sa, sb)

    def stack(z):
        return jnp.concatenate([z * m_a, z * m_b], axis=0)

    def body(c, carry):
        t0 = pl.multiple_of(c * C, C)
        sl = pl.ds(t0, C)
        r = r_ref[0, sl, :].astype(F32)
        k = k_ref[0, sl, :].astype(F32)
        v = v_ref[0, sl, :].astype(F32)
        lw = lw_ref[0, sl, :]
        a_sig = a_ref[0, sl, :].astype(F32)
        kk = k * kk_w
        kk = kk / jnp.maximum(jnp.sqrt(seg_sum(kk * kk)), 1e-12)
        k = k * (1.0 + (a_sig - 1.0) * ka_w)
        a_ = -kk
        b_ = kk * a_sig
        lh, lm, ll = _split3(lw)
        gcum = _dot(tril, lh) + _dot(tril, lm) + _dot(tril, ll)
        g_end = gcum[C - 1:C, :]
        e_pos = jnp.exp(gcum)
        e_neg = jnp.exp(-gcum)
        e_rest = jnp.exp(g_end - gcum)
        a_h = stack(a_ * jnp.exp(gcum - lw))
        r_h = stack(r * e_pos)
        b_h = stack(b_ * e_neg)
        k_h = stack(k * e_neg)
        b_t = stack(b_ * e_rest)
        k_t = stack(k * e_rest)
        v_s = stack(v)
        lhs = jnp.concatenate([a_h, r_h], axis=0).astype(BF16)
        rhs = jnp.concatenate([b_h, k_h], axis=0).astype(BF16)
        sc = _dot_nt(lhs, rhs)
        a_ab = jnp.where(strict, sc[:C2, :C2], 0.0)
        a_ak = jnp.where(strict, sc[:C2, C2:], 0.0)
        a_rb = jnp.where(incl, sc[C2:, :C2], 0.0)
        a_rk = jnp.where(incl, sc[C2:, C2:], 0.0)
        tinv = eye2 + jnp.where(level_masks[0], a_ab, 0.0)
        for mk in level_masks[1:]:
            tb = tinv.astype(BF16)
            l21 = jnp.where(mk, a_ab, 0.0).astype(BF16)
            tinv = tinv + _dot(_dot(tb, l21).astype(BF16), tb)
        v_sb = v_s.astype(BF16)
        w1 = _dot(a_ak.astype(BF16), v_sb)
        au = _dot(tinv.astype(BF16), jnp.concatenate([a_h, w1], axis=1).astype(BF16))
        a_t, u0 = au[:, :LANES], au[:, LANES:]
        a_tb = a_t.astype(BF16)
        u0b = u0.astype(BF16)
        p_t = _dot_tn(a_tb, b_t.astype(BF16))
        z_t = _dot_tn(jnp.concatenate([u0b, v_sb], axis=0),
                      jnp.concatenate([b_t, k_t], axis=0).astype(BF16))
        a_rbb = a_rb.astype(BF16)
        r_bar = r_h + _dot(a_rbb, a_tb)
        y_bar = _dot(a_rbb, u0b) + _dot(a_rk.astype(BF16), v_sb)
        s0 = s_ref[...]
        s0b = s0.astype(BF16)
        ys = _dot_nt(r_bar.astype(BF16), s0b) + y_bar
        s_ref[...] = s0 * jnp.exp(g_end) + _dot(s0b, p_t.astype(BF16)) + z_t
        y = ys[:C, :] + ys[C:, :]
        mean = seg_sum(y) * (1.0 / N)
        yc = y - mean
        var = seg_sum(yc * yc) * (1.0 / N)
        yn = yc * lax.rsqrt(var + RW_LN_EPS) * ln_g + ln_b
        bonus = seg_sum(r * k * rk_w) * v
        gate = gate_ref[0, sl, :].astype(F32)
        out_ref[0, sl, :] = ((yn + bonus) * gate).astype(out_ref.dtype)
        return carry

    lax.fori_loop(0, nc, body, 0)


def rwkv_scan(r, k, v, lw, a, gate, k_k, k_a, r_k, ln_g, ln_b, b, t):
    p, n, l = r.shape
    seq = lambda: pl.BlockSpec((1, t, l), lambda i, j: (j, i, 0))
    par = lambda: pl.BlockSpec((1, 1, l), lambda i, j: (j, 0, 0))
    return pl.pallas_call(
        _rwkv_scan_kernel,
        grid=(b, p),
        in_specs=[seq()] * 6 + [par()] * 5,
        out_specs=seq(),
        out_shape=jax.ShapeDtypeStruct((p, n, l), BF16),
        scratch_shapes=[pltpu.VMEM((l, l), F32)],
        compiler_params=_cparams(("parallel", "parallel")),
        name="rwkv_scan",
    )(r, k, v, lw, a, gate, k_k, k_a, r_k, ln_g, ln_b)


def rwkv_sublayer(x, g_pre, mu, w_rkv, w0, w1, w2, a0, a1, a2, g1, g2,
                  k_k, k_a, r_k, ln_g, ln_b, w_out, g_post, b, t):
    bf = lambda a: a.astype(BF16)
    row = lambda a: a.reshape(1, -1)
    r, k, v, lw, a, gate = rwkv_proj(x, g_pre, mu, bf(w_rkv), row(w0), bf(w1), bf(w2),
                                     row(a0), bf(a1), bf(a2), bf(g1), bf(g2), t)
    par = lambda a: a.reshape(RW_PAIRS, 1, LANES)
    y = rwkv_scan(r, k, v, lw, a, gate, par(k_k), par(k_a), par(r_k), par(ln_g), par(ln_b), b, t)
    return out_proj_pairs_residual(y, bf(w_out), x, g_post)


def kernel(x, positions, norm_g, ffn_w_gu, ffn_w_down, ml_w_in, ml_b_if, ml_conv_w, ml_norm_g, ml_w_out,
           rw_mu, rw_w_rkv, rw_w0, rw_w1, rw_w2, rw_a0, rw_a1, rw_a2, rw_g1, rw_g2,
           rw_k_k, rw_k_a, rw_r_k, rw_ln_g, rw_ln_b, rw_w_out, rt_w_in, rt_w_out):
    b, t, d = x.shape
    depth = norm_g.shape[0]
    xf = x.reshape(b * t, d)
    for layer in range(depth):
        g = norm_g[layer][:, None, :]
        xf = ffn_sublayer(xf, g[0], ffn_w_gu[layer, 0].astype(BF16), ffn_w_down[layer, 0].astype(BF16), g[1])
        kind, j = layer % 3, layer // 3
        if kind == 0:
            xf = mlstm_sublayer(xf, g[2], ml_w_in[j], ml_b_if[j], ml_conv_w[j], ml_norm_g[j], ml_w_out[j],
                                g[3], b, t)
        elif kind == 1:
            xf = rwkv_sublayer(xf, g[2], rw_mu[j], rw_w_rkv[j], rw_w0[j], rw_w1[j], rw_w2[j],
                               rw_a0[j], rw_a1[j], rw_a2[j], rw_g1[j], rw_g2[j],
                               rw_k_k[j], rw_k_a[j], rw_r_k[j].reshape(-1), rw_ln_g[j], rw_ln_b[j],
                               rw_w_out[j], g[3], b, t)
        else:
            xf = retention_sublayer(xf, positions, g[2], rt_w_in[j], rt_w_out[j], g[3], b, t)
        xf = ffn_sublayer(xf, g[4], ffn_w_gu[layer, 1].astype(BF16), ffn_w_down[layer, 1].astype(BF16), g[5])
    return xf.reshape(b, t, d)
```

```python
import functools
import math

import jax
import jax.numpy as jnp
from jax import lax
from jax.experimental import pallas as pl
from jax.experimental.pallas import tpu as pltpu

F32 = jnp.float32
BF16 = jnp.bfloat16

D_MODEL = 1024
D_FF = 2816
NORM_EPS = 1e-6
HALF_STEP = 0.5

ML_HEADS = 4
ML_DQK = 128
ML_DV = 256
ML_CHUNK = 64
ML_GATE_CAP = 15.0

RW_HEAD = 64
RW_PAIRS = D_MODEL // (2 * RW_HEAD)
RW_CHUNK = 64
RW_GROUP = 4
RW_LN_EPS = 64e-5

RT_HEADS = 4
RT_DK = 256
RT_DV = 512
RT_CHUNK = 128
RT_ROPE_BASE = 10000.0

LANES = 128
VMEM_LIMIT = 56 * 1024 * 1024


def _cparams(sem):
    return pltpu.CompilerParams(dimension_semantics=sem, vmem_limit_bytes=VMEM_LIMIT)


def _lockstep(gens):
    gens = list(gens)
    results = [None] * len(gens)
    live = list(range(len(gens)))
    while live:
        for i in list(live):
            try:
                next(gens[i])
            except StopIteration as done:
                results[i] = done.value
                live.remove(i)
    return results


def _rms(x, g):
    return x * lax.rsqrt(jnp.mean(x * x, axis=-1, keepdims=True) + NORM_EPS) * g


def _dot(a, b):
    return jnp.dot(a, b, preferred_element_type=F32)


def _dot_nt(a, b):
    return lax.dot_general(a, b, (((1,), (1,)), ((), ())), preferred_element_type=F32)


def _dot_tn(a, b):
    return lax.dot_general(a, b, (((0,), (0,)), ((), ())), preferred_element_type=F32)


def _split3(x):
    hi = x.astype(BF16)
    r1 = x - hi.astype(F32)
    mid = r1.astype(BF16)
    lo = (r1 - mid.astype(F32)).astype(BF16)
    return hi, mid, lo


def _sigmoid(x):
    return 1.0 / (1.0 + jnp.exp(-x))


def _softplus(x):
    return jnp.maximum(x, 0.0) + jnp.log(1.0 + jnp.exp(-jnp.abs(x)))


def _ffn_kernel(x_ref, gpre_ref, wg_ref, wu_ref, wd_ref, gpost_ref, o_ref, xn_ref, acc_ref):
    j = pl.program_id(1)

    @pl.when(j == 0)
    def _():
        xn_ref[...] = _rms(x_ref[...], gpre_ref[...]).astype(BF16)

    xn = xn_ref[...]
    gate = _dot(xn, wg_ref[...])
    up = _dot(xn, wu_ref[...])
    act = (gate * _sigmoid(gate) * up).astype(BF16)
    contrib = _dot(act, wd_ref[...])

    @pl.when(j == 0)
    def _():
        acc_ref[...] = contrib

    @pl.when(j > 0)
    def _():
        acc_ref[...] += contrib

    @pl.when(j == pl.num_programs(1) - 1)
    def _():
        o_ref[...] = x_ref[...] + HALF_STEP * _rms(acc_ref[...], gpost_ref[...])


def ffn_sublayer(x, g_pre, w_gu, w_down, g_post, *, tm=1024, tf=256):
    n, d = x.shape
    tm = min(tm, n)
    f = w_down.shape[0]
    nf = f // tf
    return pl.pallas_call(
        _ffn_kernel,
        grid=(n // tm, nf),
        in_specs=[
            pl.BlockSpec((tm, d), lambda i, j: (i, 0)),
            pl.BlockSpec((1, d), lambda i, j: (0, 0)),
            pl.BlockSpec((d, tf), lambda i, j: (0, j)),
            pl.BlockSpec((d, tf), lambda i, j: (0, j + nf)),
            pl.BlockSpec((tf, d), lambda i, j: (j, 0)),
            pl.BlockSpec((1, d), lambda i, j: (0, 0)),
        ],
        out_specs=pl.BlockSpec((tm, d), lambda i, j: (i, 0)),
        out_shape=jax.ShapeDtypeStruct((n, d), F32),
        scratch_shapes=[pltpu.VMEM((tm, d), BF16), pltpu.VMEM((tm, d), F32)],
        compiler_params=_cparams(("parallel", "arbitrary")),
        name="ffn",
    )(x, g_pre, w_gu, w_gu, w_down, g_post)


def _norm_matmul_kernel(x_ref, g_ref, w_ref, o_ref, xn_ref):
    @pl.when(pl.program_id(1) == 0)
    def _():
        xn_ref[...] = _rms(x_ref[...], g_ref[...]).astype(BF16)

    o_ref[...] = _dot(xn_ref[...], w_ref[...]).astype(o_ref.dtype)


def norm_matmul(x, g, w, out_dtype, *, tm=1024, tn=512):
    n, d = x.shape
    nout = w.shape[1]
    tm = min(tm, n)
    tn = min(tn, nout)
    return pl.pallas_call(
        _norm_matmul_kernel,
        grid=(n // tm, nout // tn),
        in_specs=[
            pl.BlockSpec((tm, d), lambda i, j: (i, 0)),
            pl.BlockSpec((1, d), lambda i, j: (0, 0)),
            pl.BlockSpec((d, tn), lambda i, j: (0, j)),
        ],
        out_specs=pl.BlockSpec((tm, tn), lambda i, j: (i, j)),
        out_shape=jax.ShapeDtypeStruct((n, nout), out_dtype),
        scratch_shapes=[pltpu.VMEM((tm, d), BF16)],
        compiler_params=_cparams(("parallel", "arbitrary")),
        name="norm_matmul",
    )(x, g, w)


def _out_proj_kernel(h_ref, w_ref, x_ref, g_ref, o_ref):
    y = _dot(h_ref[...], w_ref[...])
    o_ref[...] = x_ref[...] + _rms(y, g_ref[...])


def out_proj_residual(h, w, x, g, *, tm=1024):
    n, k = h.shape
    tm = min(tm, n)
    d = w.shape[1]
    return pl.pallas_call(
        _out_proj_kernel,
        grid=(n // tm,),
        in_specs=[
            pl.BlockSpec((tm, k), lambda i: (i, 0)),
            pl.BlockSpec((k, d), lambda i: (0, 0)),
            pl.BlockSpec((tm, d), lambda i: (i, 0)),
            pl.BlockSpec((1, d), lambda i: (0, 0)),
        ],
        out_specs=pl.BlockSpec((tm, d), lambda i: (i, 0)),
        out_shape=jax.ShapeDtypeStruct((n, d), F32),
        compiler_params=_cparams(("parallel",)),
        name="out_proj",
    )(h, w, x, g)


def _out_proj_pairs_kernel(h_ref, w_ref, x_ref, g_ref, o_ref):
    h = jnp.concatenate([h_ref[p] for p in range(h_ref.shape[0])], axis=-1)
    y = _dot(h, w_ref[...])
    o_ref[...] = x_ref[...] + _rms(y, g_ref[...])


def out_proj_pairs_residual(h, w, x, g, *, tm=1024):
    p, n, l = h.shape
    tm = min(tm, n)
    d = w.shape[1]
    return pl.pallas_call(
        _out_proj_pairs_kernel,
        grid=(n // tm,),
        in_specs=[
            pl.BlockSpec((p, tm, l), lambda i: (0, i, 0)),
            pl.BlockSpec((p * l, d), lambda i: (0, 0)),
            pl.BlockSpec((tm, d), lambda i: (i, 0)),
            pl.BlockSpec((1, d), lambda i: (0, 0)),
        ],
        out_specs=pl.BlockSpec((tm, d), lambda i: (i, 0)),
        out_shape=jax.ShapeDtypeStruct((n, d), F32),
        compiler_params=_cparams(("parallel",)),
        name="out_proj_pairs",
    )(h, w, x, g)


def _mlstm_kernel(q_ref, k_ref, v_ref, o_ref, ig_ref, fg_ref, ng_ref, out_ref, c_ref, n_ref):
    L, H, dk, dv = ML_CHUNK, ML_HEADS, ML_DQK, ML_DV
    nc = ig_ref.shape[2]
    c_ref[...] = jnp.zeros_like(c_ref)
    n_ref[...] = jnp.zeros_like(n_ref)
    row = lax.broadcasted_iota(jnp.int32, (L, L), 0)
    col = lax.broadcasted_iota(jnp.int32, (L, L), 1)
    tri = row >= col
    eye = row == col
    ones = jnp.ones((L, L), BF16)

    def step(hd, c, sl, m):
        qc = q_ref[0, sl, hd * dk:(hd + 1) * dk]
        kc = k_ref[0, sl, hd * dk:(hd + 1) * dk]
        vc = v_ref[0, sl, hd * dv:(hd + 1) * dv]
        irow = ig_ref[0, hd, pl.ds(c, 1), :]
        frow = fg_ref[0, hd, pl.ds(c, 1), :]
        fh, fm, fl = _split3(jnp.where(tri, frow, 0.0))
        bmat = _dot(fh, ones) + _dot(fm, ones) + _dot(fl, ones)
        brow = _dot_nt(ones, fh) + _dot_nt(ones, fm) + _dot_nt(ones, fl)
        qk = _dot_nt(qc, kc)
        cmat = c_ref[hd]
        nrow = n_ref[hd]
        yield
        log_d = jnp.where(tri, bmat - brow + irow, -jnp.inf)
        bcol = bmat[:, 0:1]
        log_inter = bcol + m
        m_t = jnp.maximum(log_inter, jnp.max(log_d, axis=-1, keepdims=True))
        s = qk * jnp.exp(log_d - m_t)
        inter = jnp.exp(log_inter - m_t)
        b_end = bmat[L - 1:L, 0:1]
        log_w = b_end - brow[0:1, :] + irow
        m_new = jnp.maximum(b_end + m, jnp.max(log_w, axis=-1, keepdims=True))
        w_row = jnp.exp(log_w - m_new)
        carry_decay = jnp.exp(b_end + m - m_new)
        wdiag = jnp.where(eye, w_row, 0.0).astype(BF16)
        wv = _dot(wdiag, vc).astype(BF16)
        n_add = _dot(jnp.broadcast_to(w_row, (8, L)).astype(BF16), kc)[0:1, :]
        num = _dot(s.astype(BF16), vc) + inter * _dot(qc, cmat.astype(BF16))
        yield
        c_new = carry_decay * cmat + _dot_tn(kc, wv)
        n_new = carry_decay * nrow + n_add
        qn = jnp.sum(qc.astype(F32) * nrow, axis=-1, keepdims=True)
        den = jnp.sum(s, axis=-1, keepdims=True) + inter * qn
        h = num / jnp.maximum(jnp.abs(den), jnp.exp(-m_t))
        ng = ng_ref[:, hd * dv:(hd + 1) * dv]
        hn = h * lax.rsqrt(jnp.mean(h * h, axis=-1, keepdims=True) + NORM_EPS) * ng
        og = o_ref[0, sl, hd * dv:(hd + 1) * dv].astype(F32)
        return m_new, c_new, n_new, (hn * _sigmoid(og)).astype(out_ref.dtype)

    def body(c, ms):
        sl = pl.ds(pl.multiple_of(c * L, L), L)
        res = _lockstep(step(hd, c, sl, ms[hd]) for hd in range(H))
        for hd in range(H):
            _, c_ref[hd], n_ref[hd], out_ref[0, sl, hd * dv:(hd + 1) * dv] = res[hd]
        return tuple(r[0] for r in res)

    lax.fori_loop(0, nc, body, tuple(jnp.zeros((1, 1), F32) for _ in range(H)))


def mlstm_scan(q, k, v, o, ig, fg, ng):
    b, t, _ = q.shape
    h, dk, dv, L = ML_HEADS, ML_DQK, ML_DV, ML_CHUNK
    nc = t // L
    seq = lambda w: pl.BlockSpec((1, t, w), lambda i: (i, 0, 0))
    gates = pl.BlockSpec((1, h, nc, L), lambda i: (i, 0, 0, 0))
    return pl.pallas_call(
        _mlstm_kernel,
        grid=(b,),
        in_specs=[seq(h * dk), seq(h * dk), seq(h * dv), seq(h * dv), gates, gates,
                  pl.BlockSpec((1, h * dv), lambda i: (0, 0))],
        out_specs=seq(h * dv),
        out_shape=jax.ShapeDtypeStruct((b, t, h * dv), BF16),
        scratch_shapes=[pltpu.VMEM((h, dk, dv), F32), pltpu.VMEM((h, 1, dk), F32)],
        compiler_params=_cparams(("parallel",)),
        name="mlstm_scan",
    )(q, k, v, o, ig, fg, ng)


def mlstm_sublayer(x, g_pre, w_in, b_if, conv_w, norm_g, w_out, g_post, b, t):
    h, dk, dv, L = ML_HEADS, ML_DQK, ML_DV, ML_CHUNK
    n_main = 2 * h * dk + 2 * h * dv
    w_main = w_in[:, :n_main].astype(BF16)
    w_if = jnp.pad(w_in[:, n_main:], ((0, 0), (0, LANES - 2 * h))).astype(BF16)
    proj = norm_matmul(x, g_pre, w_main, BF16)
    if_pre = norm_matmul(x, g_pre, w_if, F32)[:, :2 * h]
    qk = proj[:, :2 * h * dk].astype(F32).reshape(b, t, 2 * h * dk)
    kw = conv_w.shape[0]
    qk_pad = jnp.pad(qk, ((0, 0), (kw - 1, 0), (0, 0)))
    conv = sum(qk_pad[:, i:i + t, :] * conv_w[i] for i in range(kw))
    conv = conv * jax.nn.sigmoid(conv)
    q = (conv[..., :h * dk] * (dk ** -0.5)).astype(BF16)
    k = conv[..., h * dk:].astype(BF16)
    v = proj[:, 2 * h * dk:2 * h * dk + h * dv].reshape(b, t, h * dv)
    o = proj[:, 2 * h * dk + h * dv:].reshape(b, t, h * dv)
    if_pre = ML_GATE_CAP * jnp.tanh((if_pre + b_if) / ML_GATE_CAP)
    i_log = if_pre[:, :h]
    f_log = jax.nn.log_sigmoid(if_pre[:, h:])
    to_rows = lambda a: jnp.transpose(a.reshape(b, t, h), (0, 2, 1)).reshape(b, h, t // L, L)
    hs = mlstm_scan(q, k, v, o, to_rows(i_log), to_rows(f_log), norm_g[None, :])
    return out_proj_residual(hs.reshape(b * t, h * dv), w_out.astype(BF16), x, g_post)


def _retention_kernel(lg_ref, q_ref, k_ref, v_ref, g_ref, cos_ref, sin_ref, out_ref, r_ref):
    L = RT_CHUNK
    half = RT_DK // 2
    nc = q_ref.shape[1] // L
    log_gamma = lg_ref[pl.program_id(1)]
    r_ref[...] = jnp.zeros_like(r_ref)
    row = lax.broadcasted_iota(jnp.int32, (L, L), 0)
    col = lax.broadcasted_iota(jnp.int32, (L, L), 1)
    diff = (row - col).astype(F32)
    d_mask = jnp.where(row >= col, jnp.exp(log_gamma * jnp.maximum(diff, 0.0)), 0.0)
    idx = lax.broadcasted_iota(jnp.int32, (L, 1), 0).astype(F32)
    q_decay = jnp.exp(log_gamma * (idx + 1.0))
    k_decay = jnp.exp(log_gamma * (L - 1.0 - idx))
    chunk_decay = jnp.exp(jnp.full((1, 1), L, F32) * log_gamma)

    def rotate(u, cos, sin):
        u1, u2 = u[:, :half], u[:, half:]
        return jnp.concatenate([u1 * cos - u2 * sin, u1 * sin + u2 * cos], axis=-1)

    def body(c, carry):
        t0 = pl.multiple_of(c * L, L)
        cos = cos_ref[0, pl.ds(t0, L), :]
        sin = sin_ref[0, pl.ds(t0, L), :]
        qr = rotate(q_ref[0, pl.ds(t0, L), :].astype(F32), cos, sin)
        kr = rotate(k_ref[0, pl.ds(t0, L), :].astype(F32), cos, sin) * (RT_DK ** -0.5)
        vc = v_ref[0, pl.ds(t0, L), :]
        qb = qr.astype(BF16)
        s = _dot_nt(qb, kr.astype(BF16)) * d_mask
        rmat = r_ref[...]
        y = _dot(s.astype(BF16), vc) + _dot(qb, rmat.astype(BF16)) * q_decay
        r_ref[...] = chunk_decay * rmat + _dot_tn((kr * k_decay).astype(BF16), vc)
        yn = y * lax.rsqrt(jnp.mean(y * y, axis=-1, keepdims=True) + NORM_EPS)
        gc = g_ref[0, pl.ds(t0, L), :].astype(F32)
        out_ref[0, pl.ds(t0, L), :] = (gc * _sigmoid(gc) * yn).astype(out_ref.dtype)
        return carry

    lax.fori_loop(0, nc, body, 0)


def retention_scan(proj, cos, sin, log_gamma):
    b, t, _ = proj.shape
    h, dk, dv = RT_HEADS, RT_DK, RT_DV
    kq = (h * dk) // dk
    kv = (2 * h * dk) // dv
    return pl.pallas_call(
        _retention_kernel,
        grid=(b, h),
        in_specs=[
            pl.BlockSpec(memory_space=pltpu.SMEM),
            pl.BlockSpec((1, t, dk), lambda i, j: (i, 0, j)),
            pl.BlockSpec((1, t, dk), lambda i, j: (i, 0, kq + j)),
            pl.BlockSpec((1, t, dv), lambda i, j: (i, 0, kv + j)),
            pl.BlockSpec((1, t, dv), lambda i, j: (i, 0, kv + h + j)),
            pl.BlockSpec((1, t, dk // 2), lambda i, j: (i, 0, 0)),
            pl.BlockSpec((1, t, dk // 2), lambda i, j: (i, 0, 0)),
        ],
        out_specs=pl.BlockSpec((1, t, dv), lambda i, j: (i, 0, j)),
        out_shape=jax.ShapeDtypeStruct((b, t, h * dv), BF16),
        scratch_shapes=[pltpu.VMEM((dk, dv), F32)],
        compiler_params=_cparams(("parallel", "parallel")),
        name="retention_scan",
    )(log_gamma, proj, proj, proj, proj, cos, sin)


def retention_sublayer(x, positions, g_pre, w_in, w_out, g_post, b, t):
    h, dk, dv = RT_HEADS, RT_DK, RT_DV
    proj = norm_matmul(x, g_pre, w_in.astype(BF16), BF16).reshape(b, t, -1)
    inv_freq = 1.0 / (RT_ROPE_BASE ** jnp.linspace(0.0, 1.0, dk // 2, dtype=F32))
    ang = positions.astype(F32)[:, :, None] * inv_freq
    log_gamma = jnp.log(1.0 - 2.0 ** (-5.0 - jnp.arange(h, dtype=F32)))
    y = retention_scan(proj, jnp.cos(ang), jnp.sin(ang), log_gamma)
    return out_proj_residual(y.reshape(b * t, h * dv), w_out.astype(BF16), x, g_post)


def _rwkv_proj_kernel(x_ref, xp_ref, g_ref, mu_ref, wrkv_ref, w0_ref, w1_ref, w2_ref,
                      a0_ref, a1_ref, a2_ref, g1_ref, g2_ref,
                      r_ref, k_ref, v_ref, lw_ref, a_ref, gate_ref, *, seq_tiles):
    i = pl.program_id(0)
    tm = x_ref.shape[0]
    g = g_ref[...]
    h = _rms(x_ref[...], g)
    prev = _rms(xp_ref[...], g)[7:8, :]
    prev = prev * (i % seq_tiles != 0).astype(F32)
    rid = lax.broadcasted_iota(jnp.int32, (tm, 1), 0)
    h_prev = jnp.where(rid == 0, prev, pltpu.roll(h, 1, axis=0))
    xx = h_prev - h
    mix = lambda j: (h + xx * mu_ref[j:j + 1, :]).astype(BF16)

    def put(ref, val):
        for p in range(ref.shape[0]):
            ref[p] = val[:, p * LANES:(p + 1) * LANES].astype(ref.dtype)

    put(r_ref, _dot(mix(0), wrkv_ref[0]))
    put(k_ref, _dot(mix(1), wrkv_ref[1]))
    put(v_ref, _dot(mix(2), wrkv_ref[2]))
    wl = jnp.tanh(_dot(mix(3), w1_ref[...])).astype(BF16)
    w_log = -_softplus(-(w0_ref[...] + _dot(wl, w2_ref[...]))) - 0.5
    put(lw_ref, -jnp.exp(w_log))
    al = _dot(mix(4), a1_ref[...]).astype(BF16)
    put(a_ref, _sigmoid(a0_ref[...] + _dot(al, a2_ref[...])))
    gl = _sigmoid(_dot(mix(5), g1_ref[...])).astype(BF16)
    put(gate_ref, _dot(gl, g2_ref[...]))


def rwkv_proj(x, g_pre, mu, w_rkv, w0, w1, w2, a0, a1, a2, g1, g2, t, *, tm=512):
    n, d = x.shape
    tm = min(tm, t)
    p = d // LANES
    full = lambda a: pl.BlockSpec(a.shape, lambda i: (0,) * a.ndim)
    outs = pl.BlockSpec((p, tm, LANES), lambda i: (0, i, 0))
    args = (g_pre, mu, w_rkv, w0, w1, w2, a0, a1, a2, g1, g2)
    shp = lambda dt: jax.ShapeDtypeStruct((p, n, LANES), dt)
    return pl.pallas_call(
        functools.partial(_rwkv_proj_kernel, seq_tiles=t // tm),
        grid=(n // tm,),
        in_specs=[
            pl.BlockSpec((tm, d), lambda i: (i, 0)),
            pl.BlockSpec((8, d), lambda i: (jnp.maximum(i * (tm // 8) - 1, 0), 0)),
        ] + [full(a) for a in args],
        out_specs=[outs] * 6,
        out_shape=[shp(BF16), shp(BF16), shp(BF16), shp(F32), shp(BF16), shp(BF16)],
        compiler_params=_cparams(("parallel",)),
        name="rwkv_proj",
    )(x, x, *args)


def _rwkv_scan_kernel(r_ref, k_ref, v_ref, lw_ref, a_ref, gate_ref, kk_ref, ka_ref, rk_ref,
                      lng_ref, lnb_ref, out_ref, s_ref):
    C = RW_CHUNK
    N = RW_HEAD
    C2 = 2 * C
    nc = r_ref.shape[1] // C
    s_ref[...] = jnp.zeros_like(s_ref)
    lane = lax.broadcasted_iota(jnp.int32, (1, LANES), 1)
    in_a = lane < N
    m_a = in_a.astype(F32)
    m_b = 1.0 - m_a
    rr = lax.broadcasted_iota(jnp.int32, (C2, C2), 0)
    cc = lax.broadcasted_iota(jnp.int32, (C2, C2), 1)
    same = (rr // C) == (cc // C)
    strict = same & ((rr % C) > (cc % C))
    incl = same & ((rr % C) >= (cc % C))
    eye2 = (rr == cc).astype(F32)
    level_masks = []
    for lvl in range(int(math.log2(C))):
        rb, cb = (rr % C) >> lvl, (cc % C) >> lvl
        level_masks.append(same & ((rb & 1) == 1) & (cb == rb - 1))
    tr =lax.broadcasted_iota(jnp.int32, (C, C), 0)
    tc = lax.broadcasted_iota(jnp.int32, (C, C), 1)
    tril = (tr >= tc).astype(BF16)

    def seg_sum(z):
        sa = jnp.sum(z * m_a, axis=-1, keepdims=True)
        sb = jnp.sum(z * m_b, axis=-1, keepdims=True)
        return jnp.where(in_a, sa, sb)

    def stack(z):
        return jnp.concatenate([z * m_a, z * m_b], axis=0)

    def load(gi, sl):
        f32 = lambda ref: ref[gi, sl, :].astype(F32)
        return (f32(r_ref), f32(k_ref), f32(v_ref), lw_ref[gi, sl, :], f32(a_ref), f32(gate_ref), s_ref[gi],
                kk_ref[gi], ka_ref[gi], rk_ref[gi], lng_ref[gi], lnb_ref[gi])

    def step(r, k, v, lw, a_sig, gate, s0, kk_w, ka_w, rk_w, ln_g, ln_b):
        kk = k * kk_w
        kk = kk / jnp.maximum(jnp.sqrt(seg_sum(kk * kk)), 1e-12)
        k = k * (1.0 + (a_sig - 1.0) * ka_w)
        a_ = -kk
        b_ = kk * a_sig
        lh, lm, ll = _split3(lw)
        gcum = _dot(tril, lh) + _dot(tril, lm) + _dot(tril, ll)
        yield
        g_end = gcum[C - 1:C, :]
        e_pos = jnp.exp(gcum)
        e_neg = jnp.exp(-gcum)
        e_rest = jnp.exp(g_end - gcum)
        a_h = stack(a_ * jnp.exp(gcum - lw))
        r_h = stack(r * e_pos)
        b_h = stack(b_ * e_neg)
        k_h = stack(k * e_neg)
        b_t = stack(b_ * e_rest)
        k_t = stack(k * e_rest)
        v_s = stack(v)
        lhs = jnp.concatenate([a_h, r_h], axis=0).astype(BF16)
        rhs = jnp.concatenate([b_h, k_h], axis=0).astype(BF16)
        sc = _dot_nt(lhs, rhs)
        yield
        a_ab = jnp.where(strict, sc[:C2, :C2], 0.0)
        a_ak = jnp.where(strict, sc[:C2, C2:], 0.0)
        a_rb = jnp.where(incl, sc[C2:, :C2], 0.0)
        a_rk = jnp.where(incl, sc[C2:, C2:], 0.0)
        v_sb = v_s.astype(BF16)
        w1 = _dot(a_ak.astype(BF16), v_sb)
        tinv = eye2 + jnp.where(level_masks[0], a_ab, 0.0)
        for mk in level_masks[1:]:
            tb = tinv.astype(BF16)
            l21 = jnp.where(mk, a_ab, 0.0).astype(BF16)
            tl = _dot(tb, l21)
            yield
            tinv = tinv + _dot(tl.astype(BF16), tb)
            yield
        au = _dot(tinv.astype(BF16), jnp.concatenate([a_h, w1], axis=1).astype(BF16))
        yield
        a_t, u0 = au[:, :LANES], au[:, LANES:]
        a_tb = a_t.astype(BF16)
        u0b = u0.astype(BF16)
        p_t = _dot_tn(a_tb, b_t.astype(BF16))
        z_t = _dot_tn(jnp.concatenate([u0b, v_sb], axis=0),
                      jnp.concatenate([b_t, k_t], axis=0).astype(BF16))
        a_rbb = a_rb.astype(BF16)
        r_bar = r_h + _dot(a_rbb, a_tb)
        y_bar = _dot(a_rbb, u0b) + _dot(a_rk.astype(BF16), v_sb)
        yield
        s0b = s0.astype(BF16)
        ys = _dot_nt(r_bar.astype(BF16), s0b) + y_bar
        s_new = s0 * jnp.exp(g_end) + _dot(s0b, p_t.astype(BF16)) + z_t
        yield
        y = ys[:C, :] + ys[C:, :]
        mean = seg_sum(y) * (1.0 / N)
        yc = y - mean
        var = seg_sum(yc * yc) * (1.0 / N)
        yn = yc * lax.rsqrt(var + RW_LN_EPS) * ln_g + ln_b
        bonus = seg_sum(r * k * rk_w) * v
        return s_new, ((yn + bonus) * gate).astype(out_ref.dtype)

    def body(c, carry):
        sl = pl.ds(pl.multiple_of(c * C, C), C)
        groups = range(r_ref.shape[0])
        results = _lockstep(step(*load(gi, sl)) for gi in groups)
        for gi in groups:
            s_ref[gi], out_ref[gi, sl, :] = results[gi]
        return carry

    lax.fori_loop(0, nc, body, 0)


def rwkv_scan(r, k, v, lw, a, gate, k_k, k_a, r_k, ln_g, ln_b, b, t):
    p, n, l = r.shape
    gp = RW_GROUP
    seq = lambda: pl.BlockSpec((gp, t, l), lambda i, j: (j, i, 0))
    par = lambda: pl.BlockSpec((gp, 1, l), lambda i, j: (j, 0, 0))
    return pl.pallas_call(
        _rwkv_scan_kernel,
        grid=(b, p // gp),
        in_specs=[seq()] * 6 + [par()] * 5,
        out_specs=seq(),
        out_shape=jax.ShapeDtypeStruct((p, n, l), BF16),
        scratch_shapes=[pltpu.VMEM((gp, l, l), F32)],
        compiler_params=_cparams(("parallel", "parallel")),
        name="rwkv_scan",
    )(r, k, v, lw, a, gate, k_k, k_a, r_k, ln_g, ln_b)


def rwkv_sublayer(x, g_pre, mu, w_rkv, w0, w1, w2, a0, a1, a2, g1, g2,
                  k_k, k_a, r_k, ln_g, ln_b, w_out, g_post, b, t):
    bf = lambda a: a.astype(BF16)
    row = lambda a: a.reshape(1, -1)
    r, k, v, lw, a, gate = rwkv_proj(x, g_pre, mu, bf(w_rkv), row(w0), bf(w1), bf(w2),
                                     row(a0), bf(a1), bf(a2), bf(g1), bf(g2), t)
    par = lambda a: a.reshape(RW_PAIRS, 1, LANES)
    y = rwkv_scan(r, k, v, lw, a, gate, par(k_k), par(k_a), par(r_k), par(ln_g), par(ln_b), b, t)
    return out_proj_pairs_residual(y, bf(w_out), x, g_post)


def kernel(x, positions, norm_g, ffn_w_gu, ffn_w_down, ml_w_in, ml_b_if, ml_conv_w, ml_norm_g, ml_w_out,
           rw_mu, rw_w_rkv, rw_w0, rw_w1, rw_w2, rw_a0, rw_a1, rw_a2, rw_g1, rw_g2,
           rw_k_k, rw_k_a, rw_r_k, rw_ln_g, rw_ln_b, rw_w_out, rt_w_in, rt_w_out):
    b, t, d = x.shape
    depth = norm_g.shape[0]
    xf = x.reshape(b * t, d)
    for layer in range(depth):
        g = norm_g[layer][:, None, :]
        xf = ffn_sublayer(xf, g[0], ffn_w_gu[layer, 0].astype(BF16), ffn_w_down[layer, 0].astype(BF16), g[1])
        kind, j = layer % 3, layer // 3
        if kind == 0:
            xf = mlstm_sublayer(xf, g[2], ml_w_in[j], ml_b_if[j], ml_conv_w[j], ml_norm_g[j], ml_w_out[j],
                                g[3], b, t)
        elif kind == 1:
            xf = rwkv_sublayer(xf, g[2], rw_mu[j], rw_w_rkv[j], rw_w0[j], rw_w1[j], rw_w2[j],
                               rw_a0[j], rw_a1[j], rw_a2[j], rw_g1[j], rw_g2[j],
                               rw_k_k[j], rw_k_a[j], rw_r_k[j].reshape(-1), rw_ln_g[j], rw_ln_b[j],
                               rw_w_out[j], g[3], b, t)
        else:
            xf = retention_sublayer(xf, positions, g[2], rt_w_in[j], rt_w_out[j], g[3], b, t)
        xf = ffn_sublayer(xf, g[4], ffn_w_gu[layer, 1].astype(BF16), ffn_w_down[layer, 1].astype(BF16), g[5])
    return xf.reshape(b, t, d)
```

```python
import functools
import math

import jax
import jax.numpy as jnp
from jax import lax
from jax.experimental import pallas as pl
from jax.experimental.pallas import tpu as pltpu

F32 = jnp.float32
BF16 = jnp.bfloat16

D_MODEL = 1024
D_FF = 2816
NORM_EPS = 1e-6
HALF_STEP = 0.5

ML_HEADS = 4
ML_DQK = 128
ML_DV = 256
ML_CHUNK = 64
ML_GATE_CAP = 15.0

RW_HEAD = 64
RW_PAIRS = D_MODEL // (2 * RW_HEAD)
RW_CHUNK = 64
RW_TIME_BLOCK = 512
RW_LN_EPS = 64e-5

RT_HEADS = 4
RT_DK = 256
RT_DV = 512
RT_CHUNK = 128
RT_ROPE_BASE = 10000.0

LANES = 128
VMEM_LIMIT = 56 * 1024 * 1024


def _cparams(sem):
    return pltpu.CompilerParams(dimension_semantics=sem, vmem_limit_bytes=VMEM_LIMIT)


def _lockstep(gens):
    gens = list(gens)
    results = [None] * len(gens)
    live = list(range(len(gens)))
    while live:
        for i in list(live):
            try:
                next(gens[i])
            except StopIteration as done:
                results[i] = done.value
                live.remove(i)
    return results


def _rms(x, g):
    return x * lax.rsqrt(jnp.mean(x * x, axis=-1, keepdims=True) + NORM_EPS) * g


def _dot(a, b):
    return jnp.dot(a, b, preferred_element_type=F32)


def _dot_nt(a, b):
    return lax.dot_general(a, b, (((1,), (1,)), ((), ())), preferred_element_type=F32)


def _dot_tn(a, b):
    return lax.dot_general(a, b, (((0,), (0,)), ((), ())), preferred_element_type=F32)


def _split3(x):
    hi = x.astype(BF16)
    r1 = x - hi.astype(F32)
    mid = r1.astype(BF16)
    lo = (r1 - mid.astype(F32)).astype(BF16)
    return hi, mid, lo


def _sigmoid(x):
    return 1.0 / (1.0 + jnp.exp(-x))


def _softplus(x):
    return jnp.maximum(x, 0.0) + jnp.log(1.0 + jnp.exp(-jnp.abs(x)))


def _ffn_kernel(x_ref, gpre_ref, wgu_ref, wd_ref, gpost_ref, o_ref):
    f = wd_ref.shape[0]
    x = x_ref[...]
    xn = _rms(x, gpre_ref[...]).astype(BF16)
    gate = _dot(xn, wgu_ref[:, :f])
    up = _dot(xn, wgu_ref[:, f:])
    act = (gate * _sigmoid(gate) * up).astype(BF16)
    h = _dot(act, wd_ref[...])
    o_ref[...] = x + HALF_STEP * _rms(h, gpost_ref[...])


def _resident(shape):
    return pl.BlockSpec(shape, lambda *_: (0,) * len(shape), pipeline_mode=pl.Buffered(1))


def ffn_sublayer(x, g_pre, w_gu, w_down, g_post, *, tm=512):
    n, d = x.shape
    tm = min(tm, n)
    return pl.pallas_call(
        _ffn_kernel,
        grid=(n // tm,),
        in_specs=[
            pl.BlockSpec((tm, d), lambda i: (i, 0)),
            _resident(g_pre.shape),
            _resident(w_gu.shape),
            _resident(w_down.shape),
            _resident(g_post.shape),
        ],
        out_specs=pl.BlockSpec((tm, d), lambda i: (i, 0)),
        out_shape=jax.ShapeDtypeStruct((n, d), F32),
        compiler_params=_cparams(("parallel",)),
        name="ffn",
    )(x, g_pre, w_gu, w_down, g_post)


def _norm_matmul_kernel(x_ref, g_ref, w_ref, o_ref):
    xn = _rms(x_ref[...], g_ref[...]).astype(BF16)
    o_ref[...] = _dot(xn, w_ref[...]).astype(o_ref.dtype)


def norm_matmul(x, g, w, out_dtype, *, tm=512):
    n, d = x.shape
    nout = w.shape[1]
    tm = min(tm, n)
    return pl.pallas_call(
        _norm_matmul_kernel,
        grid=(n // tm,),
        in_specs=[
            pl.BlockSpec((tm, d), lambda i: (i, 0)),
            _resident(g.shape),
            _resident(w.shape),
        ],
        out_specs=pl.BlockSpec((tm, nout), lambda i: (i, 0)),
        out_shape=jax.ShapeDtypeStruct((n, nout), out_dtype),
        compiler_params=_cparams(("parallel",)),
        name="norm_matmul",
    )(x, g, w)


def _out_proj_kernel(h_ref, w_ref, x_ref, g_ref, o_ref):
    y = _dot(h_ref[...], w_ref[...])
    o_ref[...] = x_ref[...] + _rms(y, g_ref[...])


def out_proj_residual(h, w, x, g, *, tm=1024):
    n, k = h.shape
    tm = min(tm, n)
    d = w.shape[1]
    return pl.pallas_call(
        _out_proj_kernel,
        grid=(n // tm,),
        in_specs=[
            pl.BlockSpec((tm, k), lambda i: (i, 0)),
            pl.BlockSpec((k, d), lambda i: (0, 0)),
            pl.BlockSpec((tm, d), lambda i: (i, 0)),
            pl.BlockSpec((1, d), lambda i: (0, 0)),
        ],
        out_specs=pl.BlockSpec((tm, d), lambda i: (i, 0)),
        out_shape=jax.ShapeDtypeStruct((n, d), F32),
        compiler_params=_cparams(("parallel",)),
        name="out_proj",
    )(h, w, x, g)


def _out_proj_pairs_kernel(h_ref, w_ref, x_ref, g_ref, o_ref):
    h = jnp.concatenate([h_ref[p] for p in range(h_ref.shape[0])], axis=-1)
    y = _dot(h, w_ref[...])
    o_ref[...] = x_ref[...] + _rms(y, g_ref[...])


def out_proj_pairs_residual(h, w, x, g, *, tm=1024):
    p, n, l = h.shape
    tm = min(tm, n)
    d = w.shape[1]
    return pl.pallas_call(
        _out_proj_pairs_kernel,
        grid=(n // tm,),
        in_specs=[
            pl.BlockSpec((p, tm, l), lambda i: (0, i, 0)),
            pl.BlockSpec((p * l, d), lambda i: (0, 0)),
            pl.BlockSpec((tm, d), lambda i: (i, 0)),
            pl.BlockSpec((1, d), lambda i: (0, 0)),
        ],
        out_specs=pl.BlockSpec((tm, d), lambda i: (i, 0)),
        out_shape=jax.ShapeDtypeStruct((n, d), F32),
        compiler_params=_cparams(("parallel",)),
        name="out_proj_pairs",
    )(h, w, x, g)


def _mlstm_kernel(q_ref, k_ref, v_ref, o_ref, ig_ref, fg_ref, ng_ref, out_ref, c_ref, n_ref):
    L, H, dk, dv = ML_CHUNK, ML_HEADS, ML_DQK, ML_DV
    nc = ig_ref.shape[2]
    c_ref[...] = jnp.zeros_like(c_ref)
    n_ref[...] = jnp.zeros_like(n_ref)
    row = lax.broadcasted_iota(jnp.int32, (L, L), 0)
    col = lax.broadcasted_iota(jnp.int32, (L, L), 1)
    tri = row >= col
    eye = row == col
    ones = jnp.ones((L, L), BF16)

    def step(hd, c, sl, m):
        qc = q_ref[0, sl, hd * dk:(hd + 1) * dk]
        kc = k_ref[0, sl, hd * dk:(hd + 1) * dk]
        vc = v_ref[0, sl, hd * dv:(hd + 1) * dv]
        irow = ig_ref[0, hd, pl.ds(c, 1), :]
        frow = fg_ref[0, hd, pl.ds(c, 1), :]
        fh, fm, fl = _split3(jnp.where(tri, frow, 0.0))
        bmat = _dot(fh, ones) + _dot(fm, ones) + _dot(fl, ones)
        brow = _dot_nt(ones, fh) + _dot_nt(ones, fm) + _dot_nt(ones, fl)
        qk = _dot_nt(qc, kc)
        cmat = c_ref[hd]
        nrow = n_ref[hd]
        yield
        log_d = jnp.where(tri, bmat - brow + irow, -jnp.inf)
        bcol = bmat[:, 0:1]
        log_inter = bcol + m
        m_t = jnp.maximum(log_inter, jnp.max(log_d, axis=-1, keepdims=True))
        s = qk * jnp.exp(log_d - m_t)
        inter = jnp.exp(log_inter - m_t)
        b_end = bmat[L - 1:L, 0:1]
        log_w = b_end - brow[0:1, :] + irow
        m_new = jnp.maximum(b_end + m, jnp.max(log_w, axis=-1, keepdims=True))
        w_row = jnp.exp(log_w - m_new)
        carry_decay = jnp.exp(b_end + m - m_new)
        wdiag = jnp.where(eye, w_row, 0.0).astype(BF16)
        wv = _dot(wdiag, vc).astype(BF16)
        n_add = _dot(jnp.broadcast_to(w_row, (8, L)).astype(BF16), kc)[0:1, :]
        num = _dot(s.astype(BF16), vc) + inter * _dot(qc, cmat.astype(BF16))
        yield
        c_new = carry_decay * cmat + _dot_tn(kc, wv)
        n_new = carry_decay * nrow + n_add
        qn = jnp.sum(qc.astype(F32) * nrow, axis=-1, keepdims=True)
        den = jnp.sum(s, axis=-1, keepdims=True) + inter * qn
        h = num / jnp.maximum(jnp.abs(den), jnp.exp(-m_t))
        ng = ng_ref[:, hd * dv:(hd + 1) * dv]
        hn = h * lax.rsqrt(jnp.mean(h * h, axis=-1, keepdims=True) + NORM_EPS) * ng
        og = o_ref[0, sl, hd * dv:(hd + 1) * dv].astype(F32)
        return m_new, c_new, n_new, (hn * _sigmoid(og)).astype(out_ref.dtype)

    def body(c, ms):
        sl = pl.ds(pl.multiple_of(c * L, L), L)
        res = _lockstep(step(hd, c, sl, ms[hd]) for hd in range(H))
        for hd in range(H):
            _, c_ref[hd], n_ref[hd], out_ref[0, sl, hd * dv:(hd + 1) * dv] = res[hd]
        return tuple(r[0] for r in res)

    lax.fori_loop(0, nc, body, tuple(jnp.zeros((1, 1), F32) for _ in range(H)))


def mlstm_scan(q, k, v, o, ig, fg, ng):
    b, t, _ = q.shape
    h, dk, dv, L = ML_HEADS, ML_DQK, ML_DV, ML_CHUNK
    nc = t // L
    seq = lambda w: pl.BlockSpec((1, t, w), lambda i: (i, 0, 0))
    gates = pl.BlockSpec((1, h, nc, L), lambda i: (i, 0, 0, 0))
    return pl.pallas_call(
        _mlstm_kernel,
        grid=(b,),
        in_specs=[seq(h * dk), seq(h * dk), seq(h * dv), seq(h * dv), gates, gates,
                  pl.BlockSpec((1, h * dv), lambda i: (0, 0))],
        out_specs=seq(h * dv),
        out_shape=jax.ShapeDtypeStruct((b, t, h * dv), BF16),
        scratch_shapes=[pltpu.VMEM((h, dk, dv), F32), pltpu.VMEM((h, 1, dk), F32)],
        compiler_params=_cparams(("parallel",)),
        name="mlstm_scan",
    )(q, k, v, o, ig, fg, ng)


def mlstm_sublayer(x, g_pre, w_in, b_if, conv_w, norm_g, w_out, g_post, b, t):
    h, dk, dv, L = ML_HEADS, ML_DQK, ML_DV, ML_CHUNK
    n_main = 2 * h * dk + 2 * h * dv
    w_main = w_in[:, :n_main].astype(BF16)
    w_if = jnp.pad(w_in[:, n_main:], ((0, 0), (0, LANES - 2 * h))).astype(BF16)
    proj = norm_matmul(x, g_pre, w_main, BF16)
    if_pre = norm_matmul(x, g_pre, w_if, F32)[:, :2 * h]
    qk = proj[:, :2 * h * dk].astype(F32).reshape(b, t, 2 * h * dk)
    kw = conv_w.shape[0]
    qk_pad = jnp.pad(qk, ((0, 0), (kw - 1, 0), (0, 0)))
    conv = sum(qk_pad[:, i:i + t, :] * conv_w[i] for i in range(kw))
    conv = conv * jax.nn.sigmoid(conv)
    q = (conv[..., :h * dk] * (dk ** -0.5)).astype(BF16)
    k = conv[..., h * dk:].astype(BF16)
    v = proj[:, 2 * h * dk:2 * h * dk + h * dv].reshape(b, t, h * dv)
    o = proj[:, 2 * h * dk + h * dv:].reshape(b, t, h * dv)
    if_pre = ML_GATE_CAP * jnp.tanh((if_pre + b_if) / ML_GATE_CAP)
    i_log = if_pre[:, :h]
    f_log = jax.nn.log_sigmoid(if_pre[:, h:])
    to_rows = lambda a: jnp.transpose(a.reshape(b, t, h), (0, 2, 1)).reshape(b, h, t // L, L)
    hs = mlstm_scan(q, k, v, o, to_rows(i_log), to_rows(f_log), norm_g[None, :])
    return out_proj_residual(hs.reshape(b * t, h * dv), w_out.astype(BF16), x, g_post)


def _retention_kernel(lg_ref, q_ref, k_ref, v_ref, g_ref, cos_ref, sin_ref, out_ref, r_ref):
    L = RT_CHUNK
    half = RT_DK // 2
    nc = q_ref.shape[1] // L
    log_gamma = lg_ref[pl.program_id(1)]
    r_ref[...] = jnp.zeros_like(r_ref)
    row = lax.broadcasted_iota(jnp.int32, (L, L), 0)
    col = lax.broadcasted_iota(jnp.int32, (L, L), 1)
    diff = (row - col).astype(F32)
    d_mask = jnp.where(row >= col, jnp.exp(log_gamma * jnp.maximum(diff, 0.0)), 0.0)
    idx = lax.broadcasted_iota(jnp.int32, (L, 1), 0).astype(F32)
    q_decay = jnp.exp(log_gamma * (idx + 1.0))
    k_decay = jnp.exp(log_gamma * (L - 1.0 - idx))
    chunk_decay = jnp.exp(jnp.full((1, 1), L, F32) * log_gamma)

    def rotate(u, cos, sin):
        u1, u2 = u[:, :half], u[:, half:]
        return jnp.concatenate([u1 * cos - u2 * sin, u1 * sin + u2 * cos], axis=-1)

    def body(c, carry):
        t0 = pl.multiple_of(c * L, L)
        cos = cos_ref[0, pl.ds(t0, L), :]
        sin = sin_ref[0, pl.ds(t0, L), :]
        qr = rotate(q_ref[0, pl.ds(t0, L), :].astype(F32), cos, sin)
        kr = rotate(k_ref[0, pl.ds(t0, L), :].astype(F32), cos, sin) * (RT_DK ** -0.5)
        vc = v_ref[0, pl.ds(t0, L), :]
        qb = qr.astype(BF16)
        s = _dot_nt(qb, kr.astype(BF16)) * d_mask
        rmat = r_ref[...]
        y = _dot(s.astype(BF16), vc) + _dot(qb, rmat.astype(BF16)) * q_decay
        r_ref[...] = chunk_decay * rmat + _dot_tn((kr * k_decay).astype(BF16), vc)
        yn = y * lax.rsqrt(jnp.mean(y * y, axis=-1, keepdims=True) + NORM_EPS)
        gc = g_ref[0, pl.ds(t0, L), :].astype(F32)
        out_ref[0, pl.ds(t0, L), :] = (gc * _sigmoid(gc) * yn).astype(out_ref.dtype)
        return carry

    lax.fori_loop(0, nc, body, 0)


def retention_scan(proj, cos, sin, log_gamma):
    b, t, _ = proj.shape
    h, dk, dv = RT_HEADS, RT_DK, RT_DV
    kq = (h * dk) // dk
    kv = (2 * h * dk) // dv
    return pl.pallas_call(
        _retention_kernel,
        grid=(b, h),
        in_specs=[
            pl.BlockSpec(memory_space=pltpu.SMEM),
            pl.BlockSpec((1, t, dk), lambda i, j: (i, 0, j)),
            pl.BlockSpec((1, t, dk), lambda i, j: (i, 0, kq + j)),
            pl.BlockSpec((1, t, dv), lambda i, j: (i, 0, kv + j)),
            pl.BlockSpec((1, t, dv), lambda i, j: (i, 0, kv + h + j)),
            pl.BlockSpec((1, t, dk // 2), lambda i, j: (i, 0, 0)),
            pl.BlockSpec((1, t, dk // 2), lambda i, j: (i, 0, 0)),
        ],
        out_specs=pl.BlockSpec((1, t, dv), lambda i, j: (i, 0, j)),
        out_shape=jax.ShapeDtypeStruct((b, t, h * dv), BF16),
        scratch_shapes=[pltpu.VMEM((dk, dv), F32)],
        compiler_params=_cparams(("parallel", "parallel")),
        name="retention_scan",
    )(log_gamma, proj, proj, proj, proj, cos, sin)


def retention_sublayer(x, positions, g_pre, w_in, w_out, g_post, b, t):
    h, dk, dv = RT_HEADS, RT_DK, RT_DV
    proj = norm_matmul(x, g_pre, w_in.astype(BF16), BF16).reshape(b, t, -1)
    inv_freq = 1.0 / (RT_ROPE_BASE ** jnp.linspace(0.0, 1.0, dk // 2, dtype=F32))
    ang = positions.astype(F32)[:, :, None] * inv_freq
    log_gamma = jnp.log(1.0 - 2.0 ** (-5.0 - jnp.arange(h, dtype=F32)))
    y = retention_scan(proj, jnp.cos(ang), jnp.sin(ang), log_gamma)
    return out_proj_residual(y.reshape(b * t, h * dv), w_out.astype(BF16), x, g_post)


def _rwkv_proj_kernel(x_ref, xp_ref, g_ref, mu_ref, wrkv_ref, w0_ref, w1_ref, w2_ref,
                      a0_ref, a1_ref, a2_ref, g1_ref, g2_ref,
                      r_ref, k_ref, v_ref, gc_ref, a_ref, gate_ref, *, seq_tiles):
    i = pl.program_id(0)
    tm = x_ref.shape[0]
    g = g_ref[...]
    h = _rms(x_ref[...], g)
    prev = _rms(xp_ref[...], g)[7:8, :]
    prev = prev * (i % seq_tiles != 0).astype(F32)
    rid = lax.broadcasted_iota(jnp.int32, (tm, 1), 0)
    h_prev = jnp.where(rid == 0, prev, pltpu.roll(h, 1, axis=0))
    xx = h_prev - h
    mix = lambda j: (h + xx * mu_ref[j:j + 1, :]).astype(BF16)

    def put(ref, val):
        for p in range(ref.shape[0]):
            ref[p] = val[:, p * LANES:(p + 1) * LANES].astype(ref.dtype)

    put(r_ref, _dot(mix(0), wrkv_ref[0]))
    put(k_ref, _dot(mix(1), wrkv_ref[1]))
    put(v_ref, _dot(mix(2), wrkv_ref[2]))
    wl = jnp.tanh(_dot(mix(3), w1_ref[...])).astype(BF16)
    w_log = -_softplus(-(w0_ref[...] + _dot(wl, w2_ref[...]))) - 0.5
    lh, lm, ll = _split3(-jnp.exp(w_log))
    rr = lax.broadcasted_iota(jnp.int32, (tm, tm), 0)
    cc = lax.broadcasted_iota(jnp.int32, (tm, tm), 1)
    chunk_tril = ((rr // RW_CHUNK == cc // RW_CHUNK) & (rr >= cc)).astype(BF16)
    put(gc_ref, _dot(chunk_tril, lh) + _dot(chunk_tril, lm) + _dot(chunk_tril, ll))
    al = _dot(mix(4), a1_ref[...]).astype(BF16)
    put(a_ref, _sigmoid(a0_ref[...] + _dot(al, a2_ref[...])))
    gl = _sigmoid(_dot(mix(5), g1_ref[...])).astype(BF16)
    put(gate_ref, _dot(gl, g2_ref[...]))


def rwkv_proj(x, g_pre, mu, w_rkv, w0, w1, w2, a0, a1, a2, g1, g2, t, *, tm=512):
    n, d = x.shape
    tm = min(tm, t)
    p = d // LANES
    full = lambda a: pl.BlockSpec(a.shape, lambda i: (0,) * a.ndim)
    outs = pl.BlockSpec((p, tm, LANES), lambda i: (0, i, 0))
    args = (g_pre, mu, w_rkv, w0, w1, w2, a0, a1, a2, g1, g2)
    shp = lambda dt: jax.ShapeDtypeStruct((p, n, LANES), dt)
    return pl.pallas_call(
        functools.partial(_rwkv_proj_kernel, seq_tiles=t // tm),
        grid=(n // tm,),
        in_specs=[
            pl.BlockSpec((tm, d), lambda i: (i, 0)),
            pl.BlockSpec((8, d), lambda i: (jnp.maximum(i * (tm // 8) - 1, 0), 0)),
        ] + [full(a) for a in args],
        out_specs=[outs] * 6,
        out_shape=[shp(BF16), shp(BF16), shp(BF16), shp(F32), shp(BF16), shp(BF16)],
        compiler_params=_cparams(("parallel",)),
        name="rwkv_proj",
    )(x, x, *args)


def _rwkv_scan_kernel(r_ref, k_ref, v_ref, gc_ref, a_ref, gate_ref, kk_ref, ka_ref, rk_ref,
                      lng_ref, lnb_ref, out_ref, s_ref):
    C = RW_CHUNK
    N = RW_HEAD
    C2 = 2 * C
    nc = r_ref.shape[1] // C
    @pl.when(pl.program_id(1) == 0)
    def _():
        s_ref[...] = jnp.zeros_like(s_ref)

    lane = lax.broadcasted_iota(jnp.int32, (1, LANES), 1)
    in_a = lane < N
    m_a = in_a.astype(F32)
    m_b = 1.0 - m_a
    rr = lax.broadcasted_iota(jnp.int32, (C2, C2), 0)
    cc = lax.broadcasted_iota(jnp.int32, (C2, C2), 1)
    same = (rr // C) == (cc // C)
    strict = same & ((rr % C) > (cc % C))
    incl = same & ((rr % C) >= (cc % C))
    eye2 = (rr == cc).astype(F32)
    level_masks = []
    for lvl in range(int(math.log2(C))):
        rb, cb = (rr % C) >> lvl, (cc % C) >> lvl
        level_masks.append(same & ((rb & 1) == 1) & (cb == rb - 1))
    incl2 = jnp.concatenate([incl, incl], axis=1)
    first_row = lax.broadcasted_iota(jnp.int32, (C, 1), 0) == 0

    def seg_sum(z):
        sa = jnp.sum(z * m_a, axis=-1, keepdims=True)
        sb = jnp.sum(z * m_b, axis=-1, keepdims=True)
        return jnp.where(in_a, sa, sb)

    def stack(z):
        return jnp.concatenate([z * m_a, z * m_b], axis=0)

    def load(gi, sl):
        f32 = lambda ref: ref[gi, sl, :].astype(F32)
        return (f32(r_ref), f32(k_ref), f32(v_ref), gc_ref[gi, sl, :], f32(a_ref), f32(gate_ref), s_ref[gi],
                kk_ref[gi], ka_ref[gi], rk_ref[gi], lng_ref[gi], lnb_ref[gi])

    def step(r, k, v, gcum, a_sig, gate, st0, kk_w, ka_w, rk_w, ln_g, ln_b):
        kk = k * kk_w
        kk = kk / jnp.maximum(jnp.sqrt(seg_sum(kk * kk)), 1e-12)
        k = k * (1.0 + (a_sig - 1.0) * ka_w)
        a_ = -kk
        b_ = kk * a_sig
        g_end = gcum[C - 1:C, :]
        g_excl = jnp.where(first_row, 0.0, pltpu.roll(gcum, 1, axis=0))
        e_neg = jnp.exp(-gcum)
        e_rest = jnp.exp(g_end - gcum)
        a_h = stack(a_ * jnp.exp(g_excl))
        r_h = stack(r * jnp.exp(gcum))
        b_t = stack(b_ * e_rest)
        k_t = stack(k * e_rest)
        twice = lambda z: jnp.concatenate([z, z], axis=0)
        v_s = twice(v).astype(BF16)
        lhs = jnp.concatenate([a_h, r_h], axis=0).astype(BF16)
        rhs = jnp.concatenate([twice(b_ * e_neg), twice(k * e_neg)], axis=0).astype(BF16)
        sc = _dot_nt(lhs, rhs)
        yield
        a_ab = jnp.where(strict, sc[:C2, :C2], 0.0)
        a_ak = jnp.where(strict, sc[:C2, C2:], 0.0)
        a_r = jnp.where(incl2, sc[C2:, :], 0.0).astype(BF16)
        w1 = _dot(a_ak.astype(BF16), v_s)
        tinv = eye2 + jnp.where(level_masks[0], a_ab, 0.0)
        for mk in level_masks[1:]:
            tb = tinv.astype(BF16)
            l21 = jnp.where(mk, a_ab, 0.0).astype(BF16)
            tl = _dot(tb, l21)
            yield
            tinv = tinv + _dot(tl.astype(BF16), tb)
            yield
        au = _dot(tinv.astype(BF16), jnp.concatenate([a_h, w1], axis=1).astype(BF16))
        yield
        amat = jnp.concatenate([au.astype(BF16),
                                jnp.concatenate([jnp.zeros((C2, LANES), BF16), v_s], axis=1)], axis=0)
        pz = _dot_tn(jnp.concatenate([b_t, k_t], axis=0).astype(BF16), amat)
        ry = _dot(a_r, amat)
        yield
        r_bar = r_h + ry[:, :LANES]
        sy = _dot(jnp.concatenate([r_bar, pz[:, :LANES]], axis=0).astype(BF16), st0.astype(BF16))
        yield
        st_new = st0 * jnp.broadcast_to(jnp.exp(g_end), (LANES, LANES)).T + sy[C2:, :] + pz[:, LANES:]
        ys = sy[:C2, :] + ry[:, LANES:]
        y = jnp.where(in_a, ys[:C, :], ys[C:, :])
        mean = seg_sum(y) * (1.0 / N)
        yc = y - mean
        var = seg_sum(yc * yc) * (1.0 / N)
        yn = yc * lax.rsqrt(var + RW_LN_EPS) * ln_g + ln_b
        bonus = seg_sum(r * k * rk_w) * v
        return st_new, ((yn + bonus) * gate).astype(out_ref.dtype)

    def body(c, carry):
        sl = pl.ds(pl.multiple_of(c * C, C), C)
        groups = range(r_ref.shape[0])
        results = _lockstep(step(*load(gi, sl)) for gi in groups)
        for gi in groups:
            s_ref[gi], out_ref[gi, sl, :] = results[gi]
        return carry

    lax.fori_loop(0, nc, body, 0)


def rwkv_scan(r, k, v, gcum, a, gate, k_k, k_a, r_k, ln_g, ln_b, b, t):
    p, n, l = r.shape
    tb = min(RW_TIME_BLOCK, t)
    nt = t // tb
    seq = lambda: pl.BlockSpec((p, tb, l), lambda i, j: (0, i * nt + j, 0))
    par = lambda: pl.BlockSpec((p, 1, l), lambda i, j: (0, 0, 0))
    return pl.pallas_call(
        _rwkv_scan_kernel,
        grid=(b, nt),
        in_specs=[seq()] * 6 + [par()] * 5,
        out_specs=seq(),
        out_shape=jax.ShapeDtypeStruct((p, n, l), BF16),
        scratch_shapes=[pltpu.VMEM((p, l, l), F32)],
        compiler_params=_cparams(("parallel", "arbitrary")),
        name="rwkv_scan",
    )(r, k, v, gcum, a, gate, k_k, k_a, r_k, ln_g, ln_b)


def rwkv_sublayer(x, g_pre, mu, w_rkv, w0, w1, w2, a0, a1, a2, g1, g2,
                  k_k, k_a, r_k, ln_g, ln_b, w_out, g_post, b, t):
    bf = lambda a: a.astype(BF16)
    row = lambda a: a.reshape(1, -1)
    r, k, v, gcum, a, gate = rwkv_proj(x, g_pre, mu, bf(w_rkv), row(w0), bf(w1), bf(w2),
                                     row(a0), bf(a1), bf(a2), bf(g1), bf(g2), t)
    par = lambda a: a.reshape(RW_PAIRS, 1, LANES)
    y = rwkv_scan(r, k, v, gcum, a, gate, par(k_k), par(k_a), par(r_k), par(ln_g), par(ln_b), b, t)
    return out_proj_pairs_residual(y, bf(w_out), x, g_post)


def kernel(x, positions, norm_g, ffn_w_gu, ffn_w_down, ml_w_in, ml_b_if, ml_conv_w, ml_norm_g, ml_w_out,
           rw_mu, rw_w_rkv, rw_w0, rw_w1, rw_w2, rw_a0, rw_a1, rw_a2, rw_g1, rw_g2,
           rw_k_k, rw_k_a, rw_r_k, rw_ln_g, rw_ln_b, rw_w_out, rt_w_in, rt_w_out):
    b, t, d = x.shape
    depth = norm_g.shape[0]
    xf = x.reshape(b * t, d)
    for layer in range(depth):
        g = norm_g[layer][:, None, :]
        xf = ffn_sublayer(xf, g[0], ffn_w_gu[layer, 0].astype(BF16), ffn_w_down[layer, 0].astype(BF16), g[1])
        kind, j = layer % 3, layer // 3
        if kind == 0:
            xf = mlstm_sublayer(xf, g[2], ml_w_in[j], ml_b_if[j], ml_conv_w[j], ml_norm_g[j], ml_w_out[j],
                                g[3], b, t)
        elif kind == 1:
            xf = rwkv_sublayer(xf, g[2], rw_mu[j], rw_w_rkv[j], rw_w0[j], rw_w1[j], rw_w2[j],
                               rw_a0[j], rw_a1[j], rw_a2[j], rw_g1[j], rw_g2[j],
                               rw_k_k[j], rw_k_a[j], rw_r_k[j].reshape(-1), rw_ln_g[j], rw_ln_b[j],
                               rw_w_out[j], g[3], b, t)
        else:
            xf = retention_sublayer(xf, positions, g[2], rt_w_in[j], rt_w_out[j], g[3], b, t)
        xf = ffn_sublayer(xf, g[4], ffn_w_gu[layer, 1].astype(BF16), ffn_w_down[layer, 1].astype(BF16), g[5])
    return xf.reshape(b, t, d)
```

```python
import functools
import math

import jax
import jax.numpy as jnp
from jax import lax
from jax.experimental import pallas as pl
from jax.experimental.pallas import tpu as pltpu

F32 = jnp.float32
BF16 = jnp.bfloat16

D_MODEL = 1024
D_FF = 2816
NORM_EPS = 1e-6
HALF_STEP = 0.5

ML_HEADS = 4
ML_DQK = 128
ML_DV = 256
ML_CHUNK = 64
ML_GATE_CAP = 15.0
ML_ROWS = 2
ML_TIME_BLOCK = 512

RW_HEAD = 64
RW_PAIRS = D_MODEL // (2 * RW_HEAD)
RW_CHUNK = 64
RW_TIME_BLOCK = 512
RW_LN_EPS = 64e-5

RT_HEADS = 4
RT_DK = 256
RT_DV = 512
RT_CHUNK = 128
RT_TIME_BLOCK = 512
RT_ROPE_BASE = 10000.0

LANES = 128
VMEM_LIMIT = 56 * 1024 * 1024


def _cparams(sem):
    return pltpu.CompilerParams(dimension_semantics=sem, vmem_limit_bytes=VMEM_LIMIT)


def _lockstep(gens):
    gens = list(gens)
    results = [None] * len(gens)
    live = list(range(len(gens)))
    while live:
        for i in list(live):
            try:
                next(gens[i])
            except StopIteration as done:
                results[i] = done.value
                live.remove(i)
    return results


def _rms(x, g):
    return x * lax.rsqrt(jnp.mean(x * x, axis=-1, keepdims=True) + NORM_EPS) * g


def _dot(a, b):
    return jnp.dot(a, b, preferred_element_type=F32)


def _dot_nt(a, b):
    return lax.dot_general(a, b, (((1,), (1,)), ((), ())), preferred_element_type=F32)


def _dot_tn(a, b):
    return lax.dot_general(a, b, (((0,), (0,)), ((), ())), preferred_element_type=F32)


def _split3(x):
    hi = x.astype(BF16)
    r1 = x - hi.astype(F32)
    mid = r1.astype(BF16)
    lo = (r1 - mid.astype(F32)).astype(BF16)
    return hi, mid, lo


def _sigmoid(x):
    return 1.0 / (1.0 + jnp.exp(-x))


def _softplus(x):
    return jnp.maximum(x, 0.0) + jnp.log(1.0 + jnp.exp(-jnp.abs(x)))


def _ffn_kernel(x_ref, gpre_ref, wgu_ref, wd_ref, gpost_ref, o_ref):
    f = wd_ref.shape[0]
    x = x_ref[...]
    xn = _rms(x, gpre_ref[...]).astype(BF16)
    gate = _dot(xn, wgu_ref[:, :f])
    up = _dot(xn, wgu_ref[:, f:])
    act = (gate * _sigmoid(gate) * up).astype(BF16)
    h = _dot(act, wd_ref[...])
    o_ref[...] = x + HALF_STEP * _rms(h, gpost_ref[...])


def _resident(shape):
    return pl.BlockSpec(shape, lambda *_: (0,) * len(shape), pipeline_mode=pl.Buffered(1))


def ffn_sublayer(x, g_pre, w_gu, w_down, g_post, *, tm=512):
    n, d = x.shape
    tm = min(tm, n)
    return pl.pallas_call(
        _ffn_kernel,
        grid=(n // tm,),
        in_specs=[
            pl.BlockSpec((tm, d), lambda i: (i, 0)),
            _resident(g_pre.shape),
            _resident(w_gu.shape),
            _resident(w_down.shape),
            _resident(g_post.shape),
        ],
        out_specs=pl.BlockSpec((tm, d), lambda i: (i, 0)),
        out_shape=jax.ShapeDtypeStruct((n, d), F32),
        compiler_params=_cparams(("parallel",)),
        name="ffn",
    )(x, g_pre, w_gu, w_down, g_post)


def _norm_matmul_kernel(x_ref, g_ref, w_ref, o_ref):
    xn = _rms(x_ref[...], g_ref[...]).astype(BF16)
    o_ref[...] = _dot(xn, w_ref[...]).astype(o_ref.dtype)


def norm_matmul(x, g, w, out_dtype, *, tm=512):
    n, d = x.shape
    nout = w.shape[1]
    tm = min(tm, n)
    return pl.pallas_call(
        _norm_matmul_kernel,
        grid=(n // tm,),
        in_specs=[
            pl.BlockSpec((tm, d), lambda i: (i, 0)),
            _resident(g.shape),
            _resident(w.shape),
        ],
        out_specs=pl.BlockSpec((tm, nout), lambda i: (i, 0)),
        out_shape=jax.ShapeDtypeStruct((n, nout), out_dtype),
        compiler_params=_cparams(("parallel",)),
        name="norm_matmul",
    )(x, g, w)


def _out_proj_kernel(h_ref, w_ref, x_ref, g_ref, o_ref):
    y = _dot(h_ref[...], w_ref[...])
    o_ref[...] = x_ref[...] + _rms(y, g_ref[...])


def out_proj_residual(h, w, x, g, *, tm=1024):
    n, k = h.shape
    tm = min(tm, n)
    d = w.shape[1]
    return pl.pallas_call(
        _out_proj_kernel,
        grid=(n // tm,),
        in_specs=[
            pl.BlockSpec((tm, k), lambda i: (i, 0)),
            pl.BlockSpec((k, d), lambda i: (0, 0)),
            pl.BlockSpec((tm, d), lambda i: (i, 0)),
            pl.BlockSpec((1, d), lambda i: (0, 0)),
        ],
        out_specs=pl.BlockSpec((tm, d), lambda i: (i, 0)),
        out_shape=jax.ShapeDtypeStruct((n, d), F32),
        compiler_params=_cparams(("parallel",)),
        name="out_proj",
    )(h, w, x, g)


def _out_proj_pairs_kernel(h_ref, w_ref, x_ref, g_ref, o_ref):
    h = jnp.concatenate([h_ref[p] for p in range(h_ref.shape[0])], axis=-1)
    y = _dot(h, w_ref[...])
    o_ref[...] = x_ref[...] + _rms(y, g_ref[...])


def out_proj_pairs_residual(h, w, x, g, *, tm=1024):
    p, n, l = h.shape
    tm = min(tm, n)
    d = w.shape[1]
    return pl.pallas_call(
        _out_proj_pairs_kernel,
        grid=(n // tm,),
        in_specs=[
            pl.BlockSpec((p, tm, l), lambda i: (0, i, 0)),
            pl.BlockSpec((p * l, d), lambda i: (0, 0)),
            pl.BlockSpec((tm, d), lambda i: (i, 0)),
            pl.BlockSpec((1, d), lambda i: (0, 0)),
        ],
        out_specs=pl.BlockSpec((tm, d), lambda i: (i, 0)),
        out_shape=jax.ShapeDtypeStruct((n, d), F32),
        compiler_params=_cparams(("parallel",)),
        name="out_proj_pairs",
    )(h, w, x, g)


def _mlstm_proj_kernel(x_ref, g_ref, w_ref, wif_ref, bif_ref, proj_ref, gates_ref):
    H = ML_HEADS
    xn = _rms(x_ref[...], g_ref[...]).astype(BF16)
    proj_ref[...] = _dot(xn, w_ref[...]).astype(proj_ref.dtype)
    pre = _dot_nt(wif_ref[...], xn) + bif_ref[...]
    pre = ML_GATE_CAP * jnp.tanh(pre * (1.0 / ML_GATE_CAP))
    is_input_gate = lax.broadcasted_iota(jnp.int32, pre.shape, 0) < H
    gates_ref[...] = jnp.where(is_input_gate, pre, -_softplus(-pre))


def mlstm_proj(x, g, w_main, w_if_t, b_if, *, tm=512):
    n, d = x.shape
    tm = min(tm, n)
    nout = w_main.shape[1]
    ng = w_if_t.shape[0]
    return pl.pallas_call(
        _mlstm_proj_kernel,
        grid=(n // tm,),
        in_specs=[pl.BlockSpec((tm, d), lambda i: (i, 0)), _resident(g.shape), _resident(w_main.shape),
                  _resident(w_if_t.shape), _resident(b_if.shape)],
        out_specs=[pl.BlockSpec((tm, nout), lambda i: (i, 0)), pl.BlockSpec((ng, tm), lambda i: (0, i))],
        out_shape=[jax.ShapeDtypeStruct((n, nout), BF16), jax.ShapeDtypeStruct((ng, n), F32)],
        compiler_params=_cparams(("parallel",)),
        name="mlstm_proj",
    )(x, g, w_main, w_if_t, b_if)


def _mlstm_kernel(qk_ref, v_ref, o_ref, gates_ref, cw_ref, ng_ref, out_ref,
                  c_ref, n_ref, m_ref, tq_ref, tk_ref):
    L, H, dk, dv = ML_CHUNK, ML_HEADS, ML_DQK, ML_DV
    R = qk_ref.shape[0]
    nc = qk_ref.shape[1] // L
    KW = cw_ref.shape[0]
    TAIL = 8

    @pl.when(pl.program_id(1) == 0)
    def _():
        for ref in (c_ref, n_ref, m_ref, tq_ref, tk_ref):
            ref[...] = jnp.zeros_like(ref)

    row = lax.broadcasted_iota(jnp.int32, (L, L), 0)
    col = lax.broadcasted_iota(jnp.int32, (L, L), 1)
    tri = row >= col
    eye = row == col
    ones = jnp.ones((L, L), BF16)

    def conv_silu(x, tail, w):
        ext = jnp.concatenate([tail, x], axis=0)
        acc = x * w[KW - 1:KW, :]
        for i in range(KW - 1):
            off = TAIL - (KW - 1) + i
            acc = acc + ext[off:off + L, :] * w[i:i + 1, :]
        return acc * _sigmoid(acc)

    def step(r, hd, c, sl, m, tq, tk):
        ci = r * H + hd
        xq = qk_ref[r, sl, hd * dk:(hd + 1) * dk].astype(F32)
        xk = qk_ref[r, sl, (H + hd) * dk:(H + hd + 1) * dk].astype(F32)
        qc = (conv_silu(xq, tq, cw_ref[:, hd * dk:(hd + 1) * dk]) * (dk ** -0.5)).astype(BF16)
        kc = conv_silu(xk, tk, cw_ref[:, (H + hd) * dk:(H + hd + 1) * dk]).astype(BF16)
        vc = v_ref[r, sl, hd * dv:(hd + 1) * dv]
        irow = gates_ref[hd, r, pl.ds(c, 1), :]
        frow = gates_ref[H + hd, r, pl.ds(c, 1), :]
        fh, fm, fl = _split3(jnp.where(tri, frow, 0.0))
        bmat = _dot(fh, ones) + _dot(fm, ones) + _dot(fl, ones)
        brow = _dot_nt(ones, fh) + _dot_nt(ones, fm) + _dot_nt(ones, fl)
        qk = _dot_nt(qc, kc)
        cmat = c_ref[ci]
        nrow = n_ref[ci]
        yield
        log_d = jnp.where(tri, bmat - brow + irow, -jnp.inf)
        bcol = bmat[:, 0:1]
        log_inter = bcol + m
        m_t = jnp.maximum(log_inter, jnp.max(log_d, axis=-1, keepdims=True))
        s = qk * jnp.exp(log_d - m_t)
        inter = jnp.exp(log_inter - m_t)
        b_end = bmat[L - 1:L, 0:1]
        log_w = b_end - brow[0:1, :] + irow
        m_new = jnp.maximum(b_end + m, jnp.max(log_w, axis=-1, keepdims=True))
        w_row = jnp.exp(log_w - m_new)
        carry_decay = jnp.exp(b_end + m - m_new)
        wdiag = jnp.where(eye, w_row, 0.0).astype(BF16)
        wv = _dot(wdiag, vc).astype(BF16)
        n_add = _dot(jnp.broadcast_to(w_row, (8, L)).astype(BF16), kc)[0:1, :]
        num = _dot(s.astype(BF16), vc) + inter * _dot(qc, cmat.astype(BF16))
        yield
        c_new = carry_decay * cmat + _dot_tn(kc, wv)
        n_new = carry_decay * nrow + n_add
        qn = jnp.sum(qc.astype(F32) * nrow, axis=-1, keepdims=True)
        den = jnp.sum(s, axis=-1, keepdims=True) + inter * qn
        h = num / jnp.maximum(jnp.abs(den), jnp.exp(-m_t))
        ng = ng_ref[:, hd * dv:(hd + 1) * dv]
        hn = h * lax.rsqrt(jnp.mean(h * h, axis=-1, keepdims=True) + NORM_EPS) * ng
        og = o_ref[r, sl, hd * dv:(hd + 1) * dv].astype(F32)
        out = (hn * _sigmoid(og)).astype(out_ref.dtype)
        return (m_new, xq[L - TAIL:, :], xk[L - TAIL:, :]), c_new, n_new, out

    chains = [(r, hd) for r in range(R) for hd in range(H)]

    def body(c, carries):
        sl = pl.ds(pl.multiple_of(c * L, L), L)
        res = _lockstep(step(r, hd, c, sl, *carries[ci]) for ci, (r, hd) in enumerate(chains))
        for ci, (r, hd) in enumerate(chains):
            _, c_ref[ci], n_ref[ci], out_ref[r, sl, hd * dv:(hd + 1) * dv] = res[ci]
        return tuple(x[0] for x in res)

    init = tuple((m_ref[ci][0:1, 0:1], tq_ref[ci], tk_ref[ci]) for ci in range(len(chains)))
    final = lax.fori_loop(0, nc, body, init)
    for ci, (m, tq, tk) in enumerate(final):
        m_ref[ci] = jnp.broadcast_to(m, m_ref.shape[1:])
        tq_ref[ci] = tq
        tk_ref[ci] = tk


def mlstm_scan(proj, gates, conv_w, ng):
    b, t, _ = proj.shape
    h, dk, dv, L = ML_HEADS, ML_DQK, ML_DV, ML_CHUNK
    rows = min(ML_ROWS, b)
    tb = min(ML_TIME_BLOCK, t)
    wqk, wv = 2 * h * dk, h * dv
    nchains = rows * h
    seq = lambda w, col: pl.BlockSpec((rows, tb, w), lambda i, j: (i, j, col))
    return pl.pallas_call(
        _mlstm_kernel,
        grid=(b // rows, t // tb),
        in_specs=[seq(wqk, 0), seq(wv, wqk // wv), seq(wv, wqk // wv + 1),
                  pl.BlockSpec((2 * h, rows, tb // L, L), lambda i, j: (0, i, j, 0)),
                  _resident(conv_w.shape), _resident(ng.shape)],
        out_specs=pl.BlockSpec((rows, tb, wv), lambda i, j: (i, j, 0)),
        out_shape=jax.ShapeDtypeStruct((b, t, wv), BF16),
        scratch_shapes=[pltpu.VMEM((nchains, dk, dv), F32), pltpu.VMEM((nchains, 1, dk), F32),
                        pltpu.VMEM((nchains, 8, LANES), F32), pltpu.VMEM((nchains, 8, dk), F32),
                        pltpu.VMEM((nchains, 8, dk), F32)],
        compiler_params=_cparams(("parallel", "arbitrary")),
        name="mlstm_scan",
    )(proj, proj, proj, gates, conv_w, ng)


def mlstm_sublayer(x, g_pre, w_in, b_if, conv_w, norm_g, w_out, g_post, b, t):
    h, dk, dv, L = ML_HEADS, ML_DQK, ML_DV, ML_CHUNK
    n_main = 2 * h * dk + 2 * h * dv
    w_main = w_in[:, :n_main].astype(BF16)
    w_if_t = w_in[:, n_main:].T.astype(BF16)
    proj, gates = mlstm_proj(x, g_pre, w_main, w_if_t, b_if.reshape(2 * h, 1))
    hs = mlstm_scan(proj.reshape(b, t, n_main), gates.reshape(2 * h, b, t // L, L), conv_w, norm_g[None, :])
    return out_proj_residual(hs.reshape(b * t, h * dv), w_out.astype(BF16), x, g_post)


def _retention_kernel(lg_ref, q_ref, k_ref, v_ref, g_ref, cos_ref, sin_ref, out_ref, r_ref):
    L, H, dk, dv = RT_CHUNK, RT_HEADS, RT_DK, RT_DV
    half = dk // 2
    nc = q_ref.shape[1] // L

    @pl.when(pl.program_id(1) == 0)
    def _():
        r_ref[...] = jnp.zeros_like(r_ref)

    row = lax.broadcasted_iota(jnp.int32, (L, L), 0)
    col = lax.broadcasted_iota(jnp.int32, (L, L), 1)
    diff = jnp.maximum(row - col, 0).astype(F32)
    idx = lax.broadcasted_iota(jnp.int32, (L, 1), 0).astype(F32)

    def rotate(u, cos, sin):
        u1, u2 = u[:, :half], u[:, half:]
        return jnp.concatenate([u1 * cos - u2 * sin, u1 * sin + u2 * cos], axis=-1)

    def step(hd, sl, cos, sin):
        log_gamma = lg_ref[hd]
        qr = rotate(q_ref[0, sl, hd * dk:(hd + 1) * dk].astype(F32), cos, sin)
        kr = rotate(k_ref[0, sl, hd * dk:(hd + 1) * dk].astype(F32), cos, sin) * (dk ** -0.5)
        vc = v_ref[0, sl, hd * dv:(hd + 1) * dv]
        qb = qr.astype(BF16)
        qk = _dot_nt(qb, kr.astype(BF16))
        rmat = r_ref[hd]
        k_decay = jnp.exp(log_gamma * (L - 1.0 - idx))
        r_add = _dot_tn((kr * k_decay).astype(BF16), vc)
        cross = _dot(qb, rmat.astype(BF16))
        yield
        d_mask = jnp.where(row >= col, jnp.exp(log_gamma * diff), 0.0)
        q_decay = jnp.exp(log_gamma * (idx + 1.0))
        y = _dot((qk * d_mask).astype(BF16), vc) + cross * q_decay
        yield
        r_new = jnp.exp(jnp.full((1, 1), L, F32) * log_gamma) * rmat + r_add
        yn = y * lax.rsqrt(jnp.mean(y * y, axis=-1, keepdims=True) + NORM_EPS)
        gc = g_ref[0, sl, hd * dv:(hd + 1) * dv].astype(F32)
        return r_new, (gc * _sigmoid(gc) * yn).astype(out_ref.dtype)

    def body(c, carry):
        sl = pl.ds(pl.multiple_of(c * L, L), L)
        cos = cos_ref[0, sl, :]
        sin = sin_ref[0, sl, :]
        res = _lockstep(step(hd, sl, cos, sin) for hd in range(H))
        for hd in range(H):
            r_ref[hd], out_ref[0, sl, hd * dv:(hd + 1) * dv] = res[hd]
        return carry

    lax.fori_loop(0, nc, body, 0)


def retention_scan(proj, cos, sin, log_gamma):
    b, t, _ = proj.shape
    h, dk, dv = RT_HEADS, RT_DK, RT_DV
    tb = min(RT_TIME_BLOCK, t)
    wq, wv = h * dk, h * dv
    seq = lambda w, col: pl.BlockSpec((1, tb, w), lambda i, j: (i, j, col))
    return pl.pallas_call(
        _retention_kernel,
        grid=(b, t // tb),
        in_specs=[pl.BlockSpec(memory_space=pltpu.SMEM),
                  seq(wq, 0), seq(wq, 1), seq(wv, 2 * wq // wv), seq(wv, 2 * wq // wv + 1),
                  seq(dk // 2, 0), seq(dk // 2, 0)],
        out_specs=seq(wv, 0),
        out_shape=jax.ShapeDtypeStruct((b, t, wv), BF16),
        scratch_shapes=[pltpu.VMEM((h, dk, dv), F32)],
        compiler_params=_cparams(("parallel", "arbitrary")),
        name="retention_scan",
    )(log_gamma, proj, proj, proj, proj, cos, sin)


def retention_sublayer(x, positions, g_pre, w_in, w_out, g_post, b, t):
    h, dk, dv = RT_HEADS, RT_DK, RT_DV
    proj = norm_matmul(x, g_pre, w_in.astype(BF16), BF16).reshape(b, t, -1)
    inv_freq = 1.0 / (RT_ROPE_BASE ** jnp.linspace(0.0, 1.0, dk // 2, dtype=F32))
    ang = positions.astype(F32)[:, :, None] * inv_freq
    log_gamma = jnp.log(1.0 - 2.0 ** (-5.0 - jnp.arange(h, dtype=F32)))
    y = retention_scan(proj, jnp.cos(ang), jnp.sin(ang), log_gamma)
    return out_proj_residual(y.reshape(b * t, h * dv), w_out.astype(BF16), x, g_post)


def _rwkv_proj_kernel(x_ref, xp_ref, g_ref, mu_ref, wrkv_ref, w0_ref, w1_ref, w2_ref,
                      a0_ref, a1_ref, a2_ref, g1_ref, g2_ref,
                      r_ref, k_ref, v_ref, gc_ref, a_ref, gate_ref, *, seq_tiles):
    i = pl.program_id(0)
    tm = x_ref.shape[0]
    g = g_ref[...]
    h = _rms(x_ref[...], g)
    prev = _rms(xp_ref[...], g)[7:8, :]
    prev = prev * (i % seq_tiles != 0).astype(F32)
    rid = lax.broadcasted_iota(jnp.int32, (tm, 1), 0)
    h_prev = jnp.where(rid == 0, prev, pltpu.roll(h, 1, axis=0))
    xx = h_prev - h
    mix = lambda j: (h + xx * mu_ref[j:j + 1, :]).astype(BF16)

    def put(ref, val):
        for p in range(ref.shape[0]):
            ref[p] = val[:, p * LANES:(p + 1) * LANES].astype(ref.dtype)

    put(r_ref, _dot(mix(0), wrkv_ref[0]))
    put(k_ref, _dot(mix(1), wrkv_ref[1]))
    put(v_ref, _dot(mix(2), wrkv_ref[2]))
    wl = jnp.tanh(_dot(mix(3), w1_ref[...])).astype(BF16)
    w_log = -_softplus(-(w0_ref[...] + _dot(wl, w2_ref[...]))) - 0.5
    lh, lm, ll = _split3(-jnp.exp(w_log))
    rr = lax.broadcasted_iota(jnp.int32, (tm, tm), 0)
    cc = lax.broadcasted_iota(jnp.int32, (tm, tm), 1)
    chunk_tril = ((rr // RW_CHUNK == cc // RW_CHUNK) & (rr >= cc)).astype(BF16)
    put(gc_ref, _dot(chunk_tril, lh) + _dot(chunk_tril, lm) + _dot(chunk_tril, ll))
    al = _dot(mix(4), a1_ref[...]).astype(BF16)
    put(a_ref, _sigmoid(a0_ref[...] + _dot(al, a2_ref[...])))
    gl = _sigmoid(_dot(mix(5), g1_ref[...])).astype(BF16)
    put(gate_ref, _dot(gl, g2_ref[...]))


def rwkv_proj(x, g_pre, mu, w_rkv, w0, w1, w2, a0, a1, a2, g1, g2, t, *, tm=512):
    n, d = x.shape
    tm = min(tm, t)
    p = d // LANES
    full = lambda a: pl.BlockSpec(a.shape, lambda i: (0,) * a.ndim)
    outs = pl.BlockSpec((p, tm, LANES), lambda i: (0, i, 0))
    args = (g_pre, mu, w_rkv, w0, w1, w2, a0, a1, a2, g1, g2)
    shp = lambda dt: jax.ShapeDtypeStruct((p, n, LANES), dt)
    return pl.pallas_call(
        functools.partial(_rwkv_proj_kernel, seq_tiles=t // tm),
        grid=(n // tm,),
        in_specs=[
            pl.BlockSpec((tm, d), lambda i: (i, 0)),
            pl.BlockSpec((8, d), lambda i: (jnp.maximum(i * (tm // 8) - 1, 0), 0)),
        ] + [full(a) for a in args],
        out_specs=[outs] * 6,
        out_shape=[shp(BF16), shp(BF16), shp(BF16), shp(F32), shp(BF16), shp(BF16)],
        compiler_params=_cparams(("parallel",)),
        name="rwkv_proj",
    )(x, x, *args)


def _rwkv_scan_kernel(r_ref, k_ref, v_ref, gc_ref, a_ref, gate_ref, kk_ref, ka_ref, rk_ref,
                      lng_ref, lnb_ref, out_ref, s_ref):
    C = RW_CHUNK
    N = RW_HEAD
    C2 = 2 * C
    nc = r_ref.shape[1] // C

    @pl.when(pl.program_id(1) == 0)
    def _():
        s_ref[...] = jnp.zeros_like(s_ref)

    lane = lax.broadcasted_iota(jnp.int32, (1, LANES), 1)
    in_a = lane < N
    m_a = in_a.astype(F32)
    m_b = 1.0 - m_a
    rr = lax.broadcasted_iota(jnp.int32, (C2, C2), 0)
    cc = lax.broadcasted_iota(jnp.int32, (C2, C2), 1)
    same = (rr // C) == (cc // C)
    strict = same & ((rr % C) > (cc % C))
    incl = same & ((rr % C) >= (cc % C))
    eye2 = (rr == cc).astype(F32)
    level_masks = []
    for lvl in range(int(math.log2(C))):
        rb, cb = (rr % C) >> lvl, (cc % C) >> lvl
        level_masks.append(same & ((rb & 1) == 1) & (cb == rb - 1))
    incl2 = jnp.concatenate([incl, incl], axis=1)
    first_row = lax.broadcasted_iota(jnp.int32, (C, 1), 0) == 0

    def seg_sum(z):
        sa = jnp.sum(z * m_a, axis=-1, keepdims=True)
        sb = jnp.sum(z * m_b, axis=-1, keepdims=True)
        return jnp.where(in_a, sa, sb)

    def stack(z):
        return jnp.concatenate([z * m_a, z * m_b], axis=0)

    def load(gi, sl):
        f32 = lambda ref: ref[gi, sl, :].astype(F32)
        return (f32(r_ref), f32(k_ref), f32(v_ref), gc_ref[gi, sl, :], f32(a_ref), f32(gate_ref), s_ref[gi],
                kk_ref[gi], ka_ref[gi], rk_ref[gi], lng_ref[gi], lnb_ref[gi])

    def step(r, k, v, gcum, a_sig, gate, st0, kk_w, ka_w, rk_w, ln_g, ln_b):
        kk = k * kk_w
        kk = kk / jnp.maximum(jnp.sqrt(seg_sum(kk * kk)), 1e-12)
        k = k * (1.0 + (a_sig - 1.0) * ka_w)
        a_ = -kk
        b_ = kk * a_sig
        g_end = gcum[C - 1:C, :]
        g_excl = jnp.where(first_row, 0.0, pltpu.roll(gcum, 1, axis=0))
        e_neg = jnp.exp(-gcum)
        e_rest = jnp.exp(g_end - gcum)
        a_h = stack(a_ * jnp.exp(g_excl))
        r_h = stack(r * jnp.exp(gcum))
        b_t = stack(b_ * e_rest)
        k_t = stack(k * e_rest)
        twice = lambda z: jnp.concatenate([z, z], axis=0)
        v_s = twice(v).astype(BF16)
        lhs = jnp.concatenate([a_h, r_h], axis=0).astype(BF16)
        rhs = jnp.concatenate([twice(b_ * e_neg), twice(k * e_neg)], axis=0).astype(BF16)
        sc = _dot_nt(lhs, rhs)
        yield
        a_ab = jnp.where(strict, sc[:C2, :C2], 0.0)
        a_ak = jnp.where(strict, sc[:C2, C2:], 0.0)
        a_r = jnp.where(incl2, sc[C2:, :], 0.0).astype(BF16)
        w1 = _dot(a_ak.astype(BF16), v_s)
        tinv = eye2 + jnp.where(level_masks[0], a_ab, 0.0)
        for mk in level_masks[1:]:
            tb = tinv.astype(BF16)
            l21 = jnp.where(mk, a_ab, 0.0).astype(BF16)
            tl = _dot(tb, l21)
            yield
            tinv = tinv + _dot(tl.astype(BF16), tb)
            yield
        au = _dot(tinv.astype(BF16), jnp.concatenate([a_h, w1], axis=1).astype(BF16))
        yield
        amat = jnp.concatenate([au.astype(BF16),
                                jnp.concatenate([jnp.zeros((C2, LANES), BF16), v_s], axis=1)], axis=0)
        pz = _dot_tn(jnp.concatenate([b_t, k_t], axis=0).astype(BF16), amat)
        ry = _dot(a_r, amat)
        yield
        r_bar = r_h + ry[:, :LANES]
        sy = _dot(jnp.concatenate([r_bar, pz[:, :LANES]], axis=0).astype(BF16), st0.astype(BF16))
        yield
        st_new = st0 * jnp.broadcast_to(jnp.exp(g_end), (LANES, LANES)).T + sy[C2:, :] + pz[:, LANES:]
        ys = sy[:C2, :] + ry[:, LANES:]
        y = jnp.where(in_a, ys[:C, :], ys[C:, :])
        mean = seg_sum(y) * (1.0 / N)
        yc = y - mean
        var = seg_sum(yc * yc) * (1.0 / N)
        yn = yc * lax.rsqrt(var + RW_LN_EPS) * ln_g + ln_b
        bonus = seg_sum(r * k * rk_w) * v
        return st_new, ((yn + bonus) * gate).astype(out_ref.dtype)

    def body(c, carry):
        sl = pl.ds(pl.multiple_of(c * C, C), C)
        groups = range(r_ref.shape[0])
        results = _lockstep(step(*load(gi, sl)) for gi in groups)
        for gi in groups:
            s_ref[gi], out_ref[gi, sl, :] = results[gi]
        return carry

    lax.fori_loop(0, nc, body, 0)


def rwkv_scan(r, k, v, gcum, a, gate, k_k, k_a, r_k, ln_g, ln_b, b, t):
    p, n, l = r.shape
    tb = min(RW_TIME_BLOCK, t)
    nt = t // tb
    seq = lambda: pl.BlockSpec((p, tb, l), lambda i, j: (0, i * nt + j, 0))
    par = lambda: pl.BlockSpec((p, 1, l), lambda i, j: (0, 0, 0))
    return pl.pallas_call(
        _rwkv_scan_kernel,
        grid=(b, nt),
        in_specs=[seq()] * 6 + [par()] * 5,
        out_specs=seq(),
        out_shape=jax.ShapeDtypeStruct((p, n, l), BF16),
        scratch_shapes=[pltpu.VMEM((p, l, l), F32)],
        compiler_params=_cparams(("parallel", "arbitrary")),
        name="rwkv_scan",
    )(r, k, v, gcum, a, gate, k_k, k_a, r_k, ln_g, ln_b)


def rwkv_sublayer(x, g_pre, mu, w_rkv, w0, w1, w2, a0, a1, a2, g1, g2,
                  k_k, k_a, r_k, ln_g, ln_b, w_out, g_post, b, t):
    bf = lambda a: a.astype(BF16)
    row = lambda a: a.reshape(1, -1)
    r, k, v, gcum, a, gate = rwkv_proj(x, g_pre, mu, bf(w_rkv), row(w0), bf(w1), bf(w2),
                                     row(a0), bf(a1), bf(a2), bf(g1), bf(g2), t)
    par = lambda a: a.reshape(RW_PAIRS, 1, LANES)
    y = rwkv_scan(r, k, v, gcum, a, gate, par(k_k), par(k_a), par(r_k), par(ln_g), par(ln_b), b, t)
    return out_proj_pairs_residual(y, bf(w_out), x, g_post)


def kernel(x, positions, norm_g, ffn_w_gu, ffn_w_down, ml_w_in, ml_b_if, ml_conv_w, ml_norm_g, ml_w_out,
           rw_mu, rw_w_rkv, rw_w0, rw_w1, rw_w2, rw_a0, rw_a1, rw_a2, rw_g1, rw_g2,
           rw_k_k, rw_k_a, rw_r_k, rw_ln_g, rw_ln_b, rw_w_out, rt_w_in, rt_w_out):
    b, t, d = x.shape
    depth = norm_g.shape[0]
    xf = x.reshape(b * t, d)
    for layer in range(depth):
        g = norm_g[layer][:, None, :]
        xf = ffn_sublayer(xf, g[0], ffn_w_gu[layer, 0].astype(BF16), ffn_w_down[layer, 0].astype(BF16), g[1])
        kind, j = layer % 3, layer // 3
        if kind == 0:
            xf = mlstm_sublayer(xf, g[2], ml_w_in[j], ml_b_if[j], ml_conv_w[j], ml_norm_g[j], ml_w_out[j],
                                g[3], b, t)
        elif kind == 1:
            xf = rwkv_sublayer(xf, g[2], rw_mu[j], rw_w_rkv[j], rw_w0[j], rw_w1[j], rw_w2[j],
                               rw_a0[j], rw_a1[j], rw_a2[j], rw_g1[j], rw_g2[j],
                               rw_k_k[j], rw_k_a[j], rw_r_k[j].reshape(-1), rw_ln_g[j], rw_ln_b[j],
                               rw_w_out[j], g[3], b, t)
        else:
            xf = retention_sublayer(xf, positions, g[2], rt_w_in[j], rt_w_out[j], g[3], b, t)
        xf = ffn_sublayer(xf, g[4], ffn_w_gu[layer, 1].astype(BF16), ffn_w_down[layer, 1].astype(BF16), g[5])
    return xf.reshape(b, t, d)
```

```python
import functools
import math

import jax
import jax.numpy as jnp
from jax import lax
from jax.experimental import pallas as pl
from jax.experimental.pallas import tpu as pltpu

F32 = jnp.float32
BF16 = jnp.bfloat16

D_MODEL = 1024
D_FF = 2816
NORM_EPS = 1e-6
HALF_STEP = 0.5

ML_HEADS = 4
ML_DQK = 128
ML_DV = 256
ML_CHUNK = 128
ML_GATE_CAP = 15.0
ML_ROWS = 2
ML_TIME_BLOCK = 1024

RW_HEAD = 64
RW_PAIRS = D_MODEL // (2 * RW_HEAD)
RW_CHUNK = 64
RW_TIME_BLOCK = 512
RW_LN_EPS = 64e-5

RT_HEADS = 4
RT_DK = 256
RT_DV = 512
RT_CHUNK = 128
RT_TIME_BLOCK = 512
RT_ROPE_BASE = 10000.0

LANES = 128
VMEM_LIMIT = 56 * 1024 * 1024


def _cparams(sem):
    return pltpu.CompilerParams(dimension_semantics=sem, vmem_limit_bytes=VMEM_LIMIT)


def _lockstep(gens):
    gens = list(gens)
    results = [None] * len(gens)
    live = list(range(len(gens)))
    while live:
        for i in list(live):
            try:
                next(gens[i])
            except StopIteration as done:
                results[i] = done.value
                live.remove(i)
    return results


def _rms(x, g):
    return x * lax.rsqrt(jnp.mean(x * x, axis=-1, keepdims=True) + NORM_EPS) * g


def _dot(a, b):
    return jnp.dot(a, b, preferred_element_type=F32)


def _dot_nt(a, b):
    return lax.dot_general(a, b, (((1,), (1,)), ((), ())), preferred_element_type=F32)


def _dot_tn(a, b):
    return lax.dot_general(a, b, (((0,), (0,)), ((), ())), preferred_element_type=F32)


def _split3(x):
    hi = x.astype(BF16)
    r1 = x - hi.astype(F32)
    mid = r1.astype(BF16)
    lo = (r1 - mid.astype(F32)).astype(BF16)
    return hi, mid, lo


def _sigmoid(x):
    return 1.0 / (1.0 + jnp.exp(-x))


def _softplus(x):
    return jnp.maximum(x, 0.0) + jnp.log(1.0 + jnp.exp(-jnp.abs(x)))


def _ffn_kernel(x_ref, gpre_ref, wgu_ref, wd_ref, gpost_ref, o_ref):
    f = wd_ref.shape[0]
    half = x_ref.shape[0] // 2

    def rows(lo):
        x = x_ref[lo:lo + half, :]
        xn = _rms(x, gpre_ref[...]).astype(BF16)
        yield
        gate = _dot(xn, wgu_ref[:, :f])
        up = _dot(xn, wgu_ref[:, f:])
        yield
        act = (gate * _sigmoid(gate) * up).astype(BF16)
        h = _dot(act, wd_ref[...])
        yield
        o_ref[lo:lo + half, :] = x + HALF_STEP * _rms(h, gpost_ref[...])

    _lockstep(rows(lo) for lo in (0, half))


def _resident(shape):
    return pl.BlockSpec(shape, lambda *_: (0,) * len(shape), pipeline_mode=pl.Buffered(1))


def ffn_sublayer(x, g_pre, w_gu, w_down, g_post, *, tm=512):
    n, d = x.shape
    tm = min(tm, n)
    return pl.pallas_call(
        _ffn_kernel,
        grid=(n // tm,),
        in_specs=[
            pl.BlockSpec((tm, d), lambda i: (i, 0)),
            _resident(g_pre.shape),
            _resident(w_gu.shape),
            _resident(w_down.shape),
            _resident(g_post.shape),
        ],
        out_specs=pl.BlockSpec((tm, d), lambda i: (i, 0)),
        out_shape=jax.ShapeDtypeStruct((n, d), F32),
        compiler_params=_cparams(("parallel",)),
        name="ffn",
    )(x, g_pre, w_gu, w_down, g_post)


def _norm_matmul_kernel(x_ref, g_ref, w_ref, o_ref):
    xn = _rms(x_ref[...], g_ref[...]).astype(BF16)
    o_ref[...] = _dot(xn, w_ref[...]).astype(o_ref.dtype)


def norm_matmul(x, g, w, out_dtype, *, tm=512):
    n, d = x.shape
    nout = w.shape[1]
    tm = min(tm, n)
    return pl.pallas_call(
        _norm_matmul_kernel,
        grid=(n // tm,),
        in_specs=[
            pl.BlockSpec((tm, d), lambda i: (i, 0)),
            _resident(g.shape),
            _resident(w.shape),
        ],
        out_specs=pl.BlockSpec((tm, nout), lambda i: (i, 0)),
        out_shape=jax.ShapeDtypeStruct((n, nout), out_dtype),
        compiler_params=_cparams(("parallel",)),
        name="norm_matmul",
    )(x, g, w)


def _out_proj_kernel(h_ref, w_ref, x_ref, g_ref, o_ref):
    y = _dot(h_ref[...], w_ref[...])
    o_ref[...] = x_ref[...] + _rms(y, g_ref[...])


def out_proj_residual(h, w, x, g, *, tm=1024):
    n, k = h.shape
    tm = min(tm, n)
    d = w.shape[1]
    return pl.pallas_call(
        _out_proj_kernel,
        grid=(n // tm,),
        in_specs=[
            pl.BlockSpec((tm, k), lambda i: (i, 0)),
            pl.BlockSpec((k, d), lambda i: (0, 0)),
            pl.BlockSpec((tm, d), lambda i: (i, 0)),
            pl.BlockSpec((1, d), lambda i: (0, 0)),
        ],
        out_specs=pl.BlockSpec((tm, d), lambda i: (i, 0)),
        out_shape=jax.ShapeDtypeStruct((n, d), F32),
        compiler_params=_cparams(("parallel",)),
        name="out_proj",
    )(h, w, x, g)


def _out_proj_pairs_kernel(h_ref, w_ref, x_ref, g_ref, o_ref):
    h = jnp.concatenate([h_ref[p] for p in range(h_ref.shape[0])], axis=-1)
    y = _dot(h, w_ref[...])
    o_ref[...] = x_ref[...] + _rms(y, g_ref[...])


def out_proj_pairs_residual(h, w, x, g, *, tm=1024):
    p, n, l = h.shape
    tm = min(tm, n)
    d = w.shape[1]
    return pl.pallas_call(
        _out_proj_pairs_kernel,
        grid=(n // tm,),
        in_specs=[
            pl.BlockSpec((p, tm, l), lambda i: (0, i, 0)),
            pl.BlockSpec((p * l, d), lambda i: (0, 0)),
            pl.BlockSpec((tm, d), lambda i: (i, 0)),
            pl.BlockSpec((1, d), lambda i: (0, 0)),
        ],
        out_specs=pl.BlockSpec((tm, d), lambda i: (i, 0)),
        out_shape=jax.ShapeDtypeStruct((n, d), F32),
        compiler_params=_cparams(("parallel",)),
        name="out_proj_pairs",
    )(h, w, x, g)


def _mlstm_proj_kernel(x_ref, g_ref, w_ref, wif_ref, bif_ref, proj_ref, gates_ref):
    H = ML_HEADS
    xn = _rms(x_ref[...], g_ref[...]).astype(BF16)
    proj_ref[...] = _dot(xn, w_ref[...]).astype(proj_ref.dtype)
    pre = _dot_nt(wif_ref[...], xn) + bif_ref[...]
    pre = ML_GATE_CAP * jnp.tanh(pre * (1.0 / ML_GATE_CAP))
    is_input_gate = lax.broadcasted_iota(jnp.int32, pre.shape, 0) < H
    gates_ref[...] = jnp.where(is_input_gate, pre, -_softplus(-pre))


def mlstm_proj(x, g, w_main, w_if_t, b_if, *, tm=512):
    n, d = x.shape
    tm = min(tm, n)
    nout = w_main.shape[1]
    ng = w_if_t.shape[0]
    return pl.pallas_call(
        _mlstm_proj_kernel,
        grid=(n // tm,),
        in_specs=[pl.BlockSpec((tm, d), lambda i: (i, 0)), _resident(g.shape), _resident(w_main.shape),
                  _resident(w_if_t.shape), _resident(b_if.shape)],
        out_specs=[pl.BlockSpec((tm, nout), lambda i: (i, 0)), pl.BlockSpec((ng, tm), lambda i: (0, i))],
        out_shape=[jax.ShapeDtypeStruct((n, nout), BF16), jax.ShapeDtypeStruct((ng, n), F32)],
        compiler_params=_cparams(("parallel",)),
        name="mlstm_proj",
    )(x, g, w_main, w_if_t, b_if)


def _mlstm_kernel(qk_ref, v_ref, o_ref, gates_ref, cw_ref, ng_ref, out_ref,
                  c_ref, n_ref, m_ref, tq_ref, tk_ref):
    L, H, dk, dv = ML_CHUNK, ML_HEADS, ML_DQK, ML_DV
    R = qk_ref.shape[0]
    nc = qk_ref.shape[1] // L
    KW = cw_ref.shape[0]
    TAIL = 8

    @pl.when(pl.program_id(1) == 0)
    def _():
        for ref in (c_ref, n_ref, m_ref, tq_ref, tk_ref):
            ref[...] = jnp.zeros_like(ref)

    row = lax.broadcasted_iota(jnp.int32, (L, L), 0)
    col = lax.broadcasted_iota(jnp.int32, (L, L), 1)
    tri = row >= col
    eye = row == col
    ones = jnp.ones((L, L), BF16)

    def conv_silu(x, tail, w):
        ext = jnp.concatenate([tail, x], axis=0)
        acc = x * w[KW - 1:KW, :]
        for i in range(KW - 1):
            off = TAIL - (KW - 1) + i
            acc = acc + ext[off:off + L, :] * w[i:i + 1, :]
        return acc * _sigmoid(acc)

    def step(r, hd, c, sl, m, tq, tk):
        ci = r * H + hd
        xq = qk_ref[r, sl, hd * dk:(hd + 1) * dk].astype(F32)
        xk = qk_ref[r, sl, (H + hd) * dk:(H + hd + 1) * dk].astype(F32)
        qc = (conv_silu(xq, tq, cw_ref[:, hd * dk:(hd + 1) * dk]) * (dk ** -0.5)).astype(BF16)
        kc = conv_silu(xk, tk, cw_ref[:, (H + hd) * dk:(H + hd + 1) * dk]).astype(BF16)
        vc = v_ref[r, sl, hd * dv:(hd + 1) * dv]
        irow = gates_ref[hd, r, pl.ds(c, 1), :]
        frow = gates_ref[H + hd, r, pl.ds(c, 1), :]
        fh, fm, fl = _split3(jnp.where(tri, frow, 0.0))
        bmat = _dot(fh, ones) + _dot(fm, ones) + _dot(fl, ones)
        brow = _dot_nt(ones, fh) + _dot_nt(ones, fm) + _dot_nt(ones, fl)
        qk = _dot_nt(qc, kc)
        cmat = c_ref[ci]
        nrow = n_ref[ci]
        yield
        log_d = jnp.where(tri, bmat - brow + irow, -jnp.inf)
        bcol = bmat[:, 0:1]
        log_inter = bcol + m
        m_t = jnp.maximum(log_inter, jnp.max(log_d, axis=-1, keepdims=True))
        s = qk * jnp.exp(log_d - m_t)
        inter = jnp.exp(log_inter - m_t)
        b_end = bmat[L - 1:L, 0:1]
        log_w = b_end - brow[0:1, :] + irow
        m_new = jnp.maximum(b_end + m, jnp.max(log_w, axis=-1, keepdims=True))
        w_row = jnp.exp(log_w - m_new)
        carry_decay = jnp.exp(b_end + m - m_new)
        wdiag = jnp.where(eye, w_row, 0.0).astype(BF16)
        wv = _dot(wdiag, vc).astype(BF16)
        n_add = _dot(jnp.broadcast_to(w_row, (8, L)).astype(BF16), kc)[0:1, :]
        num = _dot(s.astype(BF16), vc) + inter * _dot(qc, cmat.astype(BF16))
        yield
        c_new = carry_decay * cmat + _dot_tn(kc, wv)
        n_new = carry_decay * nrow + n_add
        qn = jnp.sum(qc.astype(F32) * nrow, axis=-1, keepdims=True)
        den = jnp.sum(s, axis=-1, keepdims=True) + inter * qn
        h = num / jnp.maximum(jnp.abs(den), jnp.exp(-m_t))
        ng = ng_ref[:, hd * dv:(hd + 1) * dv]
        hn = h * lax.rsqrt(jnp.mean(h * h, axis=-1, keepdims=True) + NORM_EPS) * ng
        og = o_ref[r, sl, hd * dv:(hd + 1) * dv].astype(F32)
        out = (hn * _sigmoid(og)).astype(out_ref.dtype)
        return (m_new, xq[L - TAIL:, :], xk[L - TAIL:, :]), c_new, n_new, out

    chains = [(r, hd) for r in range(R) for hd in range(H)]

    def body(c, carries):
        sl = pl.ds(pl.multiple_of(c * L, L), L)
        res = _lockstep(step(r, hd, c, sl, *carries[ci]) for ci, (r, hd) in enumerate(chains))
        for ci, (r, hd) in enumerate(chains):
            _, c_ref[ci], n_ref[ci], out_ref[r, sl, hd * dv:(hd + 1) * dv] = res[ci]
        return tuple(x[0] for x in res)

    init = tuple((m_ref[ci][0:1, 0:1], tq_ref[ci], tk_ref[ci]) for ci in range(len(chains)))
    final = lax.fori_loop(0, nc, body, init)
    for ci, (m, tq, tk) in enumerate(final):
        m_ref[ci] = jnp.broadcast_to(m, m_ref.shape[1:])
        tq_ref[ci] = tq
        tk_ref[ci] = tk


def mlstm_scan(proj, gates, conv_w, ng):
    b, t, _ = proj.shape
    h, dk, dv, L = ML_HEADS, ML_DQK, ML_DV, ML_CHUNK
    rows = min(ML_ROWS, b)
    tb = min(ML_TIME_BLOCK, t)
    wqk, wv = 2 * h * dk, h * dv
    nchains = rows * h
    seq = lambda w, col: pl.BlockSpec((rows, tb, w), lambda i, j: (i, j, col))
    return pl.pallas_call(
        _mlstm_kernel,
        grid=(b // rows, t // tb),
        in_specs=[seq(wqk, 0), seq(wv, wqk // wv), seq(wv, wqk // wv + 1),
                  pl.BlockSpec((2 * h, rows, tb // L, L), lambda i, j: (0, i, j, 0)),
                  _resident(conv_w.shape), _resident(ng.shape)],
        out_specs=pl.BlockSpec((rows, tb, wv), lambda i, j: (i, j, 0)),
        out_shape=jax.ShapeDtypeStruct((b, t, wv), BF16),
        scratch_shapes=[pltpu.VMEM((nchains, dk, dv), F32), pltpu.VMEM((nchains, 1, dk), F32),
                        pltpu.VMEM((nchains, 8, LANES), F32), pltpu.VMEM((nchains, 8, dk), F32),
                        pltpu.VMEM((nchains, 8, dk), F32)],
        compiler_params=_cparams(("parallel", "arbitrary")),
        name="mlstm_scan",
    )(proj, proj, proj, gates, conv_w, ng)


def mlstm_sublayer(x, g_pre, w_in, b_if, conv_w, norm_g, w_out, g_post, b, t):
    h, dk, dv, L = ML_HEADS, ML_DQK, ML_DV, ML_CHUNK
    n_main = 2 * h * dk + 2 * h * dv
    w_main = w_in[:, :n_main].astype(BF16)
    w_if_t = w_in[:, n_main:].T.astype(BF16)
    proj, gates = mlstm_proj(x, g_pre, w_main, w_if_t, b_if.reshape(2 * h, 1))
    hs = mlstm_scan(proj.reshape(b, t, n_main), gates.reshape(2 * h, b, t // L, L), conv_w, norm_g[None, :])
    return out_proj_residual(hs.reshape(b * t, h * dv), w_out.astype(BF16), x, g_post)


def _retention_kernel(lg_ref, q_ref, k_ref, v_ref, g_ref, cos_ref, sin_ref, out_ref, r_ref):
    L, H, dk, dv = RT_CHUNK, RT_HEADS, RT_DK, RT_DV
    half = dk // 2
    nc = q_ref.shape[1] // L

    @pl.when(pl.program_id(1) == 0)
    def _():
        r_ref[...] = jnp.zeros_like(r_ref)

    row = lax.broadcasted_iota(jnp.int32, (L, L), 0)
    col = lax.broadcasted_iota(jnp.int32, (L, L), 1)
    diff = jnp.maximum(row - col, 0).astype(F32)
    idx = lax.broadcasted_iota(jnp.int32, (L, 1), 0).astype(F32)

    def rotate(u, cos, sin):
        u1, u2 = u[:, :half], u[:, half:]
        return jnp.concatenate([u1 * cos - u2 * sin, u1 * sin + u2 * cos], axis=-1)

    def step(hd, sl, cos, sin):
        log_gamma = lg_ref[hd]
        qr = rotate(q_ref[0, sl, hd * dk:(hd + 1) * dk].astype(F32), cos, sin)
        kr = rotate(k_ref[0, sl, hd * dk:(hd + 1) * dk].astype(F32), cos, sin) * (dk ** -0.5)
        vc = v_ref[0, sl, hd * dv:(hd + 1) * dv]
        qb = qr.astype(BF16)
        qk = _dot_nt(qb, kr.astype(BF16))
        rmat = r_ref[hd]
        k_decay = jnp.exp(log_gamma * (L - 1.0 - idx))
        r_add = _dot_tn((kr * k_decay).astype(BF16), vc)
        cross = _dot(qb, rmat.astype(BF16))
        yield
        d_mask = jnp.where(row >= col, jnp.exp(log_gamma * diff), 0.0)
        q_decay = jnp.exp(log_gamma * (idx + 1.0))
        y = _dot((qk * d_mask).astype(BF16), vc) + cross * q_decay
        yield
        r_new = jnp.exp(jnp.full((1, 1), L, F32) * log_gamma) * rmat + r_add
        yn = y * lax.rsqrt(jnp.mean(y * y, axis=-1, keepdims=True) + NORM_EPS)
        gc = g_ref[0, sl, hd * dv:(hd + 1) * dv].astype(F32)
        return r_new, (gc * _sigmoid(gc) * yn).astype(out_ref.dtype)

    def body(c, carry):
        sl = pl.ds(pl.multiple_of(c * L, L), L)
        cos = cos_ref[0, sl, :]
        sin = sin_ref[0, sl, :]
        res = _lockstep(step(hd, sl, cos, sin) for hd in range(H))
        for hd in range(H):
            r_ref[hd], out_ref[0, sl, hd * dv:(hd + 1) * dv] = res[hd]
        return carry

    lax.fori_loop(0, nc, body, 0)


def retention_scan(proj, cos, sin, log_gamma):
    b, t, _ = proj.shape
    h, dk, dv = RT_HEADS, RT_DK, RT_DV
    tb = min(RT_TIME_BLOCK, t)
    wq, wv = h * dk, h * dv
    seq = lambda w, col: pl.BlockSpec((1, tb, w), lambda i, j: (i, j, col))
    return pl.pallas_call(
        _retention_kernel,
        grid=(b, t // tb),
        in_specs=[pl.BlockSpec(memory_space=pltpu.SMEM),
                  seq(wq, 0), seq(wq, 1), seq(wv, 2 * wq // wv), seq(wv, 2 * wq // wv + 1),
                  seq(dk // 2, 0), seq(dk // 2, 0)],
        out_specs=seq(wv, 0),
        out_shape=jax.ShapeDtypeStruct((b, t, wv), BF16),
        scratch_shapes=[pltpu.VMEM((h, dk, dv), F32)],
        compiler_params=_cparams(("parallel", "arbitrary")),
        name="retention_scan",
    )(log_gamma, proj, proj, proj, proj, cos, sin)


def retention_sublayer(x, positions, g_pre, w_in, w_out, g_post, b, t):
    h, dk, dv = RT_HEADS, RT_DK, RT_DV
    proj = norm_matmul(x, g_pre, w_in.astype(BF16), BF16).reshape(b, t, -1)
    inv_freq = 1.0 / (RT_ROPE_BASE ** jnp.linspace(0.0, 1.0, dk // 2, dtype=F32))
    ang = positions.astype(F32)[:, :, None] * inv_freq
    log_gamma = jnp.log(1.0 - 2.0 ** (-5.0 - jnp.arange(h, dtype=F32)))
    y = retention_scan(proj, jnp.cos(ang), jnp.sin(ang), log_gamma)
    return out_proj_residual(y.reshape(b * t, h * dv), w_out.astype(BF16), x, g_post)


def _rwkv_proj_kernel(x_ref, xp_ref, g_ref, mu_ref, wrkv_ref, w0_ref, w1_ref, w2_ref,
                      a0_ref, a1_ref, a2_ref, g1_ref, g2_ref,
                      r_ref, k_ref, v_ref, gc_ref, a_ref, gate_ref, *, seq_tiles):
    i = pl.program_id(0)
    tm = x_ref.shape[0]
    g = g_ref[...]
    h = _rms(x_ref[...], g)
    prev = _rms(xp_ref[...], g)[7:8, :]
    prev = prev * (i % seq_tiles != 0).astype(F32)
    rid = lax.broadcasted_iota(jnp.int32, (tm, 1), 0)
    h_prev = jnp.where(rid == 0, prev, pltpu.roll(h, 1, axis=0))
    xx = h_prev - h
    mix = lambda j: (h + xx * mu_ref[j:j + 1, :]).astype(BF16)

    def put(ref, val):
        for p in range(ref.shape[0]):
            ref[p] = val[:, p * LANES:(p + 1) * LANES].astype(ref.dtype)

    put(r_ref, _dot(mix(0), wrkv_ref[0]))
    put(k_ref, _dot(mix(1), wrkv_ref[1]))
    put(v_ref, _dot(mix(2), wrkv_ref[2]))
    wl = jnp.tanh(_dot(mix(3), w1_ref[...])).astype(BF16)
    w_log = -_softplus(-(w0_ref[...] + _dot(wl, w2_ref[...]))) - 0.5
    lh, lm, ll = _split3(-jnp.exp(w_log))
    rr = lax.broadcasted_iota(jnp.int32, (tm, tm), 0)
    cc = lax.broadcasted_iota(jnp.int32, (tm, tm), 1)
    chunk_tril = ((rr // RW_CHUNK == cc // RW_CHUNK) & (rr >= cc)).astype(BF16)
    put(gc_ref, _dot(chunk_tril, lh) + _dot(chunk_tril, lm) + _dot(chunk_tril, ll))
    al = _dot(mix(4), a1_ref[...]).astype(BF16)
    put(a_ref, _sigmoid(a0_ref[...] + _dot(al, a2_ref[...])))
    gl = _sigmoid(_dot(mix(5), g1_ref[...])).astype(BF16)
    put(gate_ref, _dot(gl, g2_ref[...]))


def rwkv_proj(x, g_pre, mu, w_rkv, w0, w1, w2, a0, a1, a2, g1, g2, t, *, tm=512):
    n, d = x.shape
    tm = min(tm, t)
    p = d // LANES
    full = lambda a: pl.BlockSpec(a.shape, lambda i: (0,) * a.ndim)
    outs = pl.BlockSpec((p, tm, LANES), lambda i: (0, i, 0))
    args = (g_pre, mu, w_rkv, w0, w1, w2, a0, a1, a2, g1, g2)
    shp = lambda dt: jax.ShapeDtypeStruct((p, n, LANES), dt)
    return pl.pallas_call(
        functools.partial(_rwkv_proj_kernel, seq_tiles=t // tm),
        grid=(n // tm,),
        in_specs=[
            pl.BlockSpec((tm, d), lambda i: (i, 0)),
            pl.BlockSpec((8, d), lambda i: (jnp.maximum(i * (tm // 8) - 1, 0), 0)),
        ] + [full(a) for a in args],
        out_specs=[outs] * 6,
        out_shape=[shp(BF16), shp(BF16), shp(BF16), shp(F32), shp(BF16), shp(BF16)],
        compiler_params=_cparams(("parallel",)),
        name="rwkv_proj",
    )(x, x, *args)


def _rwkv_scan_kernel(r_ref, k_ref, v_ref, gc_ref, a_ref, gate_ref, kk_ref, ka_ref, rk_ref,
                      lng_ref, lnb_ref, out_ref, s_ref):
    C = RW_CHUNK
    N = RW_HEAD
    C2 = 2 * C
    nc = r_ref.shape[1] // C

    @pl.when(pl.program_id(1) == 0)
    def _():
        s_ref[...] = jnp.zeros_like(s_ref)

    lane = lax.broadcasted_iota(jnp.int32, (1, LANES), 1)
    in_a = lane < N
    m_a = in_a.astype(F32)
    m_b = 1.0 - m_a
    rr = lax.broadcasted_iota(jnp.int32, (C2, C2), 0)
    cc = lax.broadcasted_iota(jnp.int32, (C2, C2), 1)
    same = (rr // C) == (cc // C)
    strict = same & ((rr % C) > (cc % C))
    incl = same & ((rr % C) >= (cc % C))
    eye2 = (rr == cc).astype(F32)
    level_masks = []
    for lvl in range(int(math.log2(C))):
        rb, cb = (rr % C) >> lvl, (cc % C) >> lvl
        level_masks.append(same & ((rb & 1) == 1) & (cb == rb - 1))
    incl2 = jnp.concatenate([incl, incl], axis=1)
    first_row = lax.broadcasted_iota(jnp.int32, (C, 1), 0) == 0

    def seg_sum(z):
        sa = jnp.sum(z * m_a, axis=-1, keepdims=True)
        sb = jnp.sum(z * m_b, axis=-1, keepdims=True)
        return jnp.where(in_a, sa, sb)

    def stack(z):
        return jnp.concatenate([z * m_a, z * m_b], axis=0)

    def load(gi, sl):
        f32 = lambda ref: ref[gi, sl, :].astype(F32)
        return (f32(r_ref), f32(k_ref), f32(v_ref), gc_ref[gi, sl, :], f32(a_ref), f32(gate_ref), s_ref[gi],
                kk_ref[gi], ka_ref[gi], rk_ref[gi], lng_ref[gi], lnb_ref[gi])

    def step(r, k, v, gcum, a_sig, gate, st0, kk_w, ka_w, rk_w, ln_g, ln_b):
        kk = k * kk_w
        kk = kk / jnp.maximum(jnp.sqrt(seg_sum(kk * kk)), 1e-12)
        k = k * (1.0 + (a_sig - 1.0) * ka_w)
        a_ = -kk
        b_ = kk * a_sig
        g_end = gcum[C - 1:C, :]
        g_excl = jnp.where(first_row, 0.0, pltpu.roll(gcum, 1, axis=0))
        e_neg = jnp.exp(-gcum)
        e_rest = jnp.exp(g_end - gcum)
        a_h = stack(a_ * jnp.exp(g_excl))
        r_h = stack(r * jnp.exp(gcum))
        b_t = stack(b_ * e_rest)
        k_t = stack(k * e_rest)
        twice = lambda z: jnp.concatenate([z, z], axis=0)
        v_s = twice(v).astype(BF16)
        lhs = jnp.concatenate([a_h, r_h], axis=0).astype(BF16)
        rhs = jnp.concatenate([twice(b_ * e_neg), twice(k * e_neg)], axis=0).astype(BF16)
        sc = _dot_nt(lhs, rhs)
        yield
        a_ab = jnp.where(strict, sc[:C2, :C2], 0.0)
        a_ak = jnp.where(strict, sc[:C2, C2:], 0.0)
        a_r = jnp.where(incl2, sc[C2:, :], 0.0).astype(BF16)
        w1 = _dot(a_ak.astype(BF16), v_s)
        tinv = eye2 + jnp.where(level_masks[0], a_ab, 0.0)
        for mk in level_masks[1:]:
            tb = tinv.astype(BF16)
            l21 = jnp.where(mk, a_ab, 0.0).astype(BF16)
            tl = _dot(tb, l21)
            yield
            tinv = tinv + _dot(tl.astype(BF16), tb)
            yield
        au = _dot(tinv.astype(BF16), jnp.concatenate([a_h, w1], axis=1).astype(BF16))
        yield
        amat = jnp.concatenate([au.astype(BF16),
                                jnp.concatenate([jnp.zeros((C2, LANES), BF16), v_s], axis=1)], axis=0)
        pz = _dot_tn(jnp.concatenate([b_t, k_t], axis=0).astype(BF16), amat)
        ry = _dot(a_r, amat)
        yield
        r_bar = r_h + ry[:, :LANES]
        sy = _dot(jnp.concatenate([r_bar, pz[:, :LANES]], axis=0).astype(BF16), st0.astype(BF16))
        yield
        st_new = st0 * jnp.broadcast_to(jnp.exp(g_end), (LANES, LANES)).T + sy[C2:, :] + pz[:, LANES:]
        ys = sy[:C2, :] + ry[:, LANES:]
        y = jnp.where(in_a, ys[:C, :], ys[C:, :])
        mean = seg_sum(y) * (1.0 / N)
        yc = y - mean
        var = seg_sum(yc * yc) * (1.0 / N)
        yn = yc * lax.rsqrt(var + RW_LN_EPS) * ln_g + ln_b
        bonus = seg_sum(r * k * rk_w) * v
        return st_new, ((yn + bonus) * gate).astype(out_ref.dtype)

    def body(c, carry):
        sl = pl.ds(pl.multiple_of(c * C, C), C)
        groups = range(r_ref.shape[0])
        results = _lockstep(step(*load(gi, sl)) for gi in groups)
        for gi in groups:
            s_ref[gi], out_ref[gi, sl, :] = results[gi]
        return carry

    lax.fori_loop(0, nc, body, 0)


def rwkv_scan(r, k, v, gcum, a, gate, k_k, k_a, r_k, ln_g, ln_b, b, t):
    p, n, l = r.shape
    tb = min(RW_TIME_BLOCK, t)
    nt = t // tb
    seq = lambda: pl.BlockSpec((p, tb, l), lambda i, j: (0, i * nt + j, 0))
    par = lambda: pl.BlockSpec((p, 1, l), lambda i, j: (0, 0, 0))
    return pl.pallas_call(
        _rwkv_scan_kernel,
        grid=(b, nt),
        in_specs=[seq()] * 6 + [par()] * 5,
        out_specs=seq(),
        out_shape=jax.ShapeDtypeStruct((p, n, l), BF16),
        scratch_shapes=[pltpu.VMEM((p, l, l), F32)],
        compiler_params=_cparams(("parallel", "arbitrary")),
        name="rwkv_scan",
    )(r, k, v, gcum, a, gate, k_k, k_a, r_k, ln_g, ln_b)


def rwkv_sublayer(x, g_pre, mu, w_rkv, w0, w1, w2, a0, a1, a2, g1, g2,
                  k_k, k_a, r_k, ln_g, ln_b, w_out, g_post, b, t):
    bf = lambda a: a.astype(BF16)
    row = lambda a: a.reshape(1, -1)
    r, k, v, gcum, a, gate = rwkv_proj(x, g_pre, mu, bf(w_rkv), row(w0), bf(w1), bf(w2),
                                     row(a0), bf(a1), bf(a2), bf(g1), bf(g2), t)
    par = lambda a: a.reshape(RW_PAIRS, 1, LANES)
    y = rwkv_scan(r, k, v, gcum, a, gate, par(k_k), par(k_a), par(r_k), par(ln_g), par(ln_b), b, t)
    return out_proj_pairs_residual(y, bf(w_out), x, g_post)


def kernel(x, positions, norm_g, ffn_w_gu, ffn_w_down, ml_w_in, ml_b_if, ml_conv_w, ml_norm_g, ml_w_out,
           rw_mu, rw_w_rkv, rw_w0, rw_w1, rw_w2, rw_a0, rw_a1, rw_a2, rw_g1, rw_g2,
           rw_k_k, rw_k_a, rw_r_k, rw_ln_g, rw_ln_b, rw_w_out, rt_w_in, rt_w_out):
    b, t, d = x.shape
    depth = norm_g.shape[0]
    xf = x.reshape(b * t, d)
    for layer in range(depth):
        g = norm_g[layer][:, None, :]
        xf = ffn_sublayer(xf, g[0], ffn_w_gu[layer, 0].astype(BF16), ffn_w_down[layer, 0].astype(BF16), g[1])
        kind, j = layer % 3, layer // 3
        if kind == 0:
            xf = mlstm_sublayer(xf, g[2], ml_w_in[j], ml_b_if[j], ml_conv_w[j], ml_norm_g[j], ml_w_out[j],
                                g[3], b, t)
        elif kind == 1:
            xf = rwkv_sublayer(xf, g[2], rw_mu[j], rw_w_rkv[j], rw_w0[j], rw_w1[j], rw_w2[j],
                               rw_a0[j], rw_a1[j], rw_a2[j], rw_g1[j], rw_g2[j],
                               rw_k_k[j], rw_k_a[j], rw_r_k[j].reshape(-1), rw_ln_g[j], rw_ln_b[j],
                               rw_w_out[j], g[3], b, t)
        else:
            xf = retention_sublayer(xf, positions, g[2], rt_w_in[j], rt_w_out[j], g[3], b, t)
        xf = ffn_sublayer(xf, g[4], ffn_w_gu[layer, 1].astype(BF16), ffn_w_down[layer, 1].astype(BF16), g[5])
    return xf.reshape(b, t, d)
```

```python
import functools
import math

import jax
import jax.numpy as jnp
from jax import lax
from jax.experimental import pallas as pl
from jax.experimental.pallas import tpu as pltpu

F32 = jnp.float32
BF16 = jnp.bfloat16

D_MODEL = 1024
D_FF = 2816
NORM_EPS = 1e-6
HALF_STEP = 0.5

ML_HEADS = 4
ML_DQK = 128
ML_DV = 256
ML_CHUNK = 128
ML_GATE_CAP = 15.0
ML_ROWS = 2
ML_TIME_BLOCK = 1024

RW_HEAD = 64
RW_PAIRS = D_MODEL // (2 * RW_HEAD)
RW_CHUNK = 64
RW_TIME_BLOCK = 512
RW_LN_EPS = 64e-5

RT_HEADS = 4
RT_DK = 256
RT_DV = 512
RT_CHUNK = 128
RT_TIME_BLOCK = 512
RT_ROPE_BASE = 10000.0

LANES = 128
VMEM_LIMIT = 56 * 1024 * 1024


def _cparams(sem):
    return pltpu.CompilerParams(dimension_semantics=sem, vmem_limit_bytes=VMEM_LIMIT)


def _lockstep(gens):
    gens = list(gens)
    results = [None] * len(gens)
    live = list(range(len(gens)))
    while live:
        for i in list(live):
            try:
                next(gens[i])
            except StopIteration as done:
                results[i] = done.value
                live.remove(i)
    return results


def _rms(x, g):
    return x * lax.rsqrt(jnp.mean(x * x, axis=-1, keepdims=True) + NORM_EPS) * g


def _dot(a, b):
    return jnp.dot(a, b, preferred_element_type=F32)


def _dot_nt(a, b):
    return lax.dot_general(a, b, (((1,), (1,)), ((), ())), preferred_element_type=F32)


def _dot_tn(a, b):
    return lax.dot_general(a, b, (((0,), (0,)), ((), ())), preferred_element_type=F32)


def _split3(x):
    hi = x.astype(BF16)
    r1 = x - hi.astype(F32)
    mid = r1.astype(BF16)
    lo = (r1 - mid.astype(F32)).astype(BF16)
    return hi, mid, lo


def _sigmoid(x):
    return 1.0 / (1.0 + jnp.exp(-x))


def _softplus(x):
    return jnp.maximum(x, 0.0) + jnp.log(1.0 + jnp.exp(-jnp.abs(x)))


def _ffn_kernel(*refs, mixer):
    if mixer is None:
        x_ref, gpre_ref, wgu_ref, wd_ref, gpost_ref, o_ref = refs
    else:
        h_ref, wo_ref, gmix_ref, x_ref, gpre_ref, wgu_ref, wd_ref, gpost_ref, o_ref = refs
    f = wd_ref.shape[0]
    half = x_ref.shape[0] // 2

    def rows(lo):
        x = x_ref[lo:lo + half, :]
        if mixer is not None:
            if mixer == "pairs":
                h = jnp.concatenate([h_ref[p, lo:lo + half, :] for p in range(h_ref.shape[0])], axis=-1)
            else:
                h = h_ref[lo:lo + half, :]
            y = _dot(h, wo_ref[...])
            yield
            x = x + _rms(y, gmix_ref[...])
        xn = _rms(x, gpre_ref[...]).astype(BF16)
        yield
        gate = _dot(xn, wgu_ref[:, :f])
        up = _dot(xn, wgu_ref[:, f:])
        yield
        act = (gate * _sigmoid(gate) * up).astype(BF16)
        h2 = _dot(act, wd_ref[...])
        yield
        o_ref[lo:lo + half, :] = x + HALF_STEP * _rms(h2, gpost_ref[...])

    _lockstep(rows(lo) for lo in (0, half))


def _resident(shape):
    return pl.BlockSpec(shape, lambda *_: (0,) * len(shape), pipeline_mode=pl.Buffered(1))


def ffn_sublayer(x, g_pre, w_gu, w_down, g_post, mix=None, *, tm=512):
    n, d = x.shape
    tm = min(tm, n)
    ffn_args = (x, g_pre, w_gu, w_down, g_post)
    ffn_specs = [pl.BlockSpec((tm, d), lambda i: (i, 0)), _resident(g_pre.shape), _resident(w_gu.shape),
                 _resident(w_down.shape), _resident(g_post.shape)]
    mixer, mix_args, mix_specs = None, (), []
    if mix is not None:
        h, w_out, g_mix = mix
        if h.ndim == 3:
            mixer = "pairs"
            h_spec = pl.BlockSpec((h.shape[0], tm, h.shape[2]), lambda i: (0, i, 0))
        else:
            mixer = "rows"
            h_spec = pl.BlockSpec((tm, h.shape[1]), lambda i: (i, 0))
        mix_args = (h, w_out, g_mix)
        mix_specs = [h_spec, _resident(w_out.shape), _resident(g_mix.shape)]
    return pl.pallas_call(
        functools.partial(_ffn_kernel, mixer=mixer),
        grid=(n // tm,),
        in_specs=mix_specs + ffn_specs,
        out_specs=pl.BlockSpec((tm, d), lambda i: (i, 0)),
        out_shape=jax.ShapeDtypeStruct((n, d), F32),
        compiler_params=_cparams(("parallel",)),
        name="ffn" if mix is None else "mix_ffn",
    )(*mix_args, *ffn_args)


def _norm_matmul_kernel(x_ref, g_ref, w_ref, o_ref):
    xn = _rms(x_ref[...], g_ref[...]).astype(BF16)
    o_ref[...] = _dot(xn, w_ref[...]).astype(o_ref.dtype)


def norm_matmul(x, g, w, out_dtype, *, tm=512):
    n, d = x.shape
    nout = w.shape[1]
    tm = min(tm, n)
    return pl.pallas_call(
        _norm_matmul_kernel,
        grid=(n // tm,),
        in_specs=[
            pl.BlockSpec((tm, d), lambda i: (i, 0)),
            _resident(g.shape),
            _resident(w.shape),
        ],
        out_specs=pl.BlockSpec((tm, nout), lambda i: (i, 0)),
        out_shape=jax.ShapeDtypeStruct((n, nout), out_dtype),
        compiler_params=_cparams(("parallel",)),
        name="norm_matmul",
    )(x, g, w)


def _mlstm_proj_kernel(x_ref, g_ref, w_ref, wif_ref, bif_ref, proj_ref, gates_ref):
    H = ML_HEADS
    xn = _rms(x_ref[...], g_ref[...]).astype(BF16)
    proj_ref[...] = _dot(xn, w_ref[...]).astype(proj_ref.dtype)
    pre = _dot_nt(wif_ref[...], xn) + bif_ref[...]
    pre = ML_GATE_CAP * jnp.tanh(pre * (1.0 / ML_GATE_CAP))
    is_input_gate = lax.broadcasted_iota(jnp.int32, pre.shape, 0) < H
    gates_ref[...] = jnp.where(is_input_gate, pre, -_softplus(-pre))


def mlstm_proj(x, g, w_main, w_if_t, b_if, *, tm=512):
    n, d = x.shape
    tm = min(tm, n)
    nout = w_main.shape[1]
    ng = w_if_t.shape[0]
    return pl.pallas_call(
        _mlstm_proj_kernel,
        grid=(n // tm,),
        in_specs=[pl.BlockSpec((tm, d), lambda i: (i, 0)), _resident(g.shape), _resident(w_main.shape),
                  _resident(w_if_t.shape), _resident(b_if.shape)],
        out_specs=[pl.BlockSpec((tm, nout), lambda i: (i, 0)), pl.BlockSpec((ng, tm), lambda i: (0, i))],
        out_shape=[jax.ShapeDtypeStruct((n, nout), BF16), jax.ShapeDtypeStruct((ng, n), F32)],
        compiler_params=_cparams(("parallel",)),
        name="mlstm_proj",
    )(x, g, w_main, w_if_t, b_if)


def _mlstm_kernel(qk_ref, v_ref, o_ref, gates_ref, cw_ref, ng_ref, out_ref,
                  c_ref, n_ref, m_ref, tq_ref, tk_ref):
    L, H, dk, dv = ML_CHUNK, ML_HEADS, ML_DQK, ML_DV
    R = qk_ref.shape[0]
    nc = qk_ref.shape[1] // L
    KW = cw_ref.shape[0]
    TAIL = 8

    @pl.when(pl.program_id(1) == 0)
    def _():
        for ref in (c_ref, n_ref, m_ref, tq_ref, tk_ref):
            ref[...] = jnp.zeros_like(ref)

    row = lax.broadcasted_iota(jnp.int32, (L, L), 0)
    col = lax.broadcasted_iota(jnp.int32, (L, L), 1)
    tri = row >= col
    eye = row == col
    ones = jnp.ones((L, L), BF16)

    def conv_silu(x, tail, w):
        ext = jnp.concatenate([tail, x], axis=0)
        acc = x * w[KW - 1:KW, :]
        for i in range(KW - 1):
            off = TAIL - (KW - 1) + i
            acc = acc + ext[off:off + L, :] * w[i:i + 1, :]
        return acc * _sigmoid(acc)

    def step(r, hd, c, sl, m, tq, tk):
        ci = r * H + hd
        xq = qk_ref[r, sl, hd * dk:(hd + 1) * dk].astype(F32)
        xk = qk_ref[r, sl, (H + hd) * dk:(H + hd + 1) * dk].astype(F32)
        qc = (conv_silu(xq, tq, cw_ref[:, hd * dk:(hd + 1) * dk]) * (dk ** -0.5)).astype(BF16)
        kc = conv_silu(xk, tk, cw_ref[:, (H + hd) * dk:(H + hd + 1) * dk]).astype(BF16)
        vc = v_ref[r, sl, hd * dv:(hd + 1) * dv]
        irow = gates_ref[r, hd:hd + 1, sl]
        frow = gates_ref[r, H + hd:H + hd + 1, sl]
        fh, fm, fl = _split3(jnp.where(tri, frow, 0.0))
        bmat = _dot(fh, ones) + _dot(fm, ones) + _dot(fl, ones)
        brow = _dot_nt(ones, fh) + _dot_nt(ones, fm) + _dot_nt(ones, fl)
        qk = _dot_nt(qc, kc)
        cmat = c_ref[ci]
        nrow = n_ref[ci]
        yield
        log_d = jnp.where(tri, bmat - brow + irow, -jnp.inf)
        bcol = bmat[:, 0:1]
        log_inter = bcol + m
        m_t = jnp.maximum(log_inter, jnp.max(log_d, axis=-1, keepdims=True))
        s = qk * jnp.exp(log_d - m_t)
        inter = jnp.exp(log_inter - m_t)
        b_end = bmat[L - 1:L, 0:1]
        log_w = b_end - brow[0:1, :] + irow
        m_new = jnp.maximum(b_end + m, jnp.max(log_w, axis=-1, keepdims=True))
        w_row = jnp.exp(log_w - m_new)
        carry_decay = jnp.exp(b_end + m - m_new)
        wdiag = jnp.where(eye, w_row, 0.0).astype(BF16)
        wv = _dot(wdiag, vc).astype(BF16)
        n_add = _dot(jnp.broadcast_to(w_row, (8, L)).astype(BF16), kc)[0:1, :]
        num = _dot(s.astype(BF16), vc) + inter * _dot(qc, cmat.astype(BF16))
        yield
        c_new = carry_decay * cmat + _dot_tn(kc, wv)
        n_new = carry_decay * nrow + n_add
        qn = jnp.sum(qc.astype(F32) * nrow, axis=-1, keepdims=True)
        den = jnp.sum(s, axis=-1, keepdims=True) + inter * qn
        h = num / jnp.maximum(jnp.abs(den), jnp.exp(-m_t))
        ng = ng_ref[:, hd * dv:(hd + 1) * dv]
        hn = h * lax.rsqrt(jnp.mean(h * h, axis=-1, keepdims=True) + NORM_EPS) * ng
        og = o_ref[r, sl, hd * dv:(hd + 1) * dv].astype(F32)
        out = (hn * _sigmoid(og)).astype(out_ref.dtype)
        return (m_new, xq[L - TAIL:, :], xk[L - TAIL:, :]), c_new, n_new, out

    chains = [(r, hd) for r in range(R) for hd in range(H)]

    def body(c, carries):
        sl = pl.ds(pl.multiple_of(c * L, L), L)
        res = _lockstep(step(r, hd, c, sl, *carries[ci]) for ci, (r, hd) in enumerate(chains))
        for ci, (r, hd) in enumerate(chains):
            _, c_ref[ci], n_ref[ci], out_ref[r, sl, hd * dv:(hd + 1) * dv] = res[ci]
        return tuple(x[0] for x in res)

    init = tuple((m_ref[ci][0:1, 0:1], tq_ref[ci], tk_ref[ci]) for ci in range(len(chains)))
    final = lax.fori_loop(0, nc, body, init)
    for ci, (m, tq, tk) in enumerate(final):
        m_ref[ci] = jnp.broadcast_to(m, m_ref.shape[1:])
        tq_ref[ci] = tq
        tk_ref[ci] = tk


def mlstm_scan(proj, gates, conv_w, ng):
    b, t, _ = proj.shape
    h, dk, dv, L = ML_HEADS, ML_DQK, ML_DV, ML_CHUNK
    rows = min(ML_ROWS, b)
    tb = min(ML_TIME_BLOCK, t)
    wqk, wv = 2 * h * dk, h * dv
    nchains = rows * h
    seq = lambda w, col: pl.BlockSpec((rows, tb, w), lambda i, j: (i, j, col))
    return pl.pallas_call(
        _mlstm_kernel,
        grid=(b // rows, t // tb),
        in_specs=[seq(wqk, 0), seq(wv, wqk // wv), seq(wv, wqk // wv + 1),
                  pl.BlockSpec((rows, 2 * h, tb), lambda i, j: (i, 0, j)),
                  _resident(conv_w.shape), _resident(ng.shape)],
        out_specs=pl.BlockSpec((rows, tb, wv), lambda i, j: (i, j, 0)),
        out_shape=jax.ShapeDtypeStruct((b, t, wv), BF16),
        scratch_shapes=[pltpu.VMEM((nchains, dk, dv), F32), pltpu.VMEM((nchains, 1, dk), F32),
                        pltpu.VMEM((nchains, 8, LANES), F32), pltpu.VMEM((nchains, 8, dk), F32),
                        pltpu.VMEM((nchains, 8, dk), F32)],
        compiler_params=_cparams(("parallel", "arbitrary")),
        name="mlstm_scan",
    )(proj, proj, proj, gates, conv_w, ng)


def mlstm_mixer(x, g_pre, w_in, b_if, conv_w, norm_g, b, t):
    h, dk, dv, L = ML_HEADS, ML_DQK, ML_DV, ML_CHUNK
    n_main = 2 * h * dk + 2 * h * dv
    w_main = w_in[:, :n_main].astype(BF16)
    w_if_t = w_in[:, n_main:].T.astype(BF16)
    proj, gates = mlstm_proj(x, g_pre, w_main, w_if_t, b_if.reshape(2 * h, 1))
    gates = jnp.transpose(gates.reshape(2 * h, b, t), (1, 0, 2))
    hs = mlstm_scan(proj.reshape(b, t, n_main), gates, conv_w, norm_g[None, :])
    return hs.reshape(b * t, h * dv)


def _retention_kernel(lg_ref, q_ref, k_ref, v_ref, g_ref, cos_ref, sin_ref, out_ref, r_ref):
    L, H, dk, dv = RT_CHUNK, RT_HEADS, RT_DK, RT_DV
    half = dk // 2
    nc = q_ref.shape[1] // L

    @pl.when(pl.program_id(1) == 0)
    def _():
        r_ref[...] = jnp.zeros_like(r_ref)

    row = lax.broadcasted_iota(jnp.int32, (L, L), 0)
    col = lax.broadcasted_iota(jnp.int32, (L, L), 1)
    diff = jnp.maximum(row - col, 0).astype(F32)
    idx = lax.broadcasted_iota(jnp.int32, (L, 1), 0).astype(F32)

    def rotate(u, cos, sin):
        u1, u2 = u[:, :half], u[:, half:]
        return jnp.concatenate([u1 * cos - u2 * sin, u1 * sin + u2 * cos], axis=-1)

    def step(hd, sl, cos, sin):
        log_gamma = lg_ref[hd]
        qr = rotate(q_ref[0, sl, hd * dk:(hd + 1) * dk].astype(F32), cos, sin)
        kr = rotate(k_ref[0, sl, hd * dk:(hd + 1) * dk].astype(F32), cos, sin) * (dk ** -0.5)
        vc = v_ref[0, sl, hd * dv:(hd + 1) * dv]
        qb = qr.astype(BF16)
        qk = _dot_nt(qb, kr.astype(BF16))
        rmat = r_ref[hd]
        k_decay = jnp.exp(log_gamma * (L - 1.0 - idx))
        r_add = _dot_tn((kr * k_decay).astype(BF16), vc)
        cross = _dot(qb, rmat.astype(BF16))
        yield
        d_mask = jnp.where(row >= col, jnp.exp(log_gamma * diff), 0.0)
        q_decay = jnp.exp(log_gamma * (idx + 1.0))
        y = _dot((qk * d_mask).astype(BF16), vc) + cross * q_decay
        yield
        r_new = jnp.exp(jnp.full((1, 1), L, F32) * log_gamma) * rmat + r_add
        yn = y * lax.rsqrt(jnp.mean(y * y, axis=-1, keepdims=True) + NORM_EPS)
        gc = g_ref[0, sl, hd * dv:(hd + 1) * dv].astype(F32)
        return r_new, (gc * _sigmoid(gc) * yn).astype(out_ref.dtype)

    def body(c, carry):
        sl = pl.ds(pl.multiple_of(c * L, L), L)
        cos = cos_ref[0, sl, :]
        sin = sin_ref[0, sl, :]
        res = _lockstep(step(hd, sl, cos, sin) for hd in range(H))
        for hd in range(H):
            r_ref[hd], out_ref[0, sl, hd * dv:(hd + 1) * dv] = res[hd]
        return carry

    lax.fori_loop(0, nc, body, 0)


def retention_scan(proj, cos, sin, log_gamma):
    b, t, _ = proj.shape
    h, dk, dv = RT_HEADS, RT_DK, RT_DV
    tb = min(RT_TIME_BLOCK, t)
    wq, wv = h * dk, h * dv
    seq = lambda w, col: pl.BlockSpec((1, tb, w), lambda i, j: (i, j, col))
    return pl.pallas_call(
        _retention_kernel,
        grid=(b, t // tb),
        in_specs=[pl.BlockSpec(memory_space=pltpu.SMEM),
                  seq(wq, 0), seq(wq, 1), seq(wv, 2 * wq // wv), seq(wv, 2 * wq // wv + 1),
                  seq(dk // 2, 0), seq(dk // 2, 0)],
        out_specs=seq(wv, 0),
        out_shape=jax.ShapeDtypeStruct((b, t, wv), BF16),
        scratch_shapes=[pltpu.VMEM((h, dk, dv), F32)],
        compiler_params=_cparams(("parallel", "arbitrary")),
        name="retention_scan",
    )(log_gamma, proj, proj, proj, proj, cos, sin)


def retention_mixer(x, positions, g_pre, w_in, b, t):
    h, dk, dv = RT_HEADS, RT_DK, RT_DV
    proj = norm_matmul(x, g_pre, w_in.astype(BF16), BF16).reshape(b, t, -1)
    inv_freq = 1.0 / (RT_ROPE_BASE ** jnp.linspace(0.0, 1.0, dk // 2, dtype=F32))
    ang = positions.astype(F32)[:, :, None] * inv_freq
    log_gamma = jnp.log(1.0 - 2.0 ** (-5.0 - jnp.arange(h, dtype=F32)))
    y = retention_scan(proj, jnp.cos(ang), jnp.sin(ang), log_gamma)
    return y.reshape(b * t, h * dv)


def _rwkv_proj_kernel(x_ref, xp_ref, g_ref, mu_ref, wrkv_ref, w0_ref, w1_ref, w2_ref,
                      a0_ref, a1_ref, a2_ref, g1_ref, g2_ref,
                      r_ref, k_ref, v_ref, gc_ref, a_ref, gate_ref, *, seq_tiles):
    i = pl.program_id(0)
    tm = x_ref.shape[0]
    g = g_ref[...]
    h = _rms(x_ref[...], g)
    prev = _rms(xp_ref[...], g)[7:8, :]
    prev = prev * (i % seq_tiles != 0).astype(F32)
    rid = lax.broadcasted_iota(jnp.int32, (tm, 1), 0)
    h_prev = jnp.where(rid == 0, prev, pltpu.roll(h, 1, axis=0))
    xx = h_prev - h
    mix = lambda j: (h + xx * mu_ref[j:j + 1, :]).astype(BF16)

    def put(ref, val):
        for p in range(ref.shape[0]):
            ref[p] = val[:, p * LANES:(p + 1) * LANES].astype(ref.dtype)

    put(r_ref, _dot(mix(0), wrkv_ref[0]))
    put(k_ref, _dot(mix(1), wrkv_ref[1]))
    put(v_ref, _dot(mix(2), wrkv_ref[2]))
    wl = jnp.tanh(_dot(mix(3), w1_ref[...])).astype(BF16)
    w_log = -_softplus(-(w0_ref[...] + _dot(wl, w2_ref[...]))) - 0.5
    lh, lm, ll = _split3(-jnp.exp(w_log))
    rr = lax.broadcasted_iota(jnp.int32, (tm, tm), 0)
    cc = lax.broadcasted_iota(jnp.int32, (tm, tm), 1)
    chunk_tril = ((rr // RW_CHUNK == cc // RW_CHUNK) & (rr >= cc)).astype(BF16)
    put(gc_ref, _dot(chunk_tril, lh) + _dot(chunk_tril, lm) + _dot(chunk_tril, ll))
    al = _dot(mix(4), a1_ref[...]).astype(BF16)
    put(a_ref, _sigmoid(a0_ref[...] + _dot(al, a2_ref[...])))
    gl = _sigmoid(_dot(mix(5), g1_ref[...])).astype(BF16)
    put(gate_ref, _dot(gl, g2_ref[...]))


def rwkv_proj(x, g_pre, mu, w_rkv, w0, w1, w2, a0, a1, a2, g1, g2, t, *, tm=512):
    n, d = x.shape
    tm = min(tm, t)
    p = d // LANES
    full = lambda a: pl.BlockSpec(a.shape, lambda i: (0,) * a.ndim)
    outs = pl.BlockSpec((p, tm, LANES), lambda i: (0, i, 0))
    args = (g_pre, mu, w_rkv, w0, w1, w2, a0, a1, a2, g1, g2)
    shp = lambda dt: jax.ShapeDtypeStruct((p, n, LANES), dt)
    return pl.pallas_call(
        functools.partial(_rwkv_proj_kernel, seq_tiles=t // tm),
        grid=(n // tm,),
        in_specs=[
            pl.BlockSpec((tm, d), lambda i: (i, 0)),
            pl.BlockSpec((8, d), lambda i: (jnp.maximum(i * (tm // 8) - 1, 0), 0)),
        ] + [full(a) for a in args],
        out_specs=[outs] * 6,
        out_shape=[shp(BF16), shp(BF16), shp(BF16), shp(F32), shp(BF16), shp(BF16)],
        compiler_params=_cparams(("parallel",)),
        name="rwkv_proj",
    )(x, x, *args)


def _rwkv_scan_kernel(r_ref, k_ref, v_ref, gc_ref, a_ref, gate_ref, kk_ref, ka_ref, rk_ref,
                      lng_ref, lnb_ref, out_ref, s_ref):
    C = RW_CHUNK
    N = RW_HEAD
    C2 = 2 * C
    nc = r_ref.shape[1] // C

    @pl.when(pl.program_id(1) == 0)
    def _():
        s_ref[...] = jnp.zeros_like(s_ref)

    lane = lax.broadcasted_iota(jnp.int32, (1, LANES), 1)
    in_a = lane < N
    m_a = in_a.astype(F32)
    m_b = 1.0 - m_a
    rr = lax.broadcasted_iota(jnp.int32, (C2, C2), 0)
    cc = lax.broadcasted_iota(jnp.int32, (C2, C2), 1)
    same = (rr // C) == (cc // C)
    strict = same & ((rr % C) > (cc % C))
    incl = same & ((rr % C) >= (cc % C))
    eye2 = (rr == cc).astype(F32)
    level_masks = []
    for lvl in range(int(math.log2(C))):
        rb, cb = (rr % C) >> lvl, (cc % C) >> lvl
        level_masks.append(same & ((rb & 1) == 1) & (cb == rb - 1))
    incl2 = jnp.concatenate([incl, incl], axis=1)
    first_row = lax.broadcasted_iota(jnp.int32, (C, 1), 0) == 0

    def seg_sum(z):
        sa = jnp.sum(z * m_a, axis=-1, keepdims=True)
        sb = jnp.sum(z * m_b, axis=-1, keepdims=True)
        return jnp.where(in_a, sa, sb)

    def stack(z):
        return jnp.concatenate([z * m_a, z * m_b], axis=0)

    def load(gi, sl):
        f32 = lambda ref: ref[gi, sl, :].astype(F32)
        return (f32(r_ref), f32(k_ref), f32(v_ref), gc_ref[gi, sl, :], f32(a_ref), f32(gate_ref), s_ref[gi],
                kk_ref[gi], ka_ref[gi], rk_ref[gi], lng_ref[gi], lnb_ref[gi])

    def step(r, k, v, gcum, a_sig, gate, st0, kk_w, ka_w, rk_w, ln_g, ln_b):
        kk = k * kk_w
        kk = kk / jnp.maximum(jnp.sqrt(seg_sum(kk * kk)), 1e-12)
        k = k * (1.0 + (a_sig - 1.0) * ka_w)
        a_ = -kk
        b_ = kk * a_sig
        g_end = gcum[C - 1:C, :]
        g_excl = jnp.where(first_row, 0.0, pltpu.roll(gcum, 1, axis=0))
        e_neg = jnp.exp(-gcum)
        e_rest = jnp.exp(g_end - gcum)
        a_h = stack(a_ * jnp.exp(g_excl))
        r_h = stack(r * jnp.exp(gcum))
        b_t = stack(b_ * e_rest)
        k_t = stack(k * e_rest)
        twice = lambda z: jnp.concatenate([z, z], axis=0)
        v_s = twice(v).astype(BF16)
        lhs = jnp.concatenate([a_h, r_h], axis=0).astype(BF16)
        rhs = jnp.concatenate([twice(b_ * e_neg), twice(k * e_neg)], axis=0).astype(BF16)
        sc = _dot_nt(lhs, rhs)
        yield
        a_ab = jnp.where(strict, sc[:C2, :C2], 0.0)
        a_ak = jnp.where(strict, sc[:C2, C2:], 0.0)
        a_r = jnp.where(incl2, sc[C2:, :], 0.0).astype(BF16)
        w1 = _dot(a_ak.astype(BF16), v_s)
        tinv = eye2 + jnp.where(level_masks[0], a_ab, 0.0)
        for mk in level_masks[1:]:
            tb = tinv.astype(BF16)
            l21 = jnp.where(mk, a_ab, 0.0).astype(BF16)
            tl = _dot(tb, l21)
            yield
            tinv = tinv + _dot(tl.astype(BF16), tb)
            yield
        au = _dot(tinv.astype(BF16), jnp.concatenate([a_h, w1], axis=1).astype(BF16))
        yield
        amat = jnp.concatenate([au.astype(BF16),
                                jnp.concatenate([jnp.zeros((C2, LANES), BF16), v_s], axis=1)], axis=0)
        pz = _dot_tn(jnp.concatenate([b_t, k_t], axis=0).astype(BF16), amat)
        ry = _dot(a_r, amat)
        yield
        r_bar = r_h + ry[:, :LANES]
        sy = _dot(jnp.concatenate([r_bar, pz[:, :LANES]], axis=0).astype(BF16), st0.astype(BF16))
        yield
        st_new = st0 * jnp.broadcast_to(jnp.exp(g_end), (LANES, LANES)).T + sy[C2:, :] + pz[:, LANES:]
        ys = sy[:C2, :] + ry[:, LANES:]
        y = jnp.where(in_a, ys[:C, :], ys[C:, :])
        mean = seg_sum(y) * (1.0 / N)
        yc = y - mean
        var = seg_sum(yc * yc) * (1.0 / N)
        yn = yc * lax.rsqrt(var + RW_LN_EPS) * ln_g + ln_b
        bonus = seg_sum(r * k * rk_w) * v
        return st_new, ((yn + bonus) * gate).astype(out_ref.dtype)

    def body(c, carry):
        sl = pl.ds(pl.multiple_of(c * C, C), C)
        groups = range(r_ref.shape[0])
        results = _lockstep(step(*load(gi, sl)) for gi in groups)
        for gi in groups:
            s_ref[gi], out_ref[gi, sl, :] = results[gi]
        return carry

    lax.fori_loop(0, nc, body, 0)


def rwkv_scan(r, k, v, gcum, a, gate, k_k, k_a, r_k, ln_g, ln_b, b, t):
    p, n, l = r.shape
    tb = min(RW_TIME_BLOCK, t)
    nt = t // tb
    seq = lambda: pl.BlockSpec((p, tb, l), lambda i, j: (0, i * nt + j, 0))
    par = lambda: pl.BlockSpec((p, 1, l), lambda i, j: (0, 0, 0))
    return pl.pallas_call(
        _rwkv_scan_kernel,
        grid=(b, nt),
        in_specs=[seq()] * 6 + [par()] * 5,
        out_specs=seq(),
        out_shape=jax.ShapeDtypeStruct((p, n, l), BF16),
        scratch_shapes=[pltpu.VMEM((p, l, l), F32)],
        compiler_params=_cparams(("parallel", "arbitrary")),
        name="rwkv_scan",
    )(r, k, v, gcum, a, gate, k_k, k_a, r_k, ln_g, ln_b)


def rwkv_mixer(x, g_pre, mu, w_rkv, w0, w1, w2, a0, a1, a2, g1, g2,
               k_k, k_a, r_k, ln_g, ln_b, b, t):
    bf = lambda a: a.astype(BF16)
    row = lambda a: a.reshape(1, -1)
    r, k, v, gcum, a, gate = rwkv_proj(x, g_pre, mu, bf(w_rkv), row(w0), bf(w1), bf(w2),
                                     row(a0), bf(a1), bf(a2), bf(g1), bf(g2), t)
    par = lambda a: a.reshape(RW_PAIRS, 1, LANES)
    return rwkv_scan(r, k, v, gcum, a, gate, par(k_k), par(k_a), par(r_k), par(ln_g), par(ln_b), b, t)


def kernel(x, positions, norm_g, ffn_w_gu, ffn_w_down, ml_w_in, ml_b_if, ml_conv_w, ml_norm_g, ml_w_out,
           rw_mu, rw_w_rkv, rw_w0, rw_w1, rw_w2, rw_a0, rw_a1, rw_a2, rw_g1, rw_g2,
           rw_k_k, rw_k_a, rw_r_k, rw_ln_g, rw_ln_b, rw_w_out, rt_w_in, rt_w_out):
    b, t, d = x.shape
    depth = norm_g.shape[0]
    xf = x.reshape(b * t, d)
    for layer in range(depth):
        g = norm_g[layer][:, None, :]
        xf = ffn_sublayer(xf, g[0], ffn_w_gu[layer, 0].astype(BF16), ffn_w_down[layer, 0].astype(BF16), g[1])
        kind, j = layer % 3, layer // 3
        if kind == 0:
            h = mlstm_mixer(xf, g[2], ml_w_in[j], ml_b_if[j], ml_conv_w[j], ml_norm_g[j], b, t)
            w_out = ml_w_out[j]
        elif kind == 1:
            h = rwkv_mixer(xf, g[2], rw_mu[j], rw_w_rkv[j], rw_w0[j], rw_w1[j], rw_w2[j],
                           rw_a0[j], rw_a1[j], rw_a2[j], rw_g1[j], rw_g2[j],
                           rw_k_k[j], rw_k_a[j], rw_r_k[j].reshape(-1), rw_ln_g[j], rw_ln_b[j], b, t)
            w_out = rw_w_out[j]
        else:
            h = retention_mixer(xf, positions, g[2], rt_w_in[j], b, t)
            w_out = rt_w_out[j]
        xf = ffn_sublayer(xf, g[4], ffn_w_gu[layer, 1].astype(BF16), ffn_w_down[layer, 1].astype(BF16), g[5],
                          mix=(h, w_out.astype(BF16), g[3]))
    return xf.reshape(b, t, d)
```

```python
import functools
import math

import jax
import jax.numpy as jnp
from jax import lax
from jax.experimental import pallas as pl
from jax.experimental.pallas import tpu as pltpu

F32 = jnp.float32
BF16 = jnp.bfloat16

D_MODEL = 1024
D_FF = 2816
NORM_EPS = 1e-6
HALF_STEP = 0.5

ML_HEADS = 4
ML_DQK = 128
ML_DV = 256
ML_CHUNK = 256
ML_GATE_CAP = 15.0
ML_ROWS = 2
ML_TIME_BLOCK = 1024

RW_HEAD = 64
RW_PAIRS = D_MODEL // (2 * RW_HEAD)
RW_CHUNK = 64
RW_TIME_BLOCK = 256
RW_ROWS = 2
RW_LN_EPS = 64e-5

RT_HEADS = 4
RT_DK = 256
RT_DV = 512
RT_CHUNK = 256
RT_TIME_BLOCK = 512
RT_ROPE_BASE = 10000.0

LANES = 128
MXU_DEPTH = 256
VMEM_LIMIT = 56 * 1024 * 1024


def _cparams(sem):
    return pltpu.CompilerParams(dimension_semantics=sem, vmem_limit_bytes=VMEM_LIMIT)


def _lockstep(gens):
    gens = list(gens)
    results = [None] * len(gens)
    live = list(range(len(gens)))
    while live:
        for i in list(live):
            try:
                next(gens[i])
            except StopIteration as done:
                results[i] = done.value
                live.remove(i)
    return results


def _rms(x, g):
    return x * lax.rsqrt(jnp.mean(x * x, axis=-1, keepdims=True) + NORM_EPS) * g


def _dot(a, b):
    return jnp.dot(a, b, preferred_element_type=F32)


def _dot_nt(a, b):
    return lax.dot_general(a, b, (((1,), (1,)), ((), ())), preferred_element_type=F32)


def _dot_tn(a, b):
    return lax.dot_general(a, b, (((0,), (0,)), ((), ())), preferred_element_type=F32)


def _split3(x):
    hi = x.astype(BF16)
    r1 = x - hi.astype(F32)
    mid = r1.astype(BF16)
    lo = (r1 - mid.astype(F32)).astype(BF16)
    return hi, mid, lo


def _sigmoid(x):
    return 1.0 / (1.0 + jnp.exp(-x))


def _softplus(x):
    return jnp.maximum(x, 0.0) + jnp.log(1.0 + jnp.exp(-jnp.abs(x)))


def _ffn_kernel(*refs, mixer):
    if mixer is None:
        x_ref, gpre_ref, wgu_ref, wd_ref, gpost_ref, o_ref = refs
    else:
        h_ref, wo_ref, gmix_ref, x_ref, gpre_ref, wgu_ref, wd_ref, gpost_ref, o_ref = refs
    f = wd_ref.shape[0]
    half = x_ref.shape[0] // 2

    def rows(lo):
        x = x_ref[lo:lo + half, :]
        if mixer is not None:
            if mixer == "pairs":
                h = jnp.concatenate([h_ref[p, lo:lo + half, :] for p in range(h_ref.shape[0])], axis=-1)
            else:
                h = h_ref[lo:lo + half, :]
            y = _dot(h, wo_ref[...])
            yield
            x = x + _rms(y, gmix_ref[...])
        xn = _rms(x, gpre_ref[...]).astype(BF16)
        yield
        gate = _dot(xn, wgu_ref[:, :f])
        up = _dot(xn, wgu_ref[:, f:])
        yield
        act = (gate * _sigmoid(gate) * up).astype(BF16)
        h2 = _dot(act, wd_ref[...])
        yield
        o_ref[lo:lo + half, :] = x + HALF_STEP * _rms(h2, gpost_ref[...])

    _lockstep(rows(lo) for lo in (0, half))


def _resident(shape):
    return pl.BlockSpec(shape, lambda *_: (0,) * len(shape), pipeline_mode=pl.Buffered(1))


def ffn_sublayer(x, g_pre, w_gu, w_down, g_post, mix=None, *, tm=512):
    n, d = x.shape
    tm = min(tm, n)
    ffn_args = (x, g_pre, w_gu, w_down, g_post)
    ffn_specs = [pl.BlockSpec((tm, d), lambda i: (i, 0)), _resident(g_pre.shape), _resident(w_gu.shape),
                 _resident(w_down.shape), _resident(g_post.shape)]
    mixer, mix_args, mix_specs = None, (), []
    if mix is not None:
        h, w_out, g_mix = mix
        if h.ndim == 3:
            mixer = "pairs"
            h_spec = pl.BlockSpec((h.shape[0], tm, h.shape[2]), lambda i: (0, i, 0))
        else:
            mixer = "rows"
            h_spec = pl.BlockSpec((tm, h.shape[1]), lambda i: (i, 0))
        mix_args = (h, w_out, g_mix)
        mix_specs = [h_spec, _resident(w_out.shape), _resident(g_mix.shape)]
    return pl.pallas_call(
        functools.partial(_ffn_kernel, mixer=mixer),
        grid=(n // tm,),
        in_specs=mix_specs + ffn_specs,
        out_specs=pl.BlockSpec((tm, d), lambda i: (i, 0)),
        out_shape=jax.ShapeDtypeStruct((n, d), F32),
        compiler_params=_cparams(("parallel",)),
        name="ffn" if mix is None else "mix_ffn",
    )(*mix_args, *ffn_args)


def _norm_matmul_kernel(x_ref, g_ref, w_ref, o_ref):
    xn = _rms(x_ref[...], g_ref[...]).astype(BF16)
    o_ref[...] = _dot(xn, w_ref[...]).astype(o_ref.dtype)


def norm_matmul(x, g, w, out_dtype, *, tm=512):
    n, d = x.shape
    nout = w.shape[1]
    tm = min(tm, n)
    return pl.pallas_call(
        _norm_matmul_kernel,
        grid=(n // tm,),
        in_specs=[
            pl.BlockSpec((tm, d), lambda i: (i, 0)),
            _resident(g.shape),
            _resident(w.shape),
        ],
        out_specs=pl.BlockSpec((tm, nout), lambda i: (i, 0)),
        out_shape=jax.ShapeDtypeStruct((n, nout), out_dtype),
        compiler_params=_cparams(("parallel",)),
        name="norm_matmul",
    )(x, g, w)


def _mlstm_proj_kernel(x_ref, g_ref, w_ref, wif_ref, bif_ref, proj_ref, gates_ref):
    H = ML_HEADS
    xn = _rms(x_ref[...], g_ref[...]).astype(BF16)
    proj_ref[...] = _dot(xn, w_ref[...]).astype(proj_ref.dtype)
    pre = _dot_nt(wif_ref[...], xn) + bif_ref[...]
    pre = ML_GATE_CAP * jnp.tanh(pre * (1.0 / ML_GATE_CAP))
    is_input_gate = lax.broadcasted_iota(jnp.int32, pre.shape, 0) < H
    gates_ref[...] = jnp.where(is_input_gate, pre, -_softplus(-pre))


def mlstm_proj(x, g, w_main, w_if_t, b_if, *, tm=512):
    n, d = x.shape
    tm = min(tm, n)
    nout = w_main.shape[1]
    ng = w_if_t.shape[0]
    return pl.pallas_call(
        _mlstm_proj_kernel,
        grid=(n // tm,),
        in_specs=[pl.BlockSpec((tm, d), lambda i: (i, 0)), _resident(g.shape), _resident(w_main.shape),
                  _resident(w_if_t.shape), _resident(b_if.shape)],
        out_specs=[pl.BlockSpec((tm, nout), lambda i: (i, 0)), pl.BlockSpec((ng, tm), lambda i: (0, i))],
        out_shape=[jax.ShapeDtypeStruct((n, nout), BF16), jax.ShapeDtypeStruct((ng, n), F32)],
        compiler_params=_cparams(("parallel",)),
        name="mlstm_proj",
    )(x, g, w_main, w_if_t, b_if)


def _mlstm_kernel(qk_ref, v_ref, o_ref, gates_ref, cw_ref, ng_ref, out_ref,
                  c_ref, n_ref, m_ref, tq_ref, tk_ref):
    L, H, dk, dv = ML_CHUNK, ML_HEADS, ML_DQK, ML_DV
    R = qk_ref.shape[0]
    nc = qk_ref.shape[1] // L
    KW = cw_ref.shape[0]
    TAIL = 8

    @pl.when(pl.program_id(1) == 0)
    def _():
        for ref in (c_ref, n_ref, m_ref, tq_ref, tk_ref):
            ref[...] = jnp.zeros_like(ref)

    row = lax.broadcasted_iota(jnp.int32, (L, L), 0)
    col = lax.broadcasted_iota(jnp.int32, (L, L), 1)
    tri = row >= col
    eye = row == col
    ones = jnp.ones((L, L), BF16)

    def conv_silu(x, tail, w):
        ext = jnp.concatenate([tail, x], axis=0)
        acc = x * w[KW - 1:KW, :]
        for i in range(KW - 1):
            off = TAIL - (KW - 1) + i
            acc = acc + ext[off:off + L, :] * w[i:i + 1, :]
        return acc * _sigmoid(acc)

    def step(r, hd, c, sl, m, tq, tk):
        ci = r * H + hd
        xq = qk_ref[r, sl, hd * dk:(hd + 1) * dk].astype(F32)
        xk = qk_ref[r, sl, (H + hd) * dk:(H + hd + 1) * dk].astype(F32)
        qc = (conv_silu(xq, tq, cw_ref[:, hd * dk:(hd + 1) * dk]) * (dk ** -0.5)).astype(BF16)
        kc = conv_silu(xk, tk, cw_ref[:, (H + hd) * dk:(H + hd + 1) * dk]).astype(BF16)
        vc = v_ref[r, sl, hd * dv:(hd + 1) * dv]
        irow = gates_ref[r, hd:hd + 1, sl]
        frow = gates_ref[r, H + hd:H + hd + 1, sl]
        fh, fm, fl = _split3(jnp.where(tri, frow, 0.0))
        bmat = _dot(fh, ones) + _dot(fm, ones) + _dot(fl, ones)
        brow = _dot_nt(ones, fh) + _dot_nt(ones, fm) + _dot_nt(ones, fl)
        qk = _dot_nt(qc, kc)
        cmat = c_ref[ci]
        nrow = n_ref[ci]
        yield
        log_d = jnp.where(tri, bmat - brow + irow, -jnp.inf)
        bcol = bmat[:, 0:1]
        log_inter = bcol + m
        m_t = jnp.maximum(log_inter, jnp.max(log_d, axis=-1, keepdims=True))
        s = qk * jnp.exp(log_d - m_t)
        inter = jnp.exp(log_inter - m_t)
        b_end = bmat[L - 1:L, 0:1]
        log_w = b_end - brow[0:1, :] + irow
        m_new = jnp.maximum(b_end + m, jnp.max(log_w, axis=-1, keepdims=True))
        w_row = jnp.exp(log_w - m_new)
        carry_decay = jnp.exp(b_end + m - m_new)
        wdiag = jnp.where(eye, w_row, 0.0).astype(BF16)
        wv = _dot(wdiag, vc).astype(BF16)
        n_add = _dot(jnp.broadcast_to(w_row, (8, L)).astype(BF16), kc)[0:1, :]
        num = _dot(s.astype(BF16), vc) + inter * _dot(qc, cmat.astype(BF16))
        yield
        c_new = carry_decay * cmat + _dot_tn(kc, wv)
        n_new = carry_decay * nrow + n_add
        qn = jnp.sum(qc.astype(F32) * nrow, axis=-1, keepdims=True)
        den = jnp.sum(s, axis=-1, keepdims=True) + inter * qn
        h = num / jnp.maximum(jnp.abs(den), jnp.exp(-m_t))
        ng = ng_ref[:, hd * dv:(hd + 1) * dv]
        hn = h * lax.rsqrt(jnp.mean(h * h, axis=-1, keepdims=True) + NORM_EPS) * ng
        og = o_ref[r, sl, hd * dv:(hd + 1) * dv].astype(F32)
        out = (hn * _sigmoid(og)).astype(out_ref.dtype)
        return (m_new, xq[L - TAIL:, :], xk[L - TAIL:, :]), c_new, n_new, out

    chains = [(r, hd) for r in range(R) for hd in range(H)]

    def body(c, carries):
        sl = pl.ds(pl.multiple_of(c * L, L), L)
        res = _lockstep(step(r, hd, c, sl, *carries[ci]) for ci, (r, hd) in enumerate(chains))
        for ci, (r, hd) in enumerate(chains):
            _, c_ref[ci], n_ref[ci], out_ref[r, sl, hd * dv:(hd + 1) * dv] = res[ci]
        return tuple(x[0] for x in res)

    init = tuple((m_ref[ci][0:1, 0:1], tq_ref[ci], tk_ref[ci]) for ci in range(len(chains)))
    final = lax.fori_loop(0, nc, body, init)
    for ci, (m, tq, tk) in enumerate(final):
        m_ref[ci] = jnp.broadcast_to(m, m_ref.shape[1:])
        tq_ref[ci] = tq
        tk_ref[ci] = tk


def mlstm_scan(proj, gates, conv_w, ng):
    b, t, _ = proj.shape
    h, dk, dv, L = ML_HEADS, ML_DQK, ML_DV, ML_CHUNK
    rows = min(ML_ROWS, b)
    tb = min(ML_TIME_BLOCK, t)
    wqk, wv = 2 * h * dk, h * dv
    nchains = rows * h
    seq = lambda w, col: pl.BlockSpec((rows, tb, w), lambda i, j: (i, j, col))
    return pl.pallas_call(
        _mlstm_kernel,
        grid=(b // rows, t // tb),
        in_specs=[seq(wqk, 0), seq(wv, wqk // wv), seq(wv, wqk // wv + 1),
                  pl.BlockSpec((rows, 2 * h, tb), lambda i, j: (i, 0, j)),
                  _resident(conv_w.shape), _resident(ng.shape)],
        out_specs=pl.BlockSpec((rows, tb, wv), lambda i, j: (i, j, 0)),
        out_shape=jax.ShapeDtypeStruct((b, t, wv), BF16),
        scratch_shapes=[pltpu.VMEM((nchains, dk, dv), F32), pltpu.VMEM((nchains, 1, dk), F32),
                        pltpu.VMEM((nchains, 8, LANES), F32), pltpu.VMEM((nchains, 8, dk), F32),
                        pltpu.VMEM((nchains, 8, dk), F32)],
        compiler_params=_cparams(("parallel", "arbitrary")),
        name="mlstm_scan",
    )(proj, proj, proj, gates, conv_w, ng)


def mlstm_mixer(x, g_pre, w_in, b_if, conv_w, norm_g, b, t):
    h, dk, dv, L = ML_HEADS, ML_DQK, ML_DV, ML_CHUNK
    n_main = 2 * h * dk + 2 * h * dv
    w_main = w_in[:, :n_main].astype(BF16)
    w_if_t = w_in[:, n_main:].T.astype(BF16)
    proj, gates = mlstm_proj(x, g_pre, w_main, w_if_t, b_if.reshape(2 * h, 1))
    gates = jnp.transpose(gates.reshape(2 * h, b, t), (1, 0, 2))
    hs = mlstm_scan(proj.reshape(b, t, n_main), gates, conv_w, norm_g[None, :])
    return hs.reshape(b * t, h * dv)


def _retention_kernel(lg_ref, q_ref, k_ref, v_ref, g_ref, cos_ref, sin_ref, out_ref, r_ref):
    L, H, dk, dv = RT_CHUNK, RT_HEADS, RT_DK, RT_DV
    half = dk // 2
    nc = q_ref.shape[1] // L

    @pl.when(pl.program_id(1) == 0)
    def _():
        r_ref[...] = jnp.zeros_like(r_ref)

    row = lax.broadcasted_iota(jnp.int32, (L, L), 0)
    col = lax.broadcasted_iota(jnp.int32, (L, L), 1)
    diff = jnp.maximum(row - col, 0).astype(F32)
    idx = lax.broadcasted_iota(jnp.int32, (L, 1), 0).astype(F32)

    def rotate(u, cos, sin):
        u1, u2 = u[:, :half], u[:, half:]
        return jnp.concatenate([u1 * cos - u2 * sin, u1 * sin + u2 * cos], axis=-1)

    def step(hd, sl, cos, sin):
        log_gamma = lg_ref[hd]
        qr = rotate(q_ref[0, sl, hd * dk:(hd + 1) * dk].astype(F32), cos, sin)
        kr = rotate(k_ref[0, sl, hd * dk:(hd + 1) * dk].astype(F32), cos, sin) * (dk ** -0.5)
        vc = v_ref[0, sl, hd * dv:(hd + 1) * dv]
        qb = qr.astype(BF16)
        qk = _dot_nt(qb, kr.astype(BF16))
        rmat = r_ref[hd]
        k_decay = jnp.exp(log_gamma * (L - 1.0 - idx))
        r_add = _dot_tn((kr * k_decay).astype(BF16), vc)
        cross = _dot(qb, rmat.astype(BF16))
        yield
        d_mask = jnp.where(row >= col, jnp.exp(log_gamma * diff), 0.0)
        q_decay = jnp.exp(log_gamma * (idx + 1.0))
        y = _dot((qk * d_mask).astype(BF16), vc) + cross * q_decay
        yield
        r_new = jnp.exp(jnp.full((1, 1), L, F32) * log_gamma) * rmat + r_add
        yn = y * lax.rsqrt(jnp.mean(y * y, axis=-1, keepdims=True) + NORM_EPS)
        gc = g_ref[0, sl, hd * dv:(hd + 1) * dv].astype(F32)
        return r_new, (gc * _sigmoid(gc) * yn).astype(out_ref.dtype)

    def body(c, carry):
        sl = pl.ds(pl.multiple_of(c * L, L), L)
        cos = cos_ref[0, sl, :]
        sin = sin_ref[0, sl, :]
        res = _lockstep(step(hd, sl, cos, sin) for hd in range(H))
        for hd in range(H):
            r_ref[hd], out_ref[0, sl, hd * dv:(hd + 1) * dv] = res[hd]
        return carry

    lax.fori_loop(0, nc, body, 0)


def retention_scan(proj, cos, sin, log_gamma):
    b, t, _ = proj.shape
    h, dk, dv = RT_HEADS, RT_DK, RT_DV
    tb = min(RT_TIME_BLOCK, t)
    wq, wv = h * dk, h * dv
    seq = lambda w, col: pl.BlockSpec((1, tb, w), lambda i, j: (i, j, col))
    return pl.pallas_call(
        _retention_kernel,
        grid=(b, t // tb),
        in_specs=[pl.BlockSpec(memory_space=pltpu.SMEM),
                  seq(wq, 0), seq(wq, 1), seq(wv, 2 * wq // wv), seq(wv, 2 * wq // wv + 1),
                  seq(dk // 2, 0), seq(dk // 2, 0)],
        out_specs=seq(wv, 0),
        out_shape=jax.ShapeDtypeStruct((b, t, wv), BF16),
        scratch_shapes=[pltpu.VMEM((h, dk, dv), F32)],
        compiler_params=_cparams(("parallel", "arbitrary")),
        name="retention_scan",
    )(log_gamma, proj, proj, proj, proj, cos, sin)


def retention_mixer(x, positions, g_pre, w_in, b, t):
    h, dk, dv = RT_HEADS, RT_DK, RT_DV
    proj = norm_matmul(x, g_pre, w_in.astype(BF16), BF16).reshape(b, t, -1)
    inv_freq = 1.0 / (RT_ROPE_BASE ** jnp.linspace(0.0, 1.0, dk // 2, dtype=F32))
    ang = positions.astype(F32)[:, :, None] * inv_freq
    log_gamma = jnp.log(1.0 - 2.0 ** (-5.0 - jnp.arange(h, dtype=F32)))
    y = retention_scan(proj, jnp.cos(ang), jnp.sin(ang), log_gamma)
    return y.reshape(b * t, h * dv)


def _rwkv_proj_kernel(x_ref, xp_ref, g_ref, mu_ref, wrkv_ref, w0_ref, w1_ref, w2_ref,
                      a0_ref, a1_ref, a2_ref, g1_ref, g2_ref,
                      r_ref, k_ref, v_ref, gc_ref, a_ref, gate_ref, *, seq_tiles):
    i = pl.program_id(0)
    tm = x_ref.shape[0]
    g = g_ref[...]
    h = _rms(x_ref[...], g)
    prev = _rms(xp_ref[...], g)[7:8, :]
    prev = prev * (i % seq_tiles != 0).astype(F32)
    rid = lax.broadcasted_iota(jnp.int32, (tm, 1), 0)
    h_prev = jnp.where(rid == 0, prev, pltpu.roll(h, 1, axis=0))
    xx = h_prev - h
    mix = lambda j: (h + xx * mu_ref[j:j + 1, :]).astype(BF16)

    def put(ref, val):
        for p in range(ref.shape[0]):
            ref[p] = val[:, p * LANES:(p + 1) * LANES].astype(ref.dtype)

    put(r_ref, _dot(mix(0), wrkv_ref[0]))
    put(k_ref, _dot(mix(1), wrkv_ref[1]))
    put(v_ref, _dot(mix(2), wrkv_ref[2]))
    wl = jnp.tanh(_dot(mix(3), w1_ref[...])).astype(BF16)
    w_log = -_softplus(-(w0_ref[...] + _dot(wl, w2_ref[...]))) - 0.5
    lw = -jnp.exp(w_log)
    span = min(tm, MXU_DEPTH)
    rr = lax.broadcasted_iota(jnp.int32, (span, span), 0)
    cc = lax.broadcasted_iota(jnp.int32, (span, span), 1)
    chunk_tril = ((rr // RW_CHUNK == cc // RW_CHUNK) & (rr >= cc)).astype(BF16)
    gcum = []
    for lo in range(0, tm, span):
        lh, lm, ll = _split3(lw[lo:lo + span, :])
        gcum.append(_dot(chunk_tril, lh) + _dot(chunk_tril, lm) + _dot(chunk_tril, ll))
    put(gc_ref, jnp.concatenate(gcum, axis=0))
    al = _dot(mix(4), a1_ref[...]).astype(BF16)
    put(a_ref, _sigmoid(a0_ref[...] + _dot(al, a2_ref[...])))
    gl = _sigmoid(_dot(mix(5), g1_ref[...])).astype(BF16)
    put(gate_ref, _dot(gl, g2_ref[...]))


def rwkv_proj(x, g_pre, mu, w_rkv, w0, w1, w2, a0, a1, a2, g1, g2, t, *, tm=512):
    n, d = x.shape
    tm = min(tm, t)
    p = d // LANES
    full = lambda a: pl.BlockSpec(a.shape, lambda i: (0,) * a.ndim)
    outs = pl.BlockSpec((p, tm, LANES), lambda i: (0, i, 0))
    args = (g_pre, mu, w_rkv, w0, w1, w2, a0, a1, a2, g1, g2)
    shp = lambda dt: jax.ShapeDtypeStruct((p, n, LANES), dt)
    return pl.pallas_call(
        functools.partial(_rwkv_proj_kernel, seq_tiles=t // tm),
        grid=(n // tm,),
        in_specs=[
            pl.BlockSpec((tm, d), lambda i: (i, 0)),
            pl.BlockSpec((8, d), lambda i: (jnp.maximum(i * (tm // 8) - 1, 0), 0)),
        ] + [full(a) for a in args],
        out_specs=[outs] * 6,
        out_shape=[shp(BF16), shp(BF16), shp(BF16), shp(F32), shp(BF16), shp(BF16)],
        compiler_params=_cparams(("parallel",)),
        name="rwkv_proj",
    )(x, x, *args)


def _rwkv_scan_kernel(r_ref, k_ref, v_ref, gc_ref, a_ref, gate_ref, kk_ref, ka_ref, rk_ref,
                      lng_ref, lnb_ref, out_ref, s_ref):
    C = RW_CHUNK
    N = RW_HEAD
    C2 = 2 * C
    nc = r_ref.shape[2] // C

    @pl.when(pl.program_id(1) == 0)
    def _():
        s_ref[...] = jnp.zeros_like(s_ref)

    lane = lax.broadcasted_iota(jnp.int32, (1, LANES), 1)
    in_a = lane < N
    m_a = in_a.astype(F32)
    m_b = 1.0 - m_a
    rr = lax.broadcasted_iota(jnp.int32, (C2, C2), 0)
    cc = lax.broadcasted_iota(jnp.int32, (C2, C2), 1)
    same = (rr // C) == (cc // C)
    strict = same & ((rr % C) > (cc % C))
    incl = same & ((rr % C) >= (cc % C))
    eye2 = (rr == cc).astype(F32)
    level_masks = []
    for lvl in range(int(math.log2(C))):
        rb, cb = (rr % C) >> lvl, (cc % C) >> lvl
        level_masks.append(same & ((rb & 1) == 1) & (cb == rb - 1))
    incl2 = jnp.concatenate([incl, incl], axis=1)
    first_row = lax.broadcasted_iota(jnp.int32, (C, 1), 0) == 0

    def seg_sum(z):
        sa = jnp.sum(z * m_a, axis=-1, keepdims=True)
        sb = jnp.sum(z * m_b, axis=-1, keepdims=True)
        return jnp.where(in_a, sa, sb)

    def stack(z):
        return jnp.concatenate([z * m_a, z * m_b], axis=0)

    def load(row, gi, sl):
        f32 = lambda ref: ref[gi, row, sl, :].astype(F32)
        return (f32(r_ref), f32(k_ref), f32(v_ref), gc_ref[gi, row, sl, :], f32(a_ref), f32(gate_ref),
                s_ref[row, gi], kk_ref[gi], ka_ref[gi], rk_ref[gi], lng_ref[gi], lnb_ref[gi])

    def step(r, k, v, gcum, a_sig, gate, st0, kk_w, ka_w, rk_w, ln_g, ln_b):
        kk = k * kk_w
        kk = kk / jnp.maximum(jnp.sqrt(seg_sum(kk * kk)), 1e-12)
        k = k * (1.0 + (a_sig - 1.0) * ka_w)
        a_ = -kk
        b_ = kk * a_sig
        g_end = gcum[C - 1:C, :]
        g_excl = jnp.where(first_row, 0.0, pltpu.roll(gcum, 1, axis=0))
        e_neg = jnp.exp(-gcum)
        e_rest = jnp.exp(g_end - gcum)
        a_h = stack(a_ * jnp.exp(g_excl))
        r_h = stack(r * jnp.exp(gcum))
        b_t = stack(b_ * e_rest)
        k_t = stack(k * e_rest)
        twice = lambda z: jnp.concatenate([z, z], axis=0)
        v_s = twice(v).astype(BF16)
        lhs = jnp.concatenate([a_h, r_h], axis=0).astype(BF16)
        rhs = jnp.concatenate([twice(b_ * e_neg), twice(k * e_neg)], axis=0).astype(BF16)
        sc = _dot_nt(lhs, rhs)
        yield
        a_ab = jnp.where(strict, sc[:C2, :C2], 0.0)
        a_ak = jnp.where(strict, sc[:C2, C2:], 0.0)
        a_r = jnp.where(incl2, sc[C2:, :], 0.0).astype(BF16)
        w1 = _dot(a_ak.astype(BF16), v_s)
        tinv = eye2 + jnp.where(level_masks[0], a_ab, 0.0)
        for mk in level_masks[1:]:
            tb = tinv.astype(BF16)
            l21 = jnp.where(mk, a_ab, 0.0).astype(BF16)
            tl = _dot(tb, l21)
            yield
            tinv = tinv + _dot(tl.astype(BF16), tb)
            yield
        au = _dot(tinv.astype(BF16), jnp.concatenate([a_h, w1], axis=1).astype(BF16))
        yield
        amat = jnp.concatenate([au.astype(BF16),
                                jnp.concatenate([jnp.zeros((C2, LANES), BF16), v_s], axis=1)], axis=0)
        pz = _dot_tn(jnp.concatenate([b_t, k_t], axis=0).astype(BF16), amat)
        ry = _dot(a_r, amat)
        yield
        r_bar = r_h + ry[:, :LANES]
        sy = _dot(jnp.concatenate([r_bar, pz[:, :LANES]], axis=0).astype(BF16), st0.astype(BF16))
        yield
        st_new = st0 * jnp.broadcast_to(jnp.exp(g_end), (LANES, LANES)).T + sy[C2:, :] + pz[:, LANES:]
        ys = sy[:C2, :] + ry[:, LANES:]
        y = jnp.where(in_a, ys[:C, :], ys[C:, :])
        mean = seg_sum(y) * (1.0 / N)
        yc = y - mean
        var = seg_sum(yc * yc) * (1.0 / N)
        yn = yc * lax.rsqrt(var + RW_LN_EPS) * ln_g + ln_b
        bonus = seg_sum(r * k * rk_w) * v
        return st_new, ((yn + bonus) * gate).astype(out_ref.dtype)

    def body(c, carry):
        sl = pl.ds(pl.multiple_of(c * C, C), C)
        chains = [(row, gi) for row in range(r_ref.shape[1]) for gi in range(r_ref.shape[0])]
        results = _lockstep(step(*load(row, gi, sl)) for row, gi in chains)
        for (row, gi), res in zip(chains, results):
            s_ref[row, gi], out_ref[gi, row, sl, :] = res
        return carry

    lax.fori_loop(0, nc, body, 0)


def rwkv_scan(r, k, v, gcum, a, gate, k_k, k_a, r_k, ln_g, ln_b, b, t):
    p, n, l = r.shape
    tb = min(RW_TIME_BLOCK, t)
    rows = min(RW_ROWS, b)
    seq = lambda: pl.BlockSpec((p, rows, tb, l), lambda i, j: (0, i, j, 0))
    par = lambda: pl.BlockSpec((p, 1, l), lambda i, j: (0, 0, 0))
    view = lambda a: a.reshape(p, b, t, l)
    y = pl.pallas_call(
        _rwkv_scan_kernel,
        grid=(b // rows, t // tb),
        in_specs=[seq()] * 6 + [par()] * 5,
        out_specs=seq(),
        out_shape=jax.ShapeDtypeStruct((p, b, t, l), BF16),
        scratch_shapes=[pltpu.VMEM((rows, p, l, l), F32)],
        compiler_params=_cparams(("parallel", "arbitrary")),
        name="rwkv_scan",
    )(view(r), view(k), view(v), view(gcum), view(a), view(gate), k_k, k_a, r_k, ln_g, ln_b)
    return y.reshape(p, n, l)


def rwkv_mixer(x, g_pre, mu, w_rkv, w0, w1, w2, a0, a1, a2, g1, g2,
               k_k, k_a, r_k, ln_g, ln_b, b, t):
    bf = lambda a: a.astype(BF16)
    row = lambda a: a.reshape(1, -1)
    r, k, v, gcum, a, gate = rwkv_proj(x, g_pre, mu, bf(w_rkv), row(w0), bf(w1), bf(w2),
                                     row(a0), bf(a1), bf(a2), bf(g1), bf(g2), t)
    par = lambda a: a.reshape(RW_PAIRS, 1, LANES)
    return rwkv_scan(r, k, v, gcum, a, gate, par(k_k), par(k_a), par(r_k), par(ln_g), par(ln_b), b, t)


def kernel(x, positions, norm_g, ffn_w_gu, ffn_w_down, ml_w_in, ml_b_if, ml_conv_w, ml_norm_g, ml_w_out,
           rw_mu, rw_w_rkv, rw_w0, rw_w1, rw_w2, rw_a0, rw_a1, rw_a2, rw_g1, rw_g2,
           rw_k_k, rw_k_a, rw_r_k, rw_ln_g, rw_ln_b, rw_w_out, rt_w_in, rt_w_out):
    b, t, d = x.shape
    depth = norm_g.shape[0]
    xf = x.reshape(b * t, d)
    for layer in range(depth):
        g = norm_g[layer][:, None, :]
        xf = ffn_sublayer(xf, g[0], ffn_w_gu[layer, 0].astype(BF16), ffn_w_down[layer, 0].astype(BF16), g[1])
        kind, j = layer % 3, layer // 3
        if kind == 0:
            h = mlstm_mixer(xf, g[2], ml_w_in[j], ml_b_if[j], ml_conv_w[j], ml_norm_g[j], b, t)
            w_out = ml_w_out[j]
        elif kind == 1:
            h = rwkv_mixer(xf, g[2], rw_mu[j], rw_w_rkv[j], rw_w0[j], rw_w1[j], rw_w2[j],
                           rw_a0[j], rw_a1[j], rw_a2[j], rw_g1[j], rw_g2[j],
                           rw_k_k[j], rw_k_a[j], rw_r_k[j].reshape(-1), rw_ln_g[j], rw_ln_b[j], b, t)
            w_out = rw_w_out[j]
        else:
            h = retention_mixer(xf, positions, g[2], rt_w_in[j], b, t)
            w_out = rt_w_out[j]
        xf = ffn_sublayer(xf, g[4], ffn_w_gu[layer, 1].astype(BF16), ffn_w_down[layer, 1].astype(BF16), g[5],
                          mix=(h, w_out.astype(BF16), g[3]))
    return xf.reshape(b, t, d)
```

```python
import functools
import math

import jax
import jax.numpy as jnp
from jax import lax
from jax.experimental import pallas as pl
from jax.experimental.pallas import tpu as pltpu

F32 = jnp.float32
BF16 = jnp.bfloat16

D_MODEL = 1024
D_FF = 2816
NORM_EPS = 1e-6
HALF_STEP = 0.5

ML_HEADS = 4
ML_DQK = 128
ML_DV = 256
ML_CHUNK = 256
ML_GATE_CAP = 15.0
ML_ROWS = 2
ML_TIME_BLOCK = 1024

RW_HEAD = 64
RW_PAIRS = D_MODEL // (2 * RW_HEAD)
RW_CHUNK = 64
RW_TIME_BLOCK = 256
RW_ROWS = 2
RW_LN_EPS = 64e-5

RT_HEADS = 4
RT_DK = 256
RT_DV = 512
RT_CHUNK = 256
RT_TIME_BLOCK = 512
RT_ROPE_BASE = 10000.0

LANES = 128
MXU_DEPTH = 256
VMEM_LIMIT = 56 * 1024 * 1024


def _cparams(sem):
    return pltpu.CompilerParams(dimension_semantics=sem, vmem_limit_bytes=VMEM_LIMIT)


def _lockstep(gens):
    gens = list(gens)
    results = [None] * len(gens)
    live = list(range(len(gens)))
    while live:
        for i in list(live):
            try:
                next(gens[i])
            except StopIteration as done:
                results[i] = done.value
                live.remove(i)
    return results


def _rms(x, g):
    return x * lax.rsqrt(jnp.mean(x * x, axis=-1, keepdims=True) + NORM_EPS) * g


def _dot(a, b):
    return jnp.dot(a, b, preferred_element_type=F32)


def _dot_nt(a, b):
    return lax.dot_general(a, b, (((1,), (1,)), ((), ())), preferred_element_type=F32)


def _dot_tn(a, b):
    return lax.dot_general(a, b, (((0,), (0,)), ((), ())), preferred_element_type=F32)


def _split3(x):
    hi = x.astype(BF16)
    r1 = x - hi.astype(F32)
    mid = r1.astype(BF16)
    lo = (r1 - mid.astype(F32)).astype(BF16)
    return hi, mid, lo


def _sigmoid(x):
    return 1.0 / (1.0 + jnp.exp(-x))


def _softplus(x):
    return jnp.maximum(x, 0.0) + jnp.log(1.0 + jnp.exp(-jnp.abs(x)))


def _ffn_kernel(*refs, mixer):
    if mixer is None:
        x_ref, gpre_ref, wgu_ref, wd_ref, gpost_ref, o_ref = refs
    else:
        h_ref, wo_ref, gmix_ref, x_ref, gpre_ref, wgu_ref, wd_ref, gpost_ref, o_ref = refs
    f = wd_ref.shape[0]
    half = x_ref.shape[0] // 2

    def rows(lo):
        x = x_ref[lo:lo + half, :]
        if mixer is not None:
            if mixer == "pairs":
                h = jnp.concatenate([h_ref[p, lo:lo + half, :] for p in range(h_ref.shape[0])], axis=-1)
            else:
                h = h_ref[lo:lo + half, :]
            y = _dot(h, wo_ref[...])
            yield
            x = x + _rms(y, gmix_ref[...])
        xn = _rms(x, gpre_ref[...]).astype(BF16)
        yield
        gate = _dot(xn, wgu_ref[:, :f])
        up = _dot(xn, wgu_ref[:, f:])
        yield
        act = (gate * _sigmoid(gate) * up).astype(BF16)
        h2 = _dot(act, wd_ref[...])
        yield
        o_ref[lo:lo + half, :] = x + HALF_STEP * _rms(h2, gpost_ref[...])

    _lockstep(rows(lo) for lo in (0, half))


def _resident(shape):
    return pl.BlockSpec(shape, lambda *_: (0,) * len(shape), pipeline_mode=pl.Buffered(1))


def ffn_sublayer(x, g_pre, w_gu, w_down, g_post, mix=None, *, tm=512):
    n, d = x.shape
    tm = min(tm, n)
    ffn_args = (x, g_pre, w_gu, w_down, g_post)
    ffn_specs = [pl.BlockSpec((tm, d), lambda i: (i, 0)), _resident(g_pre.shape), _resident(w_gu.shape),
                 _resident(w_down.shape), _resident(g_post.shape)]
    mixer, mix_args, mix_specs = None, (), []
    if mix is not None:
        h, w_out, g_mix = mix
        if h.ndim == 3:
            mixer = "pairs"
            h_spec = pl.BlockSpec((h.shape[0], tm, h.shape[2]), lambda i: (0, i, 0))
        else:
            mixer = "rows"
            h_spec = pl.BlockSpec((tm, h.shape[1]), lambda i: (i, 0))
        mix_args = (h, w_out, g_mix)
        mix_specs = [h_spec, _resident(w_out.shape), _resident(g_mix.shape)]
    return pl.pallas_call(
        functools.partial(_ffn_kernel, mixer=mixer),
        grid=(n // tm,),
        in_specs=mix_specs + ffn_specs,
        out_specs=pl.BlockSpec((tm, d), lambda i: (i, 0)),
        out_shape=jax.ShapeDtypeStruct((n, d), F32),
        compiler_params=_cparams(("parallel",)),
        name="ffn" if mix is None else "mix_ffn",
    )(*mix_args, *ffn_args)


def _norm_matmul_kernel(x_ref, g_ref, w_ref, o_ref):
    xn = _rms(x_ref[...], g_ref[...]).astype(BF16)
    o_ref[...] = _dot(xn, w_ref[...]).astype(o_ref.dtype)


def norm_matmul(x, g, w, out_dtype, *, tm=512):
    n, d = x.shape
    nout = w.shape[1]
    tm = min(tm, n)
    return pl.pallas_call(
        _norm_matmul_kernel,
        grid=(n // tm,),
        in_specs=[
            pl.BlockSpec((tm, d), lambda i: (i, 0)),
            _resident(g.shape),
            _resident(w.shape),
        ],
        out_specs=pl.BlockSpec((tm, nout), lambda i: (i, 0)),
        out_shape=jax.ShapeDtypeStruct((n, nout), out_dtype),
        compiler_params=_cparams(("parallel",)),
        name="norm_matmul",
    )(x, g, w)


def _mlstm_proj_kernel(x_ref, g_ref, w_ref, wif_ref, bif_ref, proj_ref, gates_ref):
    H = ML_HEADS
    xn = _rms(x_ref[...], g_ref[...]).astype(BF16)
    proj_ref[...] = _dot(xn, w_ref[...]).astype(proj_ref.dtype)
    pre = _dot_nt(wif_ref[...], xn) + bif_ref[...]
    pre = ML_GATE_CAP * jnp.tanh(pre * (1.0 / ML_GATE_CAP))
    is_input_gate = lax.broadcasted_iota(jnp.int32, pre.shape, 0) < H
    gates_ref[...] = jnp.where(is_input_gate, pre, -_softplus(-pre))


def mlstm_proj(x, g, w_main, w_if_t, b_if, *, tm=512):
    n, d = x.shape
    tm = min(tm, n)
    nout = w_main.shape[1]
    ng = w_if_t.shape[0]
    return pl.pallas_call(
        _mlstm_proj_kernel,
        grid=(n // tm,),
        in_specs=[pl.BlockSpec((tm, d), lambda i: (i, 0)), _resident(g.shape), _resident(w_main.shape),
                  _resident(w_if_t.shape), _resident(b_if.shape)],
        out_specs=[pl.BlockSpec((tm, nout), lambda i: (i, 0)), pl.BlockSpec((ng, tm), lambda i: (0, i))],
        out_shape=[jax.ShapeDtypeStruct((n, nout), BF16), jax.ShapeDtypeStruct((ng, n), F32)],
        compiler_params=_cparams(("parallel",)),
        name="mlstm_proj",
    )(x, g, w_main, w_if_t, b_if)


def _mlstm_kernel(qk_ref, v_ref, o_ref, gates_ref, cw_ref, ng_ref, out_ref,
                  c_ref, n_ref, m_ref, tq_ref, tk_ref):
    L, H, dk, dv = ML_CHUNK, ML_HEADS, ML_DQK, ML_DV
    R = qk_ref.shape[0]
    nc = qk_ref.shape[1] // L
    KW = cw_ref.shape[0]
    TAIL = 8

    @pl.when(pl.program_id(1) == 0)
    def _():
        for ref in (c_ref, n_ref, m_ref, tq_ref, tk_ref):
            ref[...] = jnp.zeros_like(ref)

    row = lax.broadcasted_iota(jnp.int32, (L, L), 0)
    col = lax.broadcasted_iota(jnp.int32, (L, L), 1)
    tri = row >= col
    eye = row == col
    ones = jnp.ones((L, L), BF16)

    def conv_silu(x, tail, w):
        ext = jnp.concatenate([tail, x], axis=0)
        acc = x * w[KW - 1:KW, :]
        for i in range(KW - 1):
            off = TAIL - (KW - 1) + i
            acc = acc + ext[off:off + L, :] * w[i:i + 1, :]
        return acc * _sigmoid(acc)

    def step(r, hd, c, sl, m, tq, tk):
        ci = r * H + hd
        xq = qk_ref[r, sl, hd * dk:(hd + 1) * dk].astype(F32)
        xk = qk_ref[r, sl, (H + hd) * dk:(H + hd + 1) * dk].astype(F32)
        qc = (conv_silu(xq, tq, cw_ref[:, hd * dk:(hd + 1) * dk]) * (dk ** -0.5)).astype(BF16)
        kc = conv_silu(xk, tk, cw_ref[:, (H + hd) * dk:(H + hd + 1) * dk]).astype(BF16)
        vc = v_ref[r, sl, hd * dv:(hd + 1) * dv]
        irow = gates_ref[r, hd:hd + 1, sl]
        frow = gates_ref[r, H + hd:H + hd + 1, sl]
        fh, fm, fl = _split3(jnp.where(tri, frow, 0.0))
        bmat = _dot(fh, ones) + _dot(fm, ones) + _dot(fl, ones)
        brow = _dot_nt(ones, fh) + _dot_nt(ones, fm) + _dot_nt(ones, fl)
        qk = _dot_nt(qc, kc)
        cmat = c_ref[ci]
        nrow = n_ref[ci]
        yield
        log_d = jnp.where(tri, bmat - brow + irow, -jnp.inf)
        bcol = bmat[:, 0:1]
        log_inter = bcol + m
        m_t = jnp.maximum(log_inter, jnp.max(log_d, axis=-1, keepdims=True))
        s = qk * jnp.exp(log_d - m_t)
        inter = jnp.exp(log_inter - m_t)
        b_end = bmat[L - 1:L, 0:1]
        log_w = b_end - brow[0:1, :] + irow
        m_new = jnp.maximum(b_end + m, jnp.max(log_w, axis=-1, keepdims=True))
        w_row = jnp.exp(log_w - m_new)
        carry_decay = jnp.exp(b_end + m - m_new)
        wdiag = jnp.where(eye, w_row, 0.0).astype(BF16)
        wv = _dot(wdiag, vc).astype(BF16)
        n_add = _dot(jnp.broadcast_to(w_row, (8, L)).astype(BF16), kc)[0:1, :]
        num = _dot(s.astype(BF16), vc) + inter * _dot(qc, cmat.astype(BF16))
        yield
        c_new = carry_decay * cmat + _dot_tn(kc, wv)
        n_new = carry_decay * nrow + n_add
        qn = jnp.sum(qc.astype(F32) * nrow, axis=-1, keepdims=True)
        den = jnp.sum(s, axis=-1, keepdims=True) + inter * qn
        h = num / jnp.maximum(jnp.abs(den), jnp.exp(-m_t))
        ng = ng_ref[:, hd * dv:(hd + 1) * dv]
        hn = h * lax.rsqrt(jnp.mean(h * h, axis=-1, keepdims=True) + NORM_EPS) * ng
        og = o_ref[r, sl, hd * dv:(hd + 1) * dv].astype(F32)
        out = (hn * _sigmoid(og)).astype(out_ref.dtype)
        return (m_new, xq[L - TAIL:, :], xk[L - TAIL:, :]), c_new, n_new, out

    chains = [(r, hd) for r in range(R) for hd in range(H)]

    def body(c, carries):
        sl = pl.ds(pl.multiple_of(c * L, L), L)
        res = _lockstep(step(r, hd, c, sl, *carries[ci]) for ci, (r, hd) in enumerate(chains))
        for ci, (r, hd) in enumerate(chains):
            _, c_ref[ci], n_ref[ci], out_ref[r, sl, hd * dv:(hd + 1) * dv] = res[ci]
        return tuple(x[0] for x in res)

    init = tuple((m_ref[ci][0:1, 0:1], tq_ref[ci], tk_ref[ci]) for ci in range(len(chains)))
    final = lax.fori_loop(0, nc, body, init)
    for ci, (m, tq, tk) in enumerate(final):
        m_ref[ci] = jnp.broadcast_to(m, m_ref.shape[1:])
        tq_ref[ci] = tq
        tk_ref[ci] = tk


def mlstm_scan(proj, gates, conv_w, ng):
    b, t, _ = proj.shape
    h, dk, dv, L = ML_HEADS, ML_DQK, ML_DV, ML_CHUNK
    rows = min(ML_ROWS, b)
    tb = min(ML_TIME_BLOCK, t)
    wqk, wv = 2 * h * dk, h * dv
    nchains = rows * h
    seq = lambda w, col: pl.BlockSpec((rows, tb, w), lambda i, j: (i, j, col))
    return pl.pallas_call(
        _mlstm_kernel,
        grid=(b // rows, t // tb),
        in_specs=[seq(wqk, 0), seq(wv, wqk // wv), seq(wv, wqk // wv + 1),
                  pl.BlockSpec((rows, 2 * h, tb), lambda i, j: (i, 0, j)),
                  _resident(conv_w.shape), _resident(ng.shape)],
        out_specs=pl.BlockSpec((rows, tb, wv), lambda i, j: (i, j, 0)),
        out_shape=jax.ShapeDtypeStruct((b, t, wv), BF16),
        scratch_shapes=[pltpu.VMEM((nchains, dk, dv), F32), pltpu.VMEM((nchains, 1, dk), F32),
                        pltpu.VMEM((nchains, 8, LANES), F32), pltpu.VMEM((nchains, 8, dk), F32),
                        pltpu.VMEM((nchains, 8, dk), F32)],
        compiler_params=_cparams(("parallel", "arbitrary")),
        name="mlstm_scan",
    )(proj, proj, proj, gates, conv_w, ng)


def mlstm_mixer(x, g_pre, w_in, b_if, conv_w, norm_g, b, t):
    h, dk, dv, L = ML_HEADS, ML_DQK, ML_DV, ML_CHUNK
    n_main = 2 * h * dk + 2 * h * dv
    w_main = w_in[:, :n_main].astype(BF16)
    w_if_t = w_in[:, n_main:].T.astype(BF16)
    proj, gates = mlstm_proj(x, g_pre, w_main, w_if_t, b_if.reshape(2 * h, 1))
    gates = jnp.transpose(gates.reshape(2 * h, b, t), (1, 0, 2))
    hs = mlstm_scan(proj.reshape(b, t, n_main), gates, conv_w, norm_g[None, :])
    return hs.reshape(b * t, h * dv)


def _retention_kernel(lg_ref, q_ref, k_ref, v_ref, g_ref, cos_ref, sin_ref, out_ref, r_ref):
    L, H, dk, dv = RT_CHUNK, RT_HEADS, RT_DK, RT_DV
    half = dk // 2
    nc = q_ref.shape[1] // L

    @pl.when(pl.program_id(1) == 0)
    def _():
        r_ref[...] = jnp.zeros_like(r_ref)

    row = lax.broadcasted_iota(jnp.int32, (L, L), 0)
    col = lax.broadcasted_iota(jnp.int32, (L, L), 1)
    diff = jnp.maximum(row - col, 0).astype(F32)
    idx = lax.broadcasted_iota(jnp.int32, (L, 1), 0).astype(F32)

    def rotate(u, cos, sin):
        u1, u2 = u[:, :half], u[:, half:]
        return jnp.concatenate([u1 * cos - u2 * sin, u1 * sin + u2 * cos], axis=-1)

    def step(hd, sl, cos, sin):
        log_gamma = lg_ref[hd]
        qr = rotate(q_ref[0, sl, hd * dk:(hd + 1) * dk].astype(F32), cos, sin)
        kr = rotate(k_ref[0, sl, hd * dk:(hd + 1) * dk].astype(F32), cos, sin) * (dk ** -0.5)
        vc = v_ref[0, sl, hd * dv:(hd + 1) * dv]
        qb = qr.astype(BF16)
        qk = _dot_nt(qb, kr.astype(BF16))
        rmat = r_ref[hd]
        k_decay = jnp.exp(log_gamma * (L - 1.0 - idx))
        r_add = _dot_tn((kr * k_decay).astype(BF16), vc)
        cross = _dot(qb, rmat.astype(BF16))
        yield
        d_mask = jnp.where(row >= col, jnp.exp(log_gamma * diff), 0.0)
        q_decay = jnp.exp(log_gamma * (idx + 1.0))
        y = _dot((qk * d_mask).astype(BF16), vc) + cross * q_decay
        yield
        r_new = jnp.exp(jnp.full((1, 1), L, F32) * log_gamma) * rmat + r_add
        yn = y * lax.rsqrt(jnp.mean(y * y, axis=-1, keepdims=True) + NORM_EPS)
        gc = g_ref[0, sl, hd * dv:(hd + 1) * dv].astype(F32)
        return r_new, (gc * _sigmoid(gc) * yn).astype(out_ref.dtype)

    def body(c, carry):
        sl = pl.ds(pl.multiple_of(c * L, L), L)
        cos = cos_ref[0, sl, :]
        sin = sin_ref[0, sl, :]
        res = _lockstep(step(hd, sl, cos, sin) for hd in range(H))
        for hd in range(H):
            r_ref[hd], out_ref[0, sl, hd * dv:(hd + 1) * dv] = res[hd]
        return carry

    lax.fori_loop(0, nc, body, 0)


def retention_scan(proj, cos, sin, log_gamma):
    b, t, _ = proj.shape
    h, dk, dv = RT_HEADS, RT_DK, RT_DV
    tb = min(RT_TIME_BLOCK, t)
    wq, wv = h * dk, h * dv
    seq = lambda w, col: pl.BlockSpec((1, tb, w), lambda i, j: (i, j, col))
    return pl.pallas_call(
        _retention_kernel,
        grid=(b, t // tb),
        in_specs=[pl.BlockSpec(memory_space=pltpu.SMEM),
                  seq(wq, 0), seq(wq, 1), seq(wv, 2 * wq // wv), seq(wv, 2 * wq // wv + 1),
                  seq(dk // 2, 0), seq(dk // 2, 0)],
        out_specs=seq(wv, 0),
        out_shape=jax.ShapeDtypeStruct((b, t, wv), BF16),
        scratch_shapes=[pltpu.VMEM((h, dk, dv), F32)],
        compiler_params=_cparams(("parallel", "arbitrary")),
        name="retention_scan",
    )(log_gamma, proj, proj, proj, proj, cos, sin)


def retention_mixer(x, positions, g_pre, w_in, b, t):
    h, dk, dv = RT_HEADS, RT_DK, RT_DV
    proj = norm_matmul(x, g_pre, w_in.astype(BF16), BF16).reshape(b, t, -1)
    inv_freq = 1.0 / (RT_ROPE_BASE ** jnp.linspace(0.0, 1.0, dk // 2, dtype=F32))
    ang = positions.astype(F32)[:, :, None] * inv_freq
    log_gamma = jnp.log(1.0 - 2.0 ** (-5.0 - jnp.arange(h, dtype=F32)))
    y = retention_scan(proj, jnp.cos(ang), jnp.sin(ang), log_gamma)
    return y.reshape(b * t, h * dv)


def _rwkv_proj_kernel(x_ref, xp_ref, g_ref, mu_ref, wrkv_ref, w0_ref, w1_ref, w2_ref,
                      a0_ref, a1_ref, a2_ref, g1_ref, g2_ref,
                      r_ref, k_ref, v_ref, gc_ref, a_ref, gate_ref, *, seq_tiles):
    i = pl.program_id(0)
    tm = x_ref.shape[0]
    g = g_ref[...]
    h = _rms(x_ref[...], g)
    prev = _rms(xp_ref[...], g)[7:8, :]
    prev = prev * (i % seq_tiles != 0).astype(F32)
    rid = lax.broadcasted_iota(jnp.int32, (tm, 1), 0)
    h_prev = jnp.where(rid == 0, prev, pltpu.roll(h, 1, axis=0))
    span = min(tm, MXU_DEPTH)
    rr = lax.broadcasted_iota(jnp.int32, (span, span), 0)
    cc = lax.broadcasted_iota(jnp.int32, (span, span), 1)
    chunk_tril = ((rr // RW_CHUNK == cc // RW_CHUNK) & (rr >= cc)).astype(BF16)

    def rows(lo):
        hh = h[lo:lo + span, :]
        xx = h_prev[lo:lo + span, :] - hh
        mix = lambda j: (hh + xx * mu_ref[j:j + 1, :]).astype(BF16)

        def put(ref, val):
            for p in range(ref.shape[0]):
                ref[p, lo:lo + span, :] = val[:, p * LANES:(p + 1) * LANES].astype(ref.dtype)

        put(r_ref, _dot(mix(0), wrkv_ref[0]))
        yield
        put(k_ref, _dot(mix(1), wrkv_ref[1]))
        yield
        put(v_ref, _dot(mix(2), wrkv_ref[2]))
        yield
        wl = jnp.tanh(_dot(mix(3), w1_ref[...])).astype(BF16)
        al = _dot(mix(4), a1_ref[...]).astype(BF16)
        gl = _sigmoid(_dot(mix(5), g1_ref[...])).astype(BF16)
        yield
        w_log = -_softplus(-(w0_ref[...] + _dot(wl, w2_ref[...]))) - 0.5
        put(a_ref, _sigmoid(a0_ref[...] + _dot(al, a2_ref[...])))
        put(gate_ref, _dot(gl, g2_ref[...]))
        yield
        lh, lm, ll = _split3(-jnp.exp(w_log))
        put(gc_ref, _dot(chunk_tril, lh) + _dot(chunk_tril, lm) + _dot(chunk_tril, ll))

    _lockstep(rows(lo) for lo in range(0, tm, span))


def rwkv_proj(x, g_pre, mu, w_rkv, w0, w1, w2, a0, a1, a2, g1, g2, t, *, tm=512):
    n, d = x.shape
    tm = min(tm, t)
    p = d // LANES
    full = lambda a: pl.BlockSpec(a.shape, lambda i: (0,) * a.ndim)
    outs = pl.BlockSpec((p, tm, LANES), lambda i: (0, i, 0))
    args = (g_pre, mu, w_rkv, w0, w1, w2, a0, a1, a2, g1, g2)
    shp = lambda dt: jax.ShapeDtypeStruct((p, n, LANES), dt)
    return pl.pallas_call(
        functools.partial(_rwkv_proj_kernel, seq_tiles=t // tm),
        grid=(n // tm,),
        in_specs=[
            pl.BlockSpec((tm, d), lambda i: (i, 0)),
            pl.BlockSpec((8, d), lambda i: (jnp.maximum(i * (tm // 8) - 1, 0), 0)),
        ] + [full(a) for a in args],
        out_specs=[outs] * 6,
        out_shape=[shp(BF16), shp(BF16), shp(BF16), shp(F32), shp(BF16), shp(BF16)],
        compiler_params=_cparams(("parallel",)),
        name="rwkv_proj",
    )(x, x, *args)


def _rwkv_scan_kernel(r_ref, k_ref, v_ref, gc_ref, a_ref, gate_ref, kk_ref, ka_ref, rk_ref,
                      lng_ref, lnb_ref, out_ref, s_ref):
    C = RW_CHUNK
    N = RW_HEAD
    C2 = 2 * C
    nc = r_ref.shape[2] // C

    @pl.when(pl.program_id(1) == 0)
    def _():
        s_ref[...] = jnp.zeros_like(s_ref)

    lane = lax.broadcasted_iota(jnp.int32, (1, LANES), 1)
    in_a = lane < N
    m_a = in_a.astype(F32)
    m_b = 1.0 - m_a
    rr = lax.broadcasted_iota(jnp.int32, (C2, C2), 0)
    cc = lax.broadcasted_iota(jnp.int32, (C2, C2), 1)
    same = (rr // C) == (cc // C)
    strict = same & ((rr % C) > (cc % C))
    incl = same & ((rr % C) >= (cc % C))
    eye2 = (rr == cc).astype(F32)
    level_masks = []
    for lvl in range(int(math.log2(C))):
        rb, cb = (rr % C) >> lvl, (cc % C) >> lvl
        level_masks.append(same & ((rb & 1) == 1) & (cb == rb - 1))
    incl2 = jnp.concatenate([incl, incl], axis=1)
    first_row = lax.broadcasted_iota(jnp.int32, (C, 1), 0) == 0

    def seg_sum(z):
        sa = jnp.sum(z * m_a, axis=-1, keepdims=True)
        sb = jnp.sum(z * m_b, axis=-1, keepdims=True)
        return jnp.where(in_a, sa, sb)

    def stack(z):
        return jnp.concatenate([z * m_a, z * m_b], axis=0)

    def load(row, gi, sl):
        f32 = lambda ref: ref[gi, row, sl, :].astype(F32)
        return (f32(r_ref), f32(k_ref), f32(v_ref), gc_ref[gi, row, sl, :], f32(a_ref), f32(gate_ref),
                s_ref[row, gi], kk_ref[gi], ka_ref[gi], rk_ref[gi], lng_ref[gi], lnb_ref[gi])

    def step(r, k, v, gcum, a_sig, gate, st0, kk_w, ka_w, rk_w, ln_g, ln_b):
        kk = k * kk_w
        kk = kk / jnp.maximum(jnp.sqrt(seg_sum(kk * kk)), 1e-12)
        k = k * (1.0 + (a_sig - 1.0) * ka_w)
        a_ = -kk
        b_ = kk * a_sig
        g_end = gcum[C - 1:C, :]
        g_excl = jnp.where(first_row, 0.0, pltpu.roll(gcum, 1, axis=0))
        e_neg = jnp.exp(-gcum)
        e_rest = jnp.exp(g_end - gcum)
        a_h = stack(a_ * jnp.exp(g_excl))
        r_h = stack(r * jnp.exp(gcum))
        b_t = stack(b_ * e_rest)
        k_t = stack(k * e_rest)
        twice = lambda z: jnp.concatenate([z, z], axis=0)
        v_s = twice(v).astype(BF16)
        lhs = jnp.concatenate([a_h, r_h], axis=0).astype(BF16)
        rhs = jnp.concatenate([twice(b_ * e_neg), twice(k * e_neg)], axis=0).astype(BF16)
        sc = _dot_nt(lhs, rhs)
        yield
        a_ab = jnp.where(strict, sc[:C2, :C2], 0.0)
        a_ak = jnp.where(strict, sc[:C2, C2:], 0.0)
        a_r = jnp.where(incl2, sc[C2:, :], 0.0).astype(BF16)
        w1 = _dot(a_ak.astype(BF16), v_s)
        tinv = eye2 + jnp.where(level_masks[0], a_ab, 0.0)
        for mk in level_masks[1:]:
            tb = tinv.astype(BF16)
            l21 = jnp.where(mk, a_ab, 0.0).astype(BF16)
            tl = _dot(tb, l21)
            yield
            tinv = tinv + _dot(tl.astype(BF16), tb)
            yield
        au = _dot(tinv.astype(BF16), jnp.concatenate([a_h, w1], axis=1).astype(BF16))
        yield
        amat = jnp.concatenate([au.astype(BF16),
                                jnp.concatenate([jnp.zeros((C2, LANES), BF16), v_s], axis=1)], axis=0)
        pz = _dot_tn(jnp.concatenate([b_t, k_t], axis=0).astype(BF16), amat)
        ry = _dot(a_r, amat)
        yield
        r_bar = r_h + ry[:, :LANES]
        sy = _dot(jnp.concatenate([r_bar, pz[:, :LANES]], axis=0).astype(BF16), st0.astype(BF16))
        yield
        st_new = st0 * jnp.broadcast_to(jnp.exp(g_end), (LANES, LANES)).T + sy[C2:, :] + pz[:, LANES:]
        ys = sy[:C2, :] + ry[:, LANES:]
        y = jnp.where(in_a, ys[:C, :], ys[C:, :])
        mean = seg_sum(y) * (1.0 / N)
        yc = y - mean
        var = seg_sum(yc * yc) * (1.0 / N)
        yn = yc * lax.rsqrt(var + RW_LN_EPS) * ln_g + ln_b
        bonus = seg_sum(r * k * rk_w) * v
        return st_new, ((yn + bonus) * gate).astype(out_ref.dtype)

    def body(c, carry):
        sl = pl.ds(pl.multiple_of(c * C, C), C)
        chains = [(row, gi) for row in range(r_ref.shape[1]) for gi in range(r_ref.shape[0])]
        results = _lockstep(step(*load(row, gi, sl)) for row, gi in chains)
        for (row, gi), res in zip(chains, results):
            s_ref[row, gi], out_ref[gi, row, sl, :] = res
        return carry

    lax.fori_loop(0, nc, body, 0)


def rwkv_scan(r, k, v, gcum, a, gate, k_k, k_a, r_k, ln_g, ln_b, b, t):
    p, n, l = r.shape
    tb = min(RW_TIME_BLOCK, t)
    rows = min(RW_ROWS, b)
    seq = lambda: pl.BlockSpec((p, rows, tb, l), lambda i, j: (0, i, j, 0))
    par = lambda: pl.BlockSpec((p, 1, l), lambda i, j: (0, 0, 0))
    view = lambda a: a.reshape(p, b, t, l)
    y = pl.pallas_call(
        _rwkv_scan_kernel,
        grid=(b // rows, t // tb),
        in_specs=[seq()] * 6 + [par()] * 5,
        out_specs=seq(),
        out_shape=jax.ShapeDtypeStruct((p, b, t, l), BF16),
        scratch_shapes=[pltpu.VMEM((rows, p, l, l), F32)],
        compiler_params=_cparams(("parallel", "arbitrary")),
        name="rwkv_scan",
    )(view(r), view(k), view(v), view(gcum), view(a), view(gate), k_k, k_a, r_k, ln_g, ln_b)
    return y.reshape(p, n, l)


def rwkv_mixer(x, g_pre, mu, w_rkv, w0, w1, w2, a0, a1, a2, g1, g2,
               k_k, k_a, r_k, ln_g, ln_b, b, t):
    bf = lambda a: a.astype(BF16)
    row = lambda a: a.reshape(1, -1)
    r, k, v, gcum, a, gate = rwkv_proj(x, g_pre, mu, bf(w_rkv), row(w0), bf(w1), bf(w2),
                                     row(a0), bf(a1), bf(a2), bf(g1), bf(g2), t)
    par = lambda a: a.reshape(RW_PAIRS, 1, LANES)
    return rwkv_scan(r, k, v, gcum, a, gate, par(k_k), par(k_a), par(r_k), par(ln_g), par(ln_b), b, t)


def kernel(x, positions, norm_g, ffn_w_gu, ffn_w_down, ml_w_in, ml_b_if, ml_conv_w, ml_norm_g, ml_w_out,
           rw_mu, rw_w_rkv, rw_w0, rw_w1, rw_w2, rw_a0, rw_a1, rw_a2, rw_g1, rw_g2,
           rw_k_k, rw_k_a, rw_r_k, rw_ln_g, rw_ln_b, rw_w_out, rt_w_in, rt_w_out):
    b, t, d = x.shape
    depth = norm_g.shape[0]
    xf = x.reshape(b * t, d)
    for layer in range(depth):
        g = norm_g[layer][:, None, :]
        xf = ffn_sublayer(xf, g[0], ffn_w_gu[layer, 0].astype(BF16), ffn_w_down[layer, 0].astype(BF16), g[1])
        kind, j = layer % 3, layer // 3
        if kind == 0:
            h = mlstm_mixer(xf, g[2], ml_w_in[j], ml_b_if[j], ml_conv_w[j], ml_norm_g[j], b, t)
            w_out = ml_w_out[j]
        elif kind == 1:
            h = rwkv_mixer(xf, g[2], rw_mu[j], rw_w_rkv[j], rw_w0[j], rw_w1[j], rw_w2[j],
                           rw_a0[j], rw_a1[j], rw_a2[j], rw_g1[j], rw_g2[j],
                           rw_k_k[j], rw_k_a[j], rw_r_k[j].reshape(-1), rw_ln_g[j], rw_ln_b[j], b, t)
            w_out = rw_w_out[j]
        else:
            h = retention_mixer(xf, positions, g[2], rt_w_in[j], b, t)
            w_out = rt_w_out[j]
        xf = ffn_sublayer(xf, g[4], ffn_w_gu[layer, 1].astype(BF16), ffn_w_down[layer, 1].astype(BF16), g[5],
                          mix=(h, w_out.astype(BF16), g[3]))
    return xf.reshape(b, t, d)
```

```python
import functools
import math

import jax
import jax.numpy as jnp
from jax import lax
from jax.experimental import pallas as pl
from jax.experimental.pallas import tpu as pltpu

F32 = jnp.float32
BF16 = jnp.bfloat16

D_MODEL = 1024
NORM_EPS = 1e-6
HALF_STEP = 0.5

ML_HEADS = 4
ML_DQK = 128
ML_DV = 256
ML_CHUNK = 256
ML_GATE_CAP = 15.0
ML_ROWS = 2
ML_TIME_BLOCK = 1024

RW_HEAD = 64
RW_PAIRS = D_MODEL // (2 * RW_HEAD)
RW_CHUNK = 64
RW_TIME_BLOCK = 256
RW_ROWS = 2
RW_LN_EPS = 64e-5

RT_HEADS = 4
RT_DK = 256
RT_DV = 512
RT_CHUNK = 256
RT_TIME_BLOCK = 512
RT_ROPE_BASE = 10000.0

LANES = 128
SUBLANES = 8
MXU_DEPTH = 256
VMEM_LIMIT = 56 * 1024 * 1024


def _cparams(sem):
    return pltpu.CompilerParams(dimension_semantics=sem, vmem_limit_bytes=VMEM_LIMIT)


def _lockstep(gens):
    gens = list(gens)
    results = [None] * len(gens)
    live = list(range(len(gens)))
    while live:
        for i in list(live):
            try:
                next(gens[i])
            except StopIteration as done:
                results[i] = done.value
                live.remove(i)
    return results


def _rms(x, g):
    return x * lax.rsqrt(jnp.mean(x * x, axis=-1, keepdims=True) + NORM_EPS) * g


def _dot(a, b):
    return jnp.dot(a, b, preferred_element_type=F32)


def _dot_nt(a, b):
    return lax.dot_general(a, b, (((1,), (1,)), ((), ())), preferred_element_type=F32)


def _dot_tn(a, b):
    return lax.dot_general(a, b, (((0,), (0,)), ((), ())), preferred_element_type=F32)


def _split3(x):
    hi = x.astype(BF16)
    r1 = x - hi.astype(F32)
    mid = r1.astype(BF16)
    lo = (r1 - mid.astype(F32)).astype(BF16)
    return hi, mid, lo


def _sigmoid(x):
    return 1.0 / (1.0 + jnp.exp(-x))


def _softplus(x):
    return jnp.maximum(x, 0.0) + jnp.log(1.0 + jnp.exp(-jnp.abs(x)))


def _ffn_kernel(*refs, mixer):
    if mixer is None:
        x_ref, gpre_ref, wgu_ref, wd_ref, gpost_ref, o_ref = refs
    else:
        h_ref, wo_ref, gmix_ref, x_ref, gpre_ref, wgu_ref, wd_ref, gpost_ref, o_ref = refs
    f = wd_ref.shape[0]
    half = x_ref.shape[0] // 2

    def rows(lo):
        x = x_ref[lo:lo + half, :]
        if mixer is not None:
            if mixer == "pairs":
                h = jnp.concatenate([h_ref[p, lo:lo + half, :] for p in range(h_ref.shape[0])], axis=-1)
            else:
                h = h_ref[lo:lo + half, :]
            y = _dot(h, wo_ref[...])
            yield
            x = x + _rms(y, gmix_ref[...])
        xn = _rms(x, gpre_ref[...]).astype(BF16)
        yield
        gate = _dot(xn, wgu_ref[:, :f])
        up = _dot(xn, wgu_ref[:, f:])
        yield
        act = (gate * _sigmoid(gate) * up).astype(BF16)
        h2 = _dot(act, wd_ref[...])
        yield
        o_ref[lo:lo + half, :] = x + HALF_STEP * _rms(h2, gpost_ref[...])

    _lockstep(rows(lo) for lo in (0, half))


def _resident(shape):
    return pl.BlockSpec(shape, lambda *_: (0,) * len(shape), pipeline_mode=pl.Buffered(1))


def ffn_sublayer(x, g_pre, w_gu, w_down, g_post, mix=None, *, tm=512):
    n, d = x.shape
    tm = min(tm, n)
    ffn_args = (x, g_pre, w_gu, w_down, g_post)
    ffn_specs = [pl.BlockSpec((tm, d), lambda i: (i, 0)), _resident(g_pre.shape), _resident(w_gu.shape),
                 _resident(w_down.shape), _resident(g_post.shape)]
    mixer, mix_args, mix_specs = None, (), []
    if mix is not None:
        h, w_out, g_mix = mix
        if h.ndim == 3:
            mixer = "pairs"
            h_spec = pl.BlockSpec((h.shape[0], tm, h.shape[2]), lambda i: (0, i, 0))
        else:
            mixer = "rows"
            h_spec = pl.BlockSpec((tm, h.shape[1]), lambda i: (i, 0))
        mix_args = (h, w_out, g_mix)
        mix_specs = [h_spec, _resident(w_out.shape), _resident(g_mix.shape)]
    return pl.pallas_call(
        functools.partial(_ffn_kernel, mixer=mixer),
        grid=(n // tm,),
        in_specs=mix_specs + ffn_specs,
        out_specs=pl.BlockSpec((tm, d), lambda i: (i, 0)),
        out_shape=jax.ShapeDtypeStruct((n, d), F32),
        compiler_params=_cparams(("parallel",)),
        name="ffn" if mix is None else "mix_ffn",
    )(*mix_args, *ffn_args)


def _norm_matmul_kernel(x_ref, g_ref, w_ref, o_ref):
    half = x_ref.shape[0] // 2

    def rows(lo):
        xn = _rms(x_ref[lo:lo + half, :], g_ref[...]).astype(BF16)
        yield
        o_ref[lo:lo + half, :] = _dot(xn, w_ref[...]).astype(o_ref.dtype)

    _lockstep(rows(lo) for lo in (0, half))


def norm_matmul(x, g, w, out_dtype, *, tm=512):
    n, d = x.shape
    nout = w.shape[1]
    tm = min(tm, n)
    return pl.pallas_call(
        _norm_matmul_kernel,
        grid=(n // tm,),
        in_specs=[
            pl.BlockSpec((tm, d), lambda i: (i, 0)),
            _resident(g.shape),
            _resident(w.shape),
        ],
        out_specs=pl.BlockSpec((tm, nout), lambda i: (i, 0)),
        out_shape=jax.ShapeDtypeStruct((n, nout), out_dtype),
        compiler_params=_cparams(("parallel",)),
        name="norm_matmul",
    )(x, g, w)


def _mlstm_proj_kernel(x_ref, g_ref, w_ref, wif_ref, bif_ref, proj_ref, gates_ref):
    H = ML_HEADS
    half = x_ref.shape[0] // 2

    def rows(lo):
        xn = _rms(x_ref[lo:lo + half, :], g_ref[...]).astype(BF16)
        yield
        proj_ref[lo:lo + half, :] = _dot(xn, w_ref[...]).astype(proj_ref.dtype)
        pre = _dot_nt(wif_ref[...], xn) + bif_ref[...]
        pre = ML_GATE_CAP * jnp.tanh(pre * (1.0 / ML_GATE_CAP))
        is_input_gate = lax.broadcasted_iota(jnp.int32, pre.shape, 0) < H
        gates_ref[:, lo:lo + half] = jnp.where(is_input_gate, pre, -_softplus(-pre))

    _lockstep(rows(lo) for lo in (0, half))


def mlstm_proj(x, g, w_main, w_if_t, b_if, *, tm=512):
    n, d = x.shape
    tm = min(tm, n)
    nout = w_main.shape[1]
    ng = w_if_t.shape[0]
    return pl.pallas_call(
        _mlstm_proj_kernel,
        grid=(n // tm,),
        in_specs=[pl.BlockSpec((tm, d), lambda i: (i, 0)), _resident(g.shape), _resident(w_main.shape),
                  _resident(w_if_t.shape), _resident(b_if.shape)],
        out_specs=[pl.BlockSpec((tm, nout), lambda i: (i, 0)), pl.BlockSpec((ng, tm), lambda i: (0, i))],
        out_shape=[jax.ShapeDtypeStruct((n, nout), BF16), jax.ShapeDtypeStruct((ng, n), F32)],
        compiler_params=_cparams(("parallel",)),
        name="mlstm_proj",
    )(x, g, w_main, w_if_t, b_if)


def _mlstm_kernel(qk_ref, v_ref, o_ref, gates_ref, cw_ref, ng_ref, out_ref,
                  c_ref, n_ref, m_ref, tq_ref, tk_ref):
    L, H, dk, dv = ML_CHUNK, ML_HEADS, ML_DQK, ML_DV
    R = qk_ref.shape[0]
    nc = qk_ref.shape[1] // L
    KW = cw_ref.shape[0]
    TAIL = SUBLANES

    @pl.when(pl.program_id(1) == 0)
    def _():
        for ref in (c_ref, n_ref, m_ref, tq_ref, tk_ref):
            ref[...] = jnp.zeros_like(ref)

    row = lax.broadcasted_iota(jnp.int32, (L, L), 0)
    col = lax.broadcasted_iota(jnp.int32, (L, L), 1)
    tri = row >= col
    eye = row == col
    ones = jnp.ones((L, L), BF16)

    def conv_silu(x, tail, w):
        ext = jnp.concatenate([tail, x], axis=0)
        acc = x * w[KW - 1:KW, :]
        for i in range(KW - 1):
            off = TAIL - (KW - 1) + i
            acc = acc + ext[off:off + L, :] * w[i:i + 1, :]
        return acc * _sigmoid(acc)

    def step(r, hd, c, sl, m, tq, tk):
        ci = r * H + hd
        xq = qk_ref[r, sl, hd * dk:(hd + 1) * dk].astype(F32)
        xk = qk_ref[r, sl, (H + hd) * dk:(H + hd + 1) * dk].astype(F32)
        qc = (conv_silu(xq, tq, cw_ref[:, hd * dk:(hd + 1) * dk]) * (dk ** -0.5)).astype(BF16)
        kc = conv_silu(xk, tk, cw_ref[:, (H + hd) * dk:(H + hd + 1) * dk]).astype(BF16)
        vc = v_ref[r, sl, hd * dv:(hd + 1) * dv]
        irow = gates_ref[r, hd:hd + 1, sl]
        frow = gates_ref[r, H + hd:H + hd + 1, sl]
        fh, fm, fl = _split3(jnp.where(tri, frow, 0.0))
        bmat = _dot(fh, ones) + _dot(fm, ones) + _dot(fl, ones)
        brow = _dot_nt(ones, fh) + _dot_nt(ones, fm) + _dot_nt(ones, fl)
        qk = _dot_nt(qc, kc)
        cmat = c_ref[ci]
        nrow = n_ref[ci]
        yield
        log_d = jnp.where(tri, bmat - brow + irow, -jnp.inf)
        bcol = bmat[:, 0:1]
        log_inter = bcol + m
        m_t = jnp.maximum(log_inter, jnp.max(log_d, axis=-1, keepdims=True))
        s = qk * jnp.exp(log_d - m_t)
        inter = jnp.exp(log_inter - m_t)
        b_end = bmat[L - 1:L, 0:1]
        log_w = b_end - brow[0:1, :] + irow
        m_new = jnp.maximum(b_end + m, jnp.max(log_w, axis=-1, keepdims=True))
        w_row = jnp.exp(log_w - m_new)
        carry_decay = jnp.exp(b_end + m - m_new)
        wdiag = jnp.where(eye, w_row, 0.0).astype(BF16)
        wv = _dot(wdiag, vc).astype(BF16)
        n_add = _dot(jnp.broadcast_to(w_row, (SUBLANES, L)).astype(BF16), kc)[0:1, :]
        num = _dot(s.astype(BF16), vc) + inter * _dot(qc, cmat.astype(BF16))
        yield
        c_new = carry_decay * cmat + _dot_tn(kc, wv)
        n_new = carry_decay * nrow + n_add
        qn = jnp.sum(qc.astype(F32) * nrow, axis=-1, keepdims=True)
        den = jnp.sum(s, axis=-1, keepdims=True) + inter * qn
        h = num / jnp.maximum(jnp.abs(den), jnp.exp(-m_t))
        ng = ng_ref[:, hd * dv:(hd + 1) * dv]
        hn = h * lax.rsqrt(jnp.mean(h * h, axis=-1, keepdims=True) + NORM_EPS) * ng
        og = o_ref[r, sl, hd * dv:(hd + 1) * dv].astype(F32)
        out = (hn * _sigmoid(og)).astype(out_ref.dtype)
        return (m_new, xq[L - TAIL:, :], xk[L - TAIL:, :]), c_new, n_new, out

    chains = [(r, hd) for r in range(R) for hd in range(H)]

    def body(c, carries):
        sl = pl.ds(pl.multiple_of(c * L, L), L)
        res = _lockstep(step(r, hd, c, sl, *carries[ci]) for ci, (r, hd) in enumerate(chains))
        for ci, (r, hd) in enumerate(chains):
            _, c_ref[ci], n_ref[ci], out_ref[r, sl, hd * dv:(hd + 1) * dv] = res[ci]
        return tuple(x[0] for x in res)

    init = tuple((m_ref[ci][0:1, 0:1], tq_ref[ci], tk_ref[ci]) for ci in range(len(chains)))
    final = lax.fori_loop(0, nc, body, init)
    for ci, (m, tq, tk) in enumerate(final):
        m_ref[ci] = jnp.broadcast_to(m, m_ref.shape[1:])
        tq_ref[ci] = tq
        tk_ref[ci] = tk


def mlstm_scan(proj, gates, conv_w, ng):
    b, t, _ = proj.shape
    h, dk, dv, L = ML_HEADS, ML_DQK, ML_DV, ML_CHUNK
    rows = min(ML_ROWS, b)
    tb = min(ML_TIME_BLOCK, t)
    wqk, wv = 2 * h * dk, h * dv
    nchains = rows * h
    seq = lambda w, col: pl.BlockSpec((rows, tb, w), lambda i, j: (i, j, col))
    return pl.pallas_call(
        _mlstm_kernel,
        grid=(b // rows, t // tb),
        in_specs=[seq(wqk, 0), seq(wv, wqk // wv), seq(wv, wqk // wv + 1),
                  pl.BlockSpec((rows, 2 * h, tb), lambda i, j: (i, 0, j)),
                  _resident(conv_w.shape), _resident(ng.shape)],
        out_specs=pl.BlockSpec((rows, tb, wv), lambda i, j: (i, j, 0)),
        out_shape=jax.ShapeDtypeStruct((b, t, wv), BF16),
        scratch_shapes=[pltpu.VMEM((nchains, dk, dv), F32), pltpu.VMEM((nchains, 1, dk), F32),
                        pltpu.VMEM((nchains, SUBLANES, LANES), F32), pltpu.VMEM((nchains, SUBLANES, dk), F32),
                        pltpu.VMEM((nchains, SUBLANES, dk), F32)],
        compiler_params=_cparams(("parallel", "arbitrary")),
        name="mlstm_scan",
    )(proj, proj, proj, gates, conv_w, ng)


def mlstm_mixer(x, g_pre, w_in, b_if, conv_w, norm_g, b, t):
    h, dk, dv, L = ML_HEADS, ML_DQK, ML_DV, ML_CHUNK
    n_main = 2 * h * dk + 2 * h * dv
    w_main = w_in[:, :n_main].astype(BF16)
    w_if_t = w_in[:, n_main:].T.astype(BF16)
    proj, gates = mlstm_proj(x, g_pre, w_main, w_if_t, b_if.reshape(2 * h, 1))
    gates = jnp.transpose(gates.reshape(2 * h, b, t), (1, 0, 2))
    hs = mlstm_scan(proj.reshape(b, t, n_main), gates, conv_w, norm_g[None, :])
    return hs.reshape(b * t, h * dv)


def _retention_kernel(lg_ref, q_ref, k_ref, v_ref, g_ref, cos_ref, sin_ref, out_ref, r_ref):
    L, H, dk, dv = RT_CHUNK, RT_HEADS, RT_DK, RT_DV
    half = dk // 2
    nc = q_ref.shape[1] // L

    @pl.when(pl.program_id(1) == 0)
    def _():
        r_ref[...] = jnp.zeros_like(r_ref)

    row = lax.broadcasted_iota(jnp.int32, (L, L), 0)
    col = lax.broadcasted_iota(jnp.int32, (L, L), 1)
    diff = jnp.maximum(row - col, 0).astype(F32)
    idx = lax.broadcasted_iota(jnp.int32, (L, 1), 0).astype(F32)

    def rotate(u, cos, sin):
        u1, u2 = u[:, :half], u[:, half:]
        return jnp.concatenate([u1 * cos - u2 * sin, u1 * sin + u2 * cos], axis=-1)

    def step(hd, sl, cos, sin):
        log_gamma = lg_ref[hd]
        qr = rotate(q_ref[0, sl, hd * dk:(hd + 1) * dk].astype(F32), cos, sin)
        kr = rotate(k_ref[0, sl, hd * dk:(hd + 1) * dk].astype(F32), cos, sin) * (dk ** -0.5)
        vc = v_ref[0, sl, hd * dv:(hd + 1) * dv]
        qb = qr.astype(BF16)
        qk = _dot_nt(qb, kr.astype(BF16))
        rmat = r_ref[hd]
        k_decay = jnp.exp(log_gamma * (L - 1.0 - idx))
        r_add = _dot_tn((kr * k_decay).astype(BF16), vc)
        cross = _dot(qb, rmat.astype(BF16))
        yield
        d_mask = jnp.where(row >= col, jnp.exp(log_gamma * diff), 0.0)
        q_decay = jnp.exp(log_gamma * (idx + 1.0))
        y = _dot((qk * d_mask).astype(BF16), vc) + cross * q_decay
        yield
        r_new = jnp.exp(jnp.full((1, 1), L, F32) * log_gamma) * rmat + r_add
        yn = y * lax.rsqrt(jnp.mean(y * y, axis=-1, keepdims=True) + NORM_EPS)
        gc = g_ref[0, sl, hd * dv:(hd + 1) * dv].astype(F32)
        return r_new, (gc * _sigmoid(gc) * yn).astype(out_ref.dtype)

    def body(c, carry):
        sl = pl.ds(pl.multiple_of(c * L, L), L)
        cos = cos_ref[0, sl, :]
        sin = sin_ref[0, sl, :]
        res = _lockstep(step(hd, sl, cos, sin) for hd in range(H))
        for hd in range(H):
            r_ref[hd], out_ref[0, sl, hd * dv:(hd + 1) * dv] = res[hd]
        return carry

    lax.fori_loop(0, nc, body, 0)


def retention_scan(proj, cos, sin, log_gamma):
    b, t, _ = proj.shape
    h, dk, dv = RT_HEADS, RT_DK, RT_DV
    tb = min(RT_TIME_BLOCK, t)
    wq, wv = h * dk, h * dv
    seq = lambda w, col: pl.BlockSpec((1, tb, w), lambda i, j: (i, j, col))
    return pl.pallas_call(
        _retention_kernel,
        grid=(b, t // tb),
        in_specs=[pl.BlockSpec(memory_space=pltpu.SMEM),
                  seq(wq, 0), seq(wq, 1), seq(wv, 2 * wq // wv), seq(wv, 2 * wq // wv + 1),
                  seq(dk // 2, 0), seq(dk // 2, 0)],
        out_specs=seq(wv, 0),
        out_shape=jax.ShapeDtypeStruct((b, t, wv), BF16),
        scratch_shapes=[pltpu.VMEM((h, dk, dv), F32)],
        compiler_params=_cparams(("parallel", "arbitrary")),
        name="retention_scan",
    )(log_gamma, proj, proj, proj, proj, cos, sin)


def retention_mixer(x, positions, g_pre, w_in, b, t):
    h, dk, dv = RT_HEADS, RT_DK, RT_DV
    proj = norm_matmul(x, g_pre, w_in.astype(BF16), BF16).reshape(b, t, -1)
    inv_freq = 1.0 / (RT_ROPE_BASE ** jnp.linspace(0.0, 1.0, dk // 2, dtype=F32))
    ang = positions.astype(F32)[:, :, None] * inv_freq
    log_gamma = jnp.log(1.0 - 2.0 ** (-5.0 - jnp.arange(h, dtype=F32)))
    y = retention_scan(proj, jnp.cos(ang), jnp.sin(ang), log_gamma)
    return y.reshape(b * t, h * dv)


def _rwkv_proj_kernel(x_ref, xp_ref, g_ref, mu_ref, wrkv_ref, w0_ref, w1_ref, w2_ref,
                      a0_ref, a1_ref, a2_ref, g1_ref, g2_ref,
                      r_ref, k_ref, v_ref, gc_ref, a_ref, gate_ref, *, seq_tiles):
    i = pl.program_id(0)
    tm = x_ref.shape[0]
    g = g_ref[...]
    h = _rms(x_ref[...], g)
    prev = _rms(xp_ref[...], g)[SUBLANES - 1:SUBLANES, :]
    prev = prev * (i % seq_tiles != 0).astype(F32)
    rid = lax.broadcasted_iota(jnp.int32, (tm, 1), 0)
    h_prev = jnp.where(rid == 0, prev, pltpu.roll(h, 1, axis=0))
    span = min(tm, MXU_DEPTH)
    rr = lax.broadcasted_iota(jnp.int32, (span, span), 0)
    cc = lax.broadcasted_iota(jnp.int32, (span, span), 1)
    chunk_tril = ((rr // RW_CHUNK == cc // RW_CHUNK) & (rr >= cc)).astype(BF16)

    def rows(lo):
        hh = h[lo:lo + span, :]
        xx = h_prev[lo:lo + span, :] - hh
        mix = lambda j: (hh + xx * mu_ref[j:j + 1, :]).astype(BF16)

        def put(ref, val):
            for p in range(ref.shape[0]):
                ref[p, lo:lo + span, :] = val[:, p * LANES:(p + 1) * LANES].astype(ref.dtype)

        put(r_ref, _dot(mix(0), wrkv_ref[0]))
        yield
        put(k_ref, _dot(mix(1), wrkv_ref[1]))
        yield
        put(v_ref, _dot(mix(2), wrkv_ref[2]))
        yield
        wl = jnp.tanh(_dot(mix(3), w1_ref[...])).astype(BF16)
        al = _dot(mix(4), a1_ref[...]).astype(BF16)
        gl = _sigmoid(_dot(mix(5), g1_ref[...])).astype(BF16)
        yield
        w_log = -_softplus(-(w0_ref[...] + _dot(wl, w2_ref[...]))) - 0.5
        put(a_ref, _sigmoid(a0_ref[...] + _dot(al, a2_ref[...])))
        put(gate_ref, _dot(gl, g2_ref[...]))
        yield
        lh, lm, ll = _split3(-jnp.exp(w_log))
        put(gc_ref, _dot(chunk_tril, lh) + _dot(chunk_tril, lm) + _dot(chunk_tril, ll))

    _lockstep(rows(lo) for lo in range(0, tm, span))


def rwkv_proj(x, g_pre, mu, w_rkv, w0, w1, w2, a0, a1, a2, g1, g2, t, *, tm=512):
    n, d = x.shape
    tm = min(tm, t)
    p = d // LANES
    outs = pl.BlockSpec((p, tm, LANES), lambda i: (0, i, 0))
    args = (g_pre, mu, w_rkv, w0, w1, w2, a0, a1, a2, g1, g2)
    shp = lambda dt: jax.ShapeDtypeStruct((p, n, LANES), dt)
    return pl.pallas_call(
        functools.partial(_rwkv_proj_kernel, seq_tiles=t // tm),
        grid=(n // tm,),
        in_specs=[
            pl.BlockSpec((tm, d), lambda i: (i, 0)),
            pl.BlockSpec((SUBLANES, d), lambda i: (jnp.maximum(i * (tm // SUBLANES) - 1, 0), 0)),
        ] + [_resident(a.shape) for a in args],
        out_specs=[outs] * 6,
        out_shape=[shp(BF16), shp(BF16), shp(BF16), shp(F32), shp(BF16), shp(BF16)],
        compiler_params=_cparams(("parallel",)),
        name="rwkv_proj",
    )(x, x, *args)


def _rwkv_scan_kernel(r_ref, k_ref, v_ref, gc_ref, a_ref, gate_ref, kk_ref, ka_ref, rk_ref,
                      lng_ref, lnb_ref, out_ref, s_ref):
    C = RW_CHUNK
    N = RW_HEAD
    C2 = 2 * C
    nc = r_ref.shape[2] // C

    @pl.when(pl.program_id(1) == 0)
    def _():
        s_ref[...] = jnp.zeros_like(s_ref)

    lane = lax.broadcasted_iota(jnp.int32, (1, LANES), 1)
    in_a = lane < N
    m_a = in_a.astype(F32)
    m_b = 1.0 - m_a
    rr = lax.broadcasted_iota(jnp.int32, (C2, C2), 0)
    cc = lax.broadcasted_iota(jnp.int32, (C2, C2), 1)
    same = (rr // C) == (cc // C)
    strict = same & ((rr % C) > (cc % C))
    incl = same & ((rr % C) >= (cc % C))
    eye2 = (rr == cc).astype(F32)
    level_masks = []
    for lvl in range(int(math.log2(C))):
        rb, cb = (rr % C) >> lvl, (cc % C) >> lvl
        level_masks.append(same & ((rb & 1) == 1) & (cb == rb - 1))
    incl2 = jnp.concatenate([incl, incl], axis=1)
    first_row = lax.broadcasted_iota(jnp.int32, (C, 1), 0) == 0

    def seg_sum(z):
        sa = jnp.sum(z * m_a, axis=-1, keepdims=True)
        sb = jnp.sum(z * m_b, axis=-1, keepdims=True)
        return jnp.where(in_a, sa, sb)

    def stack(z):
        return jnp.concatenate([z * m_a, z * m_b], axis=0)

    def load(row, gi, sl):
        f32 = lambda ref: ref[gi, row, sl, :].astype(F32)
        return (f32(r_ref), f32(k_ref), f32(v_ref), gc_ref[gi, row, sl, :], f32(a_ref), f32(gate_ref),
                s_ref[row, gi], kk_ref[gi], ka_ref[gi], rk_ref[gi], lng_ref[gi], lnb_ref[gi])

    def step(r, k, v, gcum, a_sig, gate, st0, kk_w, ka_w, rk_w, ln_g, ln_b):
        kk = k * kk_w
        kk = kk / jnp.maximum(jnp.sqrt(seg_sum(kk * kk)), 1e-12)
        k = k * (1.0 + (a_sig - 1.0) * ka_w)
        a_ = -kk
        b_ = kk * a_sig
        g_end = gcum[C - 1:C, :]
        g_excl = jnp.where(first_row, 0.0, pltpu.roll(gcum, 1, axis=0))
        e_neg = jnp.exp(-gcum)
        e_rest = jnp.exp(g_end - gcum)
        a_h = stack(a_ * jnp.exp(g_excl))
        r_h = stack(r * jnp.exp(gcum))
        b_t = stack(b_ * e_rest)
        k_t = stack(k * e_rest)
        twice = lambda z: jnp.concatenate([z, z], axis=0)
        v_s = twice(v).astype(BF16)
        lhs = jnp.concatenate([a_h, r_h], axis=0).astype(BF16)
        rhs = jnp.concatenate([twice(b_ * e_neg), twice(k * e_neg)], axis=0).astype(BF16)
        sc = _dot_nt(lhs, rhs)
        yield
        a_ab = jnp.where(strict, sc[:C2, :C2], 0.0)
        a_ak = jnp.where(strict, sc[:C2, C2:], 0.0)
        a_r = jnp.where(incl2, sc[C2:, :], 0.0).astype(BF16)
        w1 = _dot(a_ak.astype(BF16), v_s)
        tinv = eye2 + jnp.where(level_masks[0], a_ab, 0.0)
        for mk in level_masks[1:]:
            tb = tinv.astype(BF16)
            l21 = jnp.where(mk, a_ab, 0.0).astype(BF16)
            tl = _dot(tb, l21)
            yield
            tinv = tinv + _dot(tl.astype(BF16), tb)
            yield
        au = _dot(tinv.astype(BF16), jnp.concatenate([a_h, w1], axis=1).astype(BF16))
        yield
        amat = jnp.concatenate([au.astype(BF16),
                                jnp.concatenate([jnp.zeros((C2, LANES), BF16), v_s], axis=1)], axis=0)
        pz = _dot_tn(jnp.concatenate([b_t, k_t], axis=0).astype(BF16), amat)
        ry = _dot(a_r, amat)
        yield
        r_bar = r_h + ry[:, :LANES]
        sy = _dot(jnp.concatenate([r_bar, pz[:, :LANES]], axis=0).astype(BF16), st0.astype(BF16))
        yield
        st_new = st0 * jnp.broadcast_to(jnp.exp(g_end), (LANES, LANES)).T + sy[C2:, :] + pz[:, LANES:]
        ys = sy[:C2, :] + ry[:, LANES:]
        y = jnp.where(in_a, ys[:C, :], ys[C:, :])
        mean = seg_sum(y) * (1.0 / N)
        yc = y - mean
        var = seg_sum(yc * yc) * (1.0 / N)
        yn = yc * lax.rsqrt(var + RW_LN_EPS) * ln_g + ln_b
        bonus = seg_sum(r * k * rk_w) * v
        return st_new, ((yn + bonus) * gate).astype(out_ref.dtype)

    def body(c, carry):
        sl = pl.ds(pl.multiple_of(c * C, C), C)
        chains = [(row, gi) for row in range(r_ref.shape[1]) for gi in range(r_ref.shape[0])]
        results = _lockstep(step(*load(row, gi, sl)) for row, gi in chains)
        for (row, gi), res in zip(chains, results):
            s_ref[row, gi], out_ref[gi, row, sl, :] = res
        return carry

    lax.fori_loop(0, nc, body, 0)


def rwkv_scan(r, k, v, gcum, a, gate, k_k, k_a, r_k, ln_g, ln_b, b, t):
    p, n, l = r.shape
    tb = min(RW_TIME_BLOCK, t)
    rows = min(RW_ROWS, b)
    seq = lambda: pl.BlockSpec((p, rows, tb, l), lambda i, j: (0, i, j, 0))
    par = lambda: pl.BlockSpec((p, 1, l), lambda i, j: (0, 0, 0))
    view = lambda a: a.reshape(p, b, t, l)
    y = pl.pallas_call(
        _rwkv_scan_kernel,
        grid=(b // rows, t // tb),
        in_specs=[seq()] * 6 + [par()] * 5,
        out_specs=seq(),
        out_shape=jax.ShapeDtypeStruct((p, b, t, l), BF16),
        scratch_shapes=[pltpu.VMEM((rows, p, l, l), F32)],
        compiler_params=_cparams(("parallel", "arbitrary")),
        name="rwkv_scan",
    )(view(r), view(k), view(v), view(gcum), view(a), view(gate), k_k, k_a, r_k, ln_g, ln_b)
    return y.reshape(p, n, l)


def rwkv_mixer(x, g_pre, mu, w_rkv, w0, w1, w2, a0, a1, a2, g1, g2,
               k_k, k_a, r_k, ln_g, ln_b, b, t):
    bf = lambda a: a.astype(BF16)
    row = lambda a: a.reshape(1, -1)
    r, k, v, gcum, a, gate = rwkv_proj(x, g_pre, mu, bf(w_rkv), row(w0), bf(w1), bf(w2),
                                     row(a0), bf(a1), bf(a2), bf(g1), bf(g2), t)
    par = lambda a: a.reshape(RW_PAIRS, 1, LANES)
    return rwkv_scan(r, k, v, gcum, a, gate, par(k_k), par(k_a), par(r_k), par(ln_g), par(ln_b), b, t)


def kernel(x, positions, norm_g, ffn_w_gu, ffn_w_down, ml_w_in, ml_b_if, ml_conv_w, ml_norm_g, ml_w_out,
           rw_mu, rw_w_rkv, rw_w0, rw_w1, rw_w2, rw_a0, rw_a1, rw_a2, rw_g1, rw_g2,
           rw_k_k, rw_k_a, rw_r_k, rw_ln_g, rw_ln_b, rw_w_out, rt_w_in, rt_w_out):
    b, t, d = x.shape
    depth = norm_g.shape[0]
    xf = x.reshape(b * t, d)
    for layer in range(depth):
        g = norm_g[layer][:, None, :]
        xf = ffn_sublayer(xf, g[0], ffn_w_gu[layer, 0].astype(BF16), ffn_w_down[layer, 0].astype(BF16), g[1])
        kind, j = layer % 3, layer // 3
        if kind == 0:
            h = mlstm_mixer(xf, g[2], ml_w_in[j], ml_b_if[j], ml_conv_w[j], ml_norm_g[j], b, t)
            w_out = ml_w_out[j]
        elif kind == 1:
            h = rwkv_mixer(xf, g[2], rw_mu[j], rw_w_rkv[j], rw_w0[j], rw_w1[j], rw_w2[j],
                           rw_a0[j], rw_a1[j], rw_a2[j], rw_g1[j], rw_g2[j],
                           rw_k_k[j], rw_k_a[j], rw_r_k[j].reshape(-1), rw_ln_g[j], rw_ln_b[j], b, t)
            w_out = rw_w_out[j]
        else:
            h = retention_mixer(xf, positions, g[2], rt_w_in[j], b, t)
            w_out = rt_w_out[j]
        xf = ffn_sublayer(xf, g[4], ffn_w_gu[layer, 1].astype(BF16), ffn_w_down[layer, 1].astype(BF16), g[5],
                          mix=(h, w_out.astype(BF16), g[3]))
    return xf.reshape(b, t, d)
```

```python
import functools
import math

import jax
import jax.numpy as jnp
from jax import lax
from jax.experimental import pallas as pl
from jax.experimental.pallas import tpu as pltpu

F32 = jnp.float32
BF16 = jnp.bfloat16

D_MODEL = 1024
NORM_EPS = 1e-6
HALF_STEP = 0.5

ML_HEADS = 4
ML_DQK = 128
ML_DV = 256
ML_CHUNK = 256
ML_GATE_CAP = 15.0
ML_ROWS = 2
ML_TIME_BLOCK = 1024

RW_HEAD = 64
RW_PAIRS = D_MODEL // (2 * RW_HEAD)
RW_CHUNK = 64
RW_TIME_BLOCK = 256
RW_ROWS = 2
RW_LN_EPS = 64e-5

RT_HEADS = 4
RT_DK = 256
RT_DV = 512
RT_CHUNK = 256
RT_TIME_BLOCK = 512
RT_ROPE_BASE = 10000.0

LANES = 128
SUBLANES = 8
MXU_DEPTH = 256
VMEM_LIMIT = 56 * 1024 * 1024


def _cparams(sem):
    return pltpu.CompilerParams(dimension_semantics=sem, vmem_limit_bytes=VMEM_LIMIT)


def _lockstep(gens):
    gens = list(gens)
    results = [None] * len(gens)
    live = list(range(len(gens)))
    while live:
        for i in list(live):
            try:
                next(gens[i])
            except StopIteration as done:
                results[i] = done.value
                live.remove(i)
    return results


def _rms(x, g):
    return x * lax.rsqrt(jnp.mean(x * x, axis=-1, keepdims=True) + NORM_EPS) * g


def _dot(a, b):
    return jnp.dot(a, b, preferred_element_type=F32)


def _dot_nt(a, b):
    return lax.dot_general(a, b, (((1,), (1,)), ((), ())), preferred_element_type=F32)


def _dot_tn(a, b):
    return lax.dot_general(a, b, (((0,), (0,)), ((), ())), preferred_element_type=F32)


def _split3(x):
    hi = x.astype(BF16)
    r1 = x - hi.astype(F32)
    mid = r1.astype(BF16)
    lo = (r1 - mid.astype(F32)).astype(BF16)
    return hi, mid, lo


def _sigmoid(x):
    return 1.0 / (1.0 + jnp.exp(-x))


def _softplus(x):
    return jnp.maximum(x, 0.0) + jnp.log(1.0 + jnp.exp(-jnp.abs(x)))


def _ffn_kernel(*refs, mixer, ncast):
    nin = len(refs) - 1 - 2 * ncast
    cast_in, o_ref, cast_out = refs[nin:nin + ncast], refs[nin + ncast], refs[nin + ncast + 1:]
    if mixer is None:
        x_ref, gpre_ref, wgu_ref, wd_ref, gpost_ref = refs[:nin]
    else:
        h_ref, wo_ref, gmix_ref, x_ref, gpre_ref, wgu_ref, wd_ref, gpost_ref = refs[:nin]
    for src, dst in zip(cast_in, cast_out):
        dst[...] = src[...].astype(dst.dtype)
    f = wd_ref.shape[0]
    half = x_ref.shape[0] // 2

    def rows(lo):
        x = x_ref[lo:lo + half, :]
        if mixer is not None:
            if mixer == "pairs":
                h = jnp.concatenate([h_ref[p, lo:lo + half, :] for p in range(h_ref.shape[0])], axis=-1)
            else:
                h = h_ref[lo:lo + half, :]
            y = _dot(h, wo_ref[...])
            yield
            x = x + _rms(y, gmix_ref[...])
        xn = _rms(x, gpre_ref[...]).astype(BF16)
        yield
        gate = _dot(xn, wgu_ref[:, :f])
        up = _dot(xn, wgu_ref[:, f:])
        yield
        act = (gate * _sigmoid(gate) * up).astype(BF16)
        h2 = _dot(act, wd_ref[...])
        yield
        o_ref[lo:lo + half, :] = x + HALF_STEP * _rms(h2, gpost_ref[...])

    _lockstep(rows(lo) for lo in (0, half))


def _resident(shape):
    return pl.BlockSpec(shape, lambda *_: (0,) * len(shape), pipeline_mode=pl.Buffered(1))


def _cast_tiling(rows, steps):
    nblocks = steps
    while nblocks > 1 and (rows % nblocks or (rows // nblocks) % (2 * SUBLANES)):
        nblocks //= 2
    return rows // nblocks, steps // nblocks


def ffn_sublayer(x, g_pre, w_gu, w_down, g_post, mix=None, cast_next=(), *, tm=512):
    n, d = x.shape
    tm = min(tm, n)
    steps = n // tm
    ffn_args = (x, g_pre, w_gu, w_down, g_post)
    ffn_specs = [pl.BlockSpec((tm, d), lambda i: (i, 0)), _resident(g_pre.shape), _resident(w_gu.shape),
                 _resident(w_down.shape), _resident(g_post.shape)]
    mixer, mix_args, mix_specs = None, (), []
    if mix is not None:
        h, w_out, g_mix = mix
        if h.ndim == 3:
            mixer = "pairs"
            h_spec = pl.BlockSpec((h.shape[0], tm, h.shape[2]), lambda i: (0, i, 0))
        else:
            mixer = "rows"
            h_spec = pl.BlockSpec((tm, h.shape[1]), lambda i: (i, 0))
        mix_args = (h, w_out, g_mix)
        mix_specs = [h_spec, _resident(w_out.shape), _resident(g_mix.shape)]
    cast_specs = []
    for w in cast_next:
        rb, per = _cast_tiling(w.shape[0], steps)
        cast_specs.append(pl.BlockSpec((rb, w.shape[1]), lambda i, per=per: (i // per, 0)))
    outs = pl.pallas_call(
        functools.partial(_ffn_kernel, mixer=mixer, ncast=len(cast_next)),
        grid=(steps,),
        in_specs=mix_specs + ffn_specs + cast_specs,
        out_specs=[pl.BlockSpec((tm, d), lambda i: (i, 0))] + cast_specs,
        out_shape=[jax.ShapeDtypeStruct((n, d), F32)] + [jax.ShapeDtypeStruct(w.shape, BF16) for w in cast_next],
        compiler_params=_cparams(("arbitrary",) if cast_next else ("parallel",)),
        name="ffn" if mix is None else "mix_ffn",
    )(*mix_args, *ffn_args, *cast_next)
    return outs


def _norm_matmul_kernel(x_ref, g_ref, w_ref, o_ref):
    half = x_ref.shape[0] // 2

    def rows(lo):
        xn = _rms(x_ref[lo:lo + half, :], g_ref[...]).astype(BF16)
        yield
        o_ref[lo:lo + half, :] = _dot(xn, w_ref[...]).astype(o_ref.dtype)

    _lockstep(rows(lo) for lo in (0, half))


def norm_matmul(x, g, w, out_dtype, *, tm=512):
    n, d = x.shape
    nout = w.shape[1]
    tm = min(tm, n)
    return pl.pallas_call(
        _norm_matmul_kernel,
        grid=(n // tm,),
        in_specs=[
            pl.BlockSpec((tm, d), lambda i: (i, 0)),
            _resident(g.shape),
            _resident(w.shape),
        ],
        out_specs=pl.BlockSpec((tm, nout), lambda i: (i, 0)),
        out_shape=jax.ShapeDtypeStruct((n, nout), out_dtype),
        compiler_params=_cparams(("parallel",)),
        name="norm_matmul",
    )(x, g, w)


def _mlstm_proj_kernel(x_ref, g_ref, w_ref, wif_ref, bif_ref, proj_ref, gates_ref):
    H = ML_HEADS
    half = x_ref.shape[0] // 2

    def rows(lo):
        xn = _rms(x_ref[lo:lo + half, :], g_ref[...]).astype(BF16)
        yield
        proj_ref[lo:lo + half, :] = _dot(xn, w_ref[...]).astype(proj_ref.dtype)
        pre = _dot_nt(wif_ref[...], xn) + bif_ref[...]
        pre = ML_GATE_CAP * jnp.tanh(pre * (1.0 / ML_GATE_CAP))
        is_input_gate = lax.broadcasted_iota(jnp.int32, pre.shape, 0) < H
        gates_ref[:, lo:lo + half] = jnp.where(is_input_gate, pre, -_softplus(-pre))

    _lockstep(rows(lo) for lo in (0, half))


def mlstm_proj(x, g, w_main, w_if_t, b_if, *, tm=512):
    n, d = x.shape
    tm = min(tm, n)
    nout = w_main.shape[1]
    ng = w_if_t.shape[0]
    return pl.pallas_call(
        _mlstm_proj_kernel,
        grid=(n // tm,),
        in_specs=[pl.BlockSpec((tm, d), lambda i: (i, 0)), _resident(g.shape), _resident(w_main.shape),
                  _resident(w_if_t.shape), _resident(b_if.shape)],
        out_specs=[pl.BlockSpec((tm, nout), lambda i: (i, 0)), pl.BlockSpec((ng, tm), lambda i: (0, i))],
        out_shape=[jax.ShapeDtypeStruct((n, nout), BF16), jax.ShapeDtypeStruct((ng, n), F32)],
        compiler_params=_cparams(("parallel",)),
        name="mlstm_proj",
    )(x, g, w_main, w_if_t, b_if)


def _mlstm_kernel(qk_ref, v_ref, o_ref, gates_ref, cw_ref, ng_ref, out_ref,
                  c_ref, n_ref, m_ref, tq_ref, tk_ref):
    L, H, dk, dv = ML_CHUNK, ML_HEADS, ML_DQK, ML_DV
    R = qk_ref.shape[0]
    nc = qk_ref.shape[1] // L
    KW = cw_ref.shape[0]
    TAIL = SUBLANES

    @pl.when(pl.program_id(1) == 0)
    def _():
        for ref in (c_ref, n_ref, m_ref, tq_ref, tk_ref):
            ref[...] = jnp.zeros_like(ref)

    row = lax.broadcasted_iota(jnp.int32, (L, L), 0)
    col = lax.broadcasted_iota(jnp.int32, (L, L), 1)
    tri = row >= col
    eye = row == col
    ones = jnp.ones((L, L), BF16)

    def conv_silu(x, tail, w):
        ext = jnp.concatenate([tail, x], axis=0)
        acc = x * w[KW - 1:KW, :]
        for i in range(KW - 1):
            off = TAIL - (KW - 1) + i
            acc = acc + ext[off:off + L, :] * w[i:i + 1, :]
        return acc * _sigmoid(acc)

    def step(r, hd, c, sl, m, tq, tk):
        ci = r * H + hd
        xq = qk_ref[r, sl, hd * dk:(hd + 1) * dk].astype(F32)
        xk = qk_ref[r, sl, (H + hd) * dk:(H + hd + 1) * dk].astype(F32)
        qc = (conv_silu(xq, tq, cw_ref[:, hd * dk:(hd + 1) * dk]) * (dk ** -0.5)).astype(BF16)
        kc = conv_silu(xk, tk, cw_ref[:, (H + hd) * dk:(H + hd + 1) * dk]).astype(BF16)
        vc = v_ref[r, sl, hd * dv:(hd + 1) * dv]
        irow = gates_ref[r, hd:hd + 1, sl]
        frow = gates_ref[r, H + hd:H + hd + 1, sl]
        fh, fm, fl = _split3(jnp.where(tri, frow, 0.0))
        bmat = _dot(fh, ones) + _dot(fm, ones) + _dot(fl, ones)
        brow = _dot_nt(ones, fh) + _dot_nt(ones, fm) + _dot_nt(ones, fl)
        qk = _dot_nt(qc, kc)
        cmat = c_ref[ci]
        nrow = n_ref[ci]
        yield
        log_d = jnp.where(tri, bmat - brow + irow, -jnp.inf)
        bcol = bmat[:, 0:1]
        log_inter = bcol + m
        m_t = jnp.maximum(log_inter, jnp.max(log_d, axis=-1, keepdims=True))
        s = qk * jnp.exp(log_d - m_t)
        inter = jnp.exp(log_inter - m_t)
        b_end = bmat[L - 1:L, 0:1]
        log_w = b_end - brow[0:1, :] + irow
        m_new = jnp.maximum(b_end + m, jnp.max(log_w, axis=-1, keepdims=True))
        w_row = jnp.exp(log_w - m_new)
        carry_decay = jnp.exp(b_end + m - m_new)
        wdiag = jnp.where(eye, w_row, 0.0).astype(BF16)
        wv = _dot(wdiag, vc).astype(BF16)
        n_add = _dot(jnp.broadcast_to(w_row, (SUBLANES, L)).astype(BF16), kc)[0:1, :]
        num = _dot(s.astype(BF16), vc) + inter * _dot(qc, cmat.astype(BF16))
        yield
        c_new = carry_decay * cmat + _dot_tn(kc, wv)
        n_new = carry_decay * nrow + n_add
        qn = jnp.sum(qc.astype(F32) * nrow, axis=-1, keepdims=True)
        den = jnp.sum(s, axis=-1, keepdims=True) + inter * qn
        h = num / jnp.maximum(jnp.abs(den), jnp.exp(-m_t))
        ng = ng_ref[:, hd * dv:(hd + 1) * dv]
        hn = h * lax.rsqrt(jnp.mean(h * h, axis=-1, keepdims=True) + NORM_EPS) * ng
        og = o_ref[r, sl, hd * dv:(hd + 1) * dv].astype(F32)
        out = (hn * _sigmoid(og)).astype(out_ref.dtype)
        return (m_new, xq[L - TAIL:, :], xk[L - TAIL:, :]), c_new, n_new, out

    chains = [(r, hd) for r in range(R) for hd in range(H)]

    def body(c, carries):
        sl = pl.ds(pl.multiple_of(c * L, L), L)
        res = _lockstep(step(r, hd, c, sl, *carries[ci]) for ci, (r, hd) in enumerate(chains))
        for ci, (r, hd) in enumerate(chains):
            _, c_ref[ci], n_ref[ci], out_ref[r, sl, hd * dv:(hd + 1) * dv] = res[ci]
        return tuple(x[0] for x in res)

    init = tuple((m_ref[ci][0:1, 0:1], tq_ref[ci], tk_ref[ci]) for ci in range(len(chains)))
    final = lax.fori_loop(0, nc, body, init)
    for ci, (m, tq, tk) in enumerate(final):
        m_ref[ci] = jnp.broadcast_to(m, m_ref.shape[1:])
        tq_ref[ci] = tq
        tk_ref[ci] = tk


def mlstm_scan(proj, gates, conv_w, ng):
    b, t, _ = proj.shape
    h, dk, dv, L = ML_HEADS, ML_DQK, ML_DV, ML_CHUNK
    rows = min(ML_ROWS, b)
    tb = min(ML_TIME_BLOCK, t)
    wqk, wv = 2 * h * dk, h * dv
    nchains = rows * h
    seq = lambda w, col: pl.BlockSpec((rows, tb, w), lambda i, j: (i, j, col))
    return pl.pallas_call(
        _mlstm_kernel,
        grid=(b // rows, t // tb),
        in_specs=[seq(wqk, 0), seq(wv, wqk // wv), seq(wv, wqk // wv + 1),
                  pl.BlockSpec((rows, 2 * h, tb), lambda i, j: (i, 0, j)),
                  _resident(conv_w.shape), _resident(ng.shape)],
        out_specs=pl.BlockSpec((rows, tb, wv), lambda i, j: (i, j, 0)),
        out_shape=jax.ShapeDtypeStruct((b, t, wv), BF16),
        scratch_shapes=[pltpu.VMEM((nchains, dk, dv), F32), pltpu.VMEM((nchains, 1, dk), F32),
                        pltpu.VMEM((nchains, SUBLANES, LANES), F32), pltpu.VMEM((nchains, SUBLANES, dk), F32),
                        pltpu.VMEM((nchains, SUBLANES, dk), F32)],
        compiler_params=_cparams(("parallel", "arbitrary")),
        name="mlstm_scan",
    )(proj, proj, proj, gates, conv_w, ng)


def mlstm_mixer(x, g_pre, w_in, b_if, conv_w, norm_g, b, t):
    h, dk, dv, L = ML_HEADS, ML_DQK, ML_DV, ML_CHUNK
    n_main = 2 * h * dk + 2 * h * dv
    w_main = w_in[:, :n_main].astype(BF16)
    w_if_t = w_in[:, n_main:].T.astype(BF16)
    proj, gates = mlstm_proj(x, g_pre, w_main, w_if_t, b_if.reshape(2 * h, 1))
    gates = jnp.transpose(gates.reshape(2 * h, b, t), (1, 0, 2))
    hs = mlstm_scan(proj.reshape(b, t, n_main), gates, conv_w, norm_g[None, :])
    return hs.reshape(b * t, h * dv)


def _retention_kernel(lg_ref, q_ref, k_ref, v_ref, g_ref, cos_ref, sin_ref, out_ref, r_ref):
    L, H, dk, dv = RT_CHUNK, RT_HEADS, RT_DK, RT_DV
    half = dk // 2
    nc = q_ref.shape[1] // L

    @pl.when(pl.program_id(1) == 0)
    def _():
        r_ref[...] = jnp.zeros_like(r_ref)

    row = lax.broadcasted_iota(jnp.int32, (L, L), 0)
    col = lax.broadcasted_iota(jnp.int32, (L, L), 1)
    diff = jnp.maximum(row - col, 0).astype(F32)
    idx = lax.broadcasted_iota(jnp.int32, (L, 1), 0).astype(F32)

    def rotate(u, cos, sin):
        u1, u2 = u[:, :half], u[:, half:]
        return jnp.concatenate([u1 * cos - u2 * sin, u1 * sin + u2 * cos], axis=-1)

    def step(hd, sl, cos, sin):
        log_gamma = lg_ref[hd]
        qr = rotate(q_ref[0, sl, hd * dk:(hd + 1) * dk].astype(F32), cos, sin)
        kr = rotate(k_ref[0, sl, hd * dk:(hd + 1) * dk].astype(F32), cos, sin) * (dk ** -0.5)
        vc = v_ref[0, sl, hd * dv:(hd + 1) * dv]
        qb = qr.astype(BF16)
        qk = _dot_nt(qb, kr.astype(BF16))
        rmat = r_ref[hd]
        k_decay = jnp.exp(log_gamma * (L - 1.0 - idx))
        r_add = _dot_tn((kr * k_decay).astype(BF16), vc)
        cross = _dot(qb, rmat.astype(BF16))
        yield
        d_mask = jnp.where(row >= col, jnp.exp(log_gamma * diff), 0.0)
        q_decay = jnp.exp(log_gamma * (idx + 1.0))
        y = _dot((qk * d_mask).astype(BF16), vc) + cross * q_decay
        yield
        r_new = jnp.exp(jnp.full((1, 1), L, F32) * log_gamma) * rmat + r_add
        yn = y * lax.rsqrt(jnp.mean(y * y, axis=-1, keepdims=True) + NORM_EPS)
        gc = g_ref[0, sl, hd * dv:(hd + 1) * dv].astype(F32)
        return r_new, (gc * _sigmoid(gc) * yn).astype(out_ref.dtype)

    def body(c, carry):
        sl = pl.ds(pl.multiple_of(c * L, L), L)
        cos = cos_ref[0, sl, :]
        sin = sin_ref[0, sl, :]
        res = _lockstep(step(hd, sl, cos, sin) for hd in range(H))
        for hd in range(H):
            r_ref[hd], out_ref[0, sl, hd * dv:(hd + 1) * dv] = res[hd]
        return carry

    lax.fori_loop(0, nc, body, 0)


def retention_scan(proj, cos, sin, log_gamma):
    b, t, _ = proj.shape
    h, dk, dv = RT_HEADS, RT_DK, RT_DV
    tb = min(RT_TIME_BLOCK, t)
    wq, wv = h * dk, h * dv
    seq = lambda w, col: pl.BlockSpec((1, tb, w), lambda i, j: (i, j, col))
    return pl.pallas_call(
        _retention_kernel,
        grid=(b, t // tb),
        in_specs=[pl.BlockSpec(memory_space=pltpu.SMEM),
                  seq(wq, 0), seq(wq, 1), seq(wv, 2 * wq // wv), seq(wv, 2 * wq // wv + 1),
                  seq(dk // 2, 0), seq(dk // 2, 0)],
        out_specs=seq(wv, 0),
        out_shape=jax.ShapeDtypeStruct((b, t, wv), BF16),
        scratch_shapes=[pltpu.VMEM((h, dk, dv), F32)],
        compiler_params=_cparams(("parallel", "arbitrary")),
        name="retention_scan",
    )(log_gamma, proj, proj, proj, proj, cos, sin)


def retention_mixer(x, positions, g_pre, w_in, b, t):
    h, dk, dv = RT_HEADS, RT_DK, RT_DV
    proj = norm_matmul(x, g_pre, w_in.astype(BF16), BF16).reshape(b, t, -1)
    inv_freq = 1.0 / (RT_ROPE_BASE ** jnp.linspace(0.0, 1.0, dk // 2, dtype=F32))
    ang = positions.astype(F32)[:, :, None] * inv_freq
    log_gamma = jnp.log(1.0 - 2.0 ** (-5.0 - jnp.arange(h, dtype=F32)))
    y = retention_scan(proj, jnp.cos(ang), jnp.sin(ang), log_gamma)
    return y.reshape(b * t, h * dv)


def _rwkv_proj_kernel(x_ref, xp_ref, g_ref, mu_ref, wrkv_ref, w0_ref, w1_ref, w2_ref,
                      a0_ref, a1_ref, a2_ref, g1_ref, g2_ref,
                      r_ref, k_ref, v_ref, gc_ref, a_ref, gate_ref, *, seq_tiles):
    i = pl.program_id(0)
    tm = x_ref.shape[0]
    g = g_ref[...]
    h = _rms(x_ref[...], g)
    prev = _rms(xp_ref[...], g)[SUBLANES - 1:SUBLANES, :]
    prev = prev * (i % seq_tiles != 0).astype(F32)
    rid = lax.broadcasted_iota(jnp.int32, (tm, 1), 0)
    h_prev = jnp.where(rid == 0, prev, pltpu.roll(h, 1, axis=0))
    span = min(tm, MXU_DEPTH)
    rr = lax.broadcasted_iota(jnp.int32, (span, span), 0)
    cc = lax.broadcasted_iota(jnp.int32, (span, span), 1)
    chunk_tril = ((rr // RW_CHUNK == cc // RW_CHUNK) & (rr >= cc)).astype(BF16)

    def rows(lo):
        hh = h[lo:lo + span, :]
        xx = h_prev[lo:lo + span, :] - hh
        mix = lambda j: (hh + xx * mu_ref[j:j + 1, :]).astype(BF16)

        def put(ref, val):
            for p in range(ref.shape[0]):
                ref[p, lo:lo + span, :] = val[:, p * LANES:(p + 1) * LANES].astype(ref.dtype)

        put(r_ref, _dot(mix(0), wrkv_ref[0]))
        yield
        put(k_ref, _dot(mix(1), wrkv_ref[1]))
        yield
        put(v_ref, _dot(mix(2), wrkv_ref[2]))
        yield
        wl = jnp.tanh(_dot(mix(3), w1_ref[...])).astype(BF16)
        al = _dot(mix(4), a1_ref[...]).astype(BF16)
        gl = _sigmoid(_dot(mix(5), g1_ref[...])).astype(BF16)
        yield
        w_log = -_softplus(-(w0_ref[...] + _dot(wl, w2_ref[...]))) - 0.5
        put(a_ref, _sigmoid(a0_ref[...] + _dot(al, a2_ref[...])))
        put(gate_ref, _dot(gl, g2_ref[...]))
        yield
        lh, lm, ll = _split3(-jnp.exp(w_log))
        put(gc_ref, _dot(chunk_tril, lh) + _dot(chunk_tril, lm) + _dot(chunk_tril, ll))

    _lockstep(rows(lo) for lo in range(0, tm, span))


def rwkv_proj(x, g_pre, mu, w_rkv, w0, w1, w2, a0, a1, a2, g1, g2, t, *, tm=512):
    n, d = x.shape
    tm = min(tm, t)
    p = d // LANES
    outs = pl.BlockSpec((p, tm, LANES), lambda i: (0, i, 0))
    args = (g_pre, mu, w_rkv, w0, w1, w2, a0, a1, a2, g1, g2)
    shp = lambda dt: jax.ShapeDtypeStruct((p, n, LANES), dt)
    return pl.pallas_call(
        functools.partial(_rwkv_proj_kernel, seq_tiles=t // tm),
        grid=(n // tm,),
        in_specs=[
            pl.BlockSpec((tm, d), lambda i: (i, 0)),
            pl.BlockSpec((SUBLANES, d), lambda i: (jnp.maximum(i * (tm // SUBLANES) - 1, 0), 0)),
        ] + [_resident(a.shape) for a in args],
        out_specs=[outs] * 6,
        out_shape=[shp(BF16), shp(BF16), shp(BF16), shp(F32), shp(BF16), shp(BF16)],
        compiler_params=_cparams(("parallel",)),
        name="rwkv_proj",
    )(x, x, *args)


def _rwkv_scan_kernel(r_ref, k_ref, v_ref, gc_ref, a_ref, gate_ref, kk_ref, ka_ref, rk_ref,
                      lng_ref, lnb_ref, out_ref, s_ref):
    C = RW_CHUNK
    N = RW_HEAD
    C2 = 2 * C
    nc = r_ref.shape[2] // C

    @pl.when(pl.program_id(1) == 0)
    def _():
        s_ref[...] = jnp.zeros_like(s_ref)

    lane = lax.broadcasted_iota(jnp.int32, (1, LANES), 1)
    in_a = lane < N
    m_a = in_a.astype(F32)
    m_b = 1.0 - m_a
    rr = lax.broadcasted_iota(jnp.int32, (C2, C2), 0)
    cc = lax.broadcasted_iota(jnp.int32, (C2, C2), 1)
    same = (rr // C) == (cc // C)
    strict = same & ((rr % C) > (cc % C))
    incl = same & ((rr % C) >= (cc % C))
    eye2 = (rr == cc).astype(F32)
    level_masks = []
    for lvl in range(int(math.log2(C))):
        rb, cb = (rr % C) >> lvl, (cc % C) >> lvl
        level_masks.append(same & ((rb & 1) == 1) & (cb == rb - 1))
    incl2 = jnp.concatenate([incl, incl], axis=1)
    first_row = lax.broadcasted_iota(jnp.int32, (C, 1), 0) == 0

    def seg_sum(z):
        sa = jnp.sum(z * m_a, axis=-1, keepdims=True)
        sb = jnp.sum(z * m_b, axis=-1, keepdims=True)
        return jnp.where(in_a, sa, sb)

    def stack(z):
        return jnp.concatenate([z * m_a, z * m_b], axis=0)

    def load(row, gi, sl):
        f32 = lambda ref: ref[gi, row, sl, :].astype(F32)
        return (f32(r_ref), f32(k_ref), f32(v_ref), gc_ref[gi, row, sl, :], f32(a_ref), f32(gate_ref),
                s_ref[row, gi], kk_ref[gi], ka_ref[gi], rk_ref[gi], lng_ref[gi], lnb_ref[gi])

    def step(r, k, v, gcum, a_sig, gate, st0, kk_w, ka_w, rk_w, ln_g, ln_b):
        kk = k * kk_w
        kk = kk / jnp.maximum(jnp.sqrt(seg_sum(kk * kk)), 1e-12)
        k = k * (1.0 + (a_sig - 1.0) * ka_w)
        a_ = -kk
        b_ = kk * a_sig
        g_end = gcum[C - 1:C, :]
        g_excl = jnp.where(first_row, 0.0, pltpu.roll(gcum, 1, axis=0))
        e_neg = jnp.exp(-gcum)
        e_rest = jnp.exp(g_end - gcum)
        a_h = stack(a_ * jnp.exp(g_excl))
        r_h = stack(r * jnp.exp(gcum))
        b_t = stack(b_ * e_rest)
        k_t = stack(k * e_rest)
        twice = lambda z: jnp.concatenate([z, z], axis=0)
        v_s = twice(v).astype(BF16)
        lhs = jnp.concatenate([a_h, r_h], axis=0).astype(BF16)
        rhs = jnp.concatenate([twice(b_ * e_neg), twice(k * e_neg)], axis=0).astype(BF16)
        sc = _dot_nt(lhs, rhs)
        yield
        a_ab = jnp.where(strict, sc[:C2, :C2], 0.0)
        a_ak = jnp.where(strict, sc[:C2, C2:], 0.0)
        a_r = jnp.where(incl2, sc[C2:, :], 0.0).astype(BF16)
        w1 = _dot(a_ak.astype(BF16), v_s)
        tinv = eye2 + jnp.where(level_masks[0], a_ab, 0.0)
        for mk in level_masks[1:]:
            tb = tinv.astype(BF16)
            l21 = jnp.where(mk, a_ab, 0.0).astype(BF16)
            tl = _dot(tb, l21)
            yield
            tinv = tinv + _dot(tl.astype(BF16), tb)
            yield
        au = _dot(tinv.astype(BF16), jnp.concatenate([a_h, w1], axis=1).astype(BF16))
        yield
        amat = jnp.concatenate([au.astype(BF16),
                                jnp.concatenate([jnp.zeros((C2, LANES), BF16), v_s], axis=1)], axis=0)
        pz = _dot_tn(jnp.concatenate([b_t, k_t], axis=0).astype(BF16), amat)
        ry = _dot(a_r, amat)
        yield
        r_bar = r_h + ry[:, :LANES]
        sy = _dot(jnp.concatenate([r_bar, pz[:, :LANES]], axis=0).astype(BF16), st0.astype(BF16))
        yield
        st_new = st0 * jnp.broadcast_to(jnp.exp(g_end), (LANES, LANES)).T + sy[C2:, :] + pz[:, LANES:]
        ys = sy[:C2, :] + ry[:, LANES:]
        y = jnp.where(in_a, ys[:C, :], ys[C:, :])
        mean = seg_sum(y) * (1.0 / N)
        yc = y - mean
        var = seg_sum(yc * yc) * (1.0 / N)
        yn = yc * lax.rsqrt(var + RW_LN_EPS) * ln_g + ln_b
        bonus = seg_sum(r * k * rk_w) * v
        return st_new, ((yn + bonus) * gate).astype(out_ref.dtype)

    def body(c, carry):
        sl = pl.ds(pl.multiple_of(c * C, C), C)
        chains = [(row, gi) for row in range(r_ref.shape[1]) for gi in range(r_ref.shape[0])]
        results = _lockstep(step(*load(row, gi, sl)) for row, gi in chains)
        for (row, gi), res in zip(chains, results):
            s_ref[row, gi], out_ref[gi, row, sl, :] = res
        return carry

    lax.fori_loop(0, nc, body, 0)


def rwkv_scan(r, k, v, gcum, a, gate, k_k, k_a, r_k, ln_g, ln_b, b, t):
    p, n, l = r.shape
    tb = min(RW_TIME_BLOCK, t)
    rows = min(RW_ROWS, b)
    seq = lambda: pl.BlockSpec((p, rows, tb, l), lambda i, j: (0, i, j, 0))
    par = lambda: pl.BlockSpec((p, 1, l), lambda i, j: (0, 0, 0))
    view = lambda a: a.reshape(p, b, t, l)
    y = pl.pallas_call(
        _rwkv_scan_kernel,
        grid=(b // rows, t // tb),
        in_specs=[seq()] * 6 + [par()] * 5,
        out_specs=seq(),
        out_shape=jax.ShapeDtypeStruct((p, b, t, l), BF16),
        scratch_shapes=[pltpu.VMEM((rows, p, l, l), F32)],
        compiler_params=_cparams(("parallel", "arbitrary")),
        name="rwkv_scan",
    )(view(r), view(k), view(v), view(gcum), view(a), view(gate), k_k, k_a, r_k, ln_g, ln_b)
    return y.reshape(p, n, l)


def rwkv_mixer(x, g_pre, mu, w_rkv, w0, w1, w2, a0, a1, a2, g1, g2,
               k_k, k_a, r_k, ln_g, ln_b, b, t):
    bf = lambda a: a.astype(BF16)
    row = lambda a: a.reshape(1, -1)
    r, k, v, gcum, a, gate = rwkv_proj(x, g_pre, mu, bf(w_rkv), row(w0), bf(w1), bf(w2),
                                     row(a0), bf(a1), bf(a2), bf(g1), bf(g2), t)
    par = lambda a: a.reshape(RW_PAIRS, 1, LANES)
    return rwkv_scan(r, k, v, gcum, a, gate, par(k_k), par(k_a), par(r_k), par(ln_g), par(ln_b), b, t)


def kernel(x, positions, norm_g, ffn_w_gu, ffn_w_down, ml_w_in, ml_b_if, ml_conv_w, ml_norm_g, ml_w_out,
           rw_mu, rw_w_rkv, rw_w0, rw_w1, rw_w2, rw_a0, rw_a1, rw_a2, rw_g1, rw_g2,
           rw_k_k, rw_k_a, rw_r_k, rw_ln_g, rw_ln_b, rw_w_out, rt_w_in, rt_w_out):
    b, t, d = x.shape
    depth = norm_g.shape[0]
    xf = x.reshape(b * t, d)
    ffn_weights = [(ffn_w_gu[layer, i], ffn_w_down[layer, i]) for layer in range(depth) for i in range(2)]
    w_gu, w_down = (w.astype(BF16) for w in ffn_weights[0])

    def ffn(k, xf, g_pre, g_post, mix=None):
        nxt = ()
        if k + 1 < len(ffn_weights):
            nxt = (ffn_weights[k + 1][0], ffn_weights[k + 1][1].reshape(-1, ffn_weights[k + 1][0].shape[1]))
        outs = ffn_sublayer(xf, g_pre, w_gu, w_down, g_post, mix=mix, cast_next=nxt)
        if nxt:
            return outs[0], outs[1], outs[2].reshape(ffn_weights[k + 1][1].shape)
        return outs[0], None, None

    for layer in range(depth):
        g = norm_g[layer][:, None, :]
        xf, w_gu, w_down = ffn(2 * layer, xf, g[0], g[1])
        kind, j = layer % 3, layer // 3
        if kind == 0:
            h = mlstm_mixer(xf, g[2], ml_w_in[j], ml_b_if[j], ml_conv_w[j], ml_norm_g[j], b, t)
            w_out = ml_w_out[j]
        elif kind == 1:
            h = rwkv_mixer(xf, g[2], rw_mu[j], rw_w_rkv[j], rw_w0[j], rw_w1[j], rw_w2[j],
                           rw_a0[j], rw_a1[j], rw_a2[j], rw_g1[j], rw_g2[j],
                           rw_k_k[j], rw_k_a[j], rw_r_k[j].reshape(-1), rw_ln_g[j], rw_ln_b[j], b, t)
            w_out = rw_w_out[j]
        else:
            h = retention_mixer(xf, positions, g[2], rt_w_in[j], b, t)
            w_out = rt_w_out[j]
        xf, w_gu, w_down = ffn(2 * layer + 1, xf, g[4], g[5], mix=(h, w_out.astype(BF16), g[3]))
    return xf.reshape(b, t, d)
```

```python
import functools
import math

import jax
import jax.numpy as jnp
from jax import lax
from jax.experimental import pallas as pl
from jax.experimental.pallas import tpu as pltpu

F32 = jnp.float32
BF16 = jnp.bfloat16

D_MODEL = 1024
NORM_EPS = 1e-6
HALF_STEP = 0.5

ML_HEADS = 4
ML_DQK = 128
ML_DV = 256
ML_CHUNK = 256
ML_GATE_CAP = 15.0
ML_ROWS = 2
ML_TIME_BLOCK = 1024

RW_HEAD = 64
RW_PAIRS = D_MODEL // (2 * RW_HEAD)
RW_CHUNK = 64
RW_TIME_BLOCK = 256
RW_ROWS = 2
RW_LN_EPS = 64e-5

RT_HEADS = 4
RT_DK = 256
RT_DV = 512
RT_CHUNK = 256
RT_TIME_BLOCK = 512
RT_ROPE_BASE = 10000.0

LANES = 128
SUBLANES = 8
MXU_DEPTH = 256
VMEM_LIMIT = 56 * 1024 * 1024


def _cparams(sem):
    return pltpu.CompilerParams(dimension_semantics=sem, vmem_limit_bytes=VMEM_LIMIT)


def _lockstep(gens):
    gens = list(gens)
    results = [None] * len(gens)
    live = list(range(len(gens)))
    while live:
        for i in list(live):
            try:
                next(gens[i])
            except StopIteration as done:
                results[i] = done.value
                live.remove(i)
    return results


def _rms(x, g):
    return x * lax.rsqrt(jnp.mean(x * x, axis=-1, keepdims=True) + NORM_EPS) * g


def _dot(a, b):
    return jnp.dot(a, b, preferred_element_type=F32)


def _dot_nt(a, b):
    return lax.dot_general(a, b, (((1,), (1,)), ((), ())), preferred_element_type=F32)


def _dot_tn(a, b):
    return lax.dot_general(a, b, (((0,), (0,)), ((), ())), preferred_element_type=F32)


def _split3(x):
    hi = x.astype(BF16)
    r1 = x - hi.astype(F32)
    mid = r1.astype(BF16)
    lo = (r1 - mid.astype(F32)).astype(BF16)
    return hi, mid, lo


def _sigmoid(x):
    return 1.0 / (1.0 + jnp.exp(-x))


def _softplus(x):
    return jnp.maximum(x, 0.0) + jnp.log(1.0 + jnp.exp(-jnp.abs(x)))


def _ffn_kernel(*refs, mixer, ncast):
    nin = len(refs) - 1 - 2 * ncast
    cast_in, o_ref, cast_out = refs[nin:nin + ncast], refs[nin + ncast], refs[nin + ncast + 1:]
    if mixer is None:
        x_ref, gpre_ref, wgu_ref, wd_ref, gpost_ref = refs[:nin]
    else:
        h_ref, wo_ref, gmix_ref, x_ref, gpre_ref, wgu_ref, wd_ref, gpost_ref = refs[:nin]
    for src, dst in zip(cast_in, cast_out):
        dst[...] = src[...].astype(dst.dtype)
    f = wd_ref.shape[0]
    half = x_ref.shape[0] // 2

    def rows(lo):
        x = x_ref[lo:lo + half, :]
        if mixer is not None:
            if mixer == "pairs":
                h = jnp.concatenate([h_ref[p, lo:lo + half, :] for p in range(h_ref.shape[0])], axis=-1)
            else:
                h = h_ref[lo:lo + half, :]
            y = _dot(h, wo_ref[...])
            yield
            x = x + _rms(y, gmix_ref[...])
        xn = _rms(x, gpre_ref[...]).astype(BF16)
        yield
        gate = _dot(xn, wgu_ref[:, :f])
        up = _dot(xn, wgu_ref[:, f:])
        yield
        act = (gate * _sigmoid(gate) * up).astype(BF16)
        h2 = _dot(act, wd_ref[...])
        yield
        o_ref[lo:lo + half, :] = x + HALF_STEP * _rms(h2, gpost_ref[...])

    _lockstep(rows(lo) for lo in (0, half))


def _resident(shape):
    return pl.BlockSpec(shape, lambda *_: (0,) * len(shape), pipeline_mode=pl.Buffered(1))


def _cast_tiling(rows, steps):
    nblocks = steps
    while nblocks > 1 and (rows % nblocks or (rows // nblocks) % (2 * SUBLANES)):
        nblocks //= 2
    return rows // nblocks, steps // nblocks


def ffn_sublayer(x, g_pre, w_gu, w_down, g_post, mix=None, cast_next=(), *, tm=512):
    n, d = x.shape
    tm = min(tm, n)
    steps = n // tm
    ffn_args = (x, g_pre, w_gu, w_down, g_post)
    ffn_specs = [pl.BlockSpec((tm, d), lambda i: (i, 0)), _resident(g_pre.shape), _resident(w_gu.shape),
                 _resident(w_down.shape), _resident(g_post.shape)]
    mixer, mix_args, mix_specs = None, (), []
    if mix is not None:
        h, w_out, g_mix = mix
        if h.ndim == 3:
            mixer = "pairs"
            h_spec = pl.BlockSpec((h.shape[0], tm, h.shape[2]), lambda i: (0, i, 0))
        else:
            mixer = "rows"
            h_spec = pl.BlockSpec((tm, h.shape[1]), lambda i: (i, 0))
        mix_args = (h, w_out, g_mix)
        mix_specs = [h_spec, _resident(w_out.shape), _resident(g_mix.shape)]
    cast_in_specs, cast_out_specs, cast_shapes = [], [], []
    for w, (layer, idx) in cast_next:
        rows, cols = w.shape[2:]
        rb, per = _cast_tiling(rows, steps)
        cast_in_specs.append(pl.BlockSpec((None, None, rb, cols),
                                          lambda i, per=per, layer=layer, idx=idx: (layer, idx, i // per, 0)))
        cast_out_specs.append(pl.BlockSpec((rb, cols), lambda i, per=per: (i // per, 0)))
        cast_shapes.append(jax.ShapeDtypeStruct((rows, cols), BF16))
    outs = pl.pallas_call(
        functools.partial(_ffn_kernel, mixer=mixer, ncast=len(cast_next)),
        grid=(steps,),
        in_specs=mix_specs + ffn_specs + cast_in_specs,
        out_specs=[pl.BlockSpec((tm, d), lambda i: (i, 0))] + cast_out_specs,
        out_shape=[jax.ShapeDtypeStruct((n, d), F32)] + cast_shapes,
        compiler_params=_cparams(("arbitrary",) if cast_next else ("parallel",)),
        name="ffn" if mix is None else "mix_ffn",
    )(*mix_args, *ffn_args, *[w for w, _ in cast_next])
    return outs


def _norm_matmul_kernel(x_ref, g_ref, w_ref, o_ref):
    half = x_ref.shape[0] // 2

    def rows(lo):
        xn = _rms(x_ref[lo:lo + half, :], g_ref[...]).astype(BF16)
        yield
        o_ref[lo:lo + half, :] = _dot(xn, w_ref[...]).astype(o_ref.dtype)

    _lockstep(rows(lo) for lo in (0, half))


def norm_matmul(x, g, w, out_dtype, *, tm=512):
    n, d = x.shape
    nout = w.shape[1]
    tm = min(tm, n)
    return pl.pallas_call(
        _norm_matmul_kernel,
        grid=(n // tm,),
        in_specs=[
            pl.BlockSpec((tm, d), lambda i: (i, 0)),
            _resident(g.shape),
            _resident(w.shape),
        ],
        out_specs=pl.BlockSpec((tm, nout), lambda i: (i, 0)),
        out_shape=jax.ShapeDtypeStruct((n, nout), out_dtype),
        compiler_params=_cparams(("parallel",)),
        name="norm_matmul",
    )(x, g, w)


def _mlstm_proj_kernel(x_ref, g_ref, w_ref, wif_ref, bif_ref, proj_ref, gates_ref):
    H = ML_HEADS
    half = x_ref.shape[0] // 2

    def rows(lo):
        xn = _rms(x_ref[lo:lo + half, :], g_ref[...]).astype(BF16)
        yield
        proj_ref[lo:lo + half, :] = _dot(xn, w_ref[...]).astype(proj_ref.dtype)
        pre = _dot_nt(wif_ref[...], xn) + bif_ref[...]
        pre = ML_GATE_CAP * jnp.tanh(pre * (1.0 / ML_GATE_CAP))
        is_input_gate = lax.broadcasted_iota(jnp.int32, pre.shape, 0) < H
        gates_ref[:, lo:lo + half] = jnp.where(is_input_gate, pre, -_softplus(-pre))

    _lockstep(rows(lo) for lo in (0, half))


def mlstm_proj(x, g, w_main, w_if_t, b_if, *, tm=512):
    n, d = x.shape
    tm = min(tm, n)
    nout = w_main.shape[1]
    ng = w_if_t.shape[0]
    return pl.pallas_call(
        _mlstm_proj_kernel,
        grid=(n // tm,),
        in_specs=[pl.BlockSpec((tm, d), lambda i: (i, 0)), _resident(g.shape), _resident(w_main.shape),
                  _resident(w_if_t.shape), _resident(b_if.shape)],
        out_specs=[pl.BlockSpec((tm, nout), lambda i: (i, 0)), pl.BlockSpec((ng, tm), lambda i: (0, i))],
        out_shape=[jax.ShapeDtypeStruct((n, nout), BF16), jax.ShapeDtypeStruct((ng, n), F32)],
        compiler_params=_cparams(("parallel",)),
        name="mlstm_proj",
    )(x, g, w_main, w_if_t, b_if)


def _mlstm_kernel(qk_ref, v_ref, o_ref, gates_ref, cw_ref, ng_ref, out_ref,
                  c_ref, n_ref, m_ref, tq_ref, tk_ref):
    L, H, dk, dv = ML_CHUNK, ML_HEADS, ML_DQK, ML_DV
    R = qk_ref.shape[0]
    nc = qk_ref.shape[1] // L
    KW = cw_ref.shape[0]
    TAIL = SUBLANES

    @pl.when(pl.program_id(1) == 0)
    def _():
        for ref in (c_ref, n_ref, m_ref, tq_ref, tk_ref):
            ref[...] = jnp.zeros_like(ref)

    row = lax.broadcasted_iota(jnp.int32, (L, L), 0)
    col = lax.broadcasted_iota(jnp.int32, (L, L), 1)
    tri = row >= col
    eye = row == col
    ones = jnp.ones((L, L), BF16)

    def conv_silu(x, tail, w):
        ext = jnp.concatenate([tail, x], axis=0)
        acc = x * w[KW - 1:KW, :]
        for i in range(KW - 1):
            off = TAIL - (KW - 1) + i
            acc = acc + ext[off:off + L, :] * w[i:i + 1, :]
        return acc * _sigmoid(acc)

    def step(r, hd, c, sl, m, tq, tk):
        ci = r * H + hd
        xq = qk_ref[r, sl, hd * dk:(hd + 1) * dk].astype(F32)
        xk = qk_ref[r, sl, (H + hd) * dk:(H + hd + 1) * dk].astype(F32)
        qc = (conv_silu(xq, tq, cw_ref[:, hd * dk:(hd + 1) * dk]) * (dk ** -0.5)).astype(BF16)
        kc = conv_silu(xk, tk, cw_ref[:, (H + hd) * dk:(H + hd + 1) * dk]).astype(BF16)
        vc = v_ref[r, sl, hd * dv:(hd + 1) * dv]
        irow = gates_ref[r, hd:hd + 1, sl]
        frow = gates_ref[r, H + hd:H + hd + 1, sl]
        fh, fm, fl = _split3(jnp.where(tri, frow, 0.0))
        bmat = _dot(fh, ones) + _dot(fm, ones) + _dot(fl, ones)
        brow = _dot_nt(ones, fh) + _dot_nt(ones, fm) + _dot_nt(ones, fl)
        qk = _dot_nt(qc, kc)
        cmat = c_ref[ci]
        nrow = n_ref[ci]
        yield
        log_d = jnp.where(tri, bmat - brow + irow, -jnp.inf)
        bcol = bmat[:, 0:1]
        log_inter = bcol + m
        m_t = jnp.maximum(log_inter, jnp.max(log_d, axis=-1, keepdims=True))
        s = qk * jnp.exp(log_d - m_t)
        inter = jnp.exp(log_inter - m_t)
        b_end = bmat[L - 1:L, 0:1]
        log_w = b_end - brow[0:1, :] + irow
        m_new = jnp.maximum(b_end + m, jnp.max(log_w, axis=-1, keepdims=True))
        w_row = jnp.exp(log_w - m_new)
        carry_decay = jnp.exp(b_end + m - m_new)
        wdiag = jnp.where(eye, w_row, 0.0).astype(BF16)
        wv = _dot(wdiag, vc).astype(BF16)
        n_add = _dot(jnp.broadcast_to(w_row, (SUBLANES, L)).astype(BF16), kc)[0:1, :]
        num = _dot(s.astype(BF16), vc) + inter * _dot(qc, cmat.astype(BF16))
        yield
        c_new = carry_decay * cmat + _dot_tn(kc, wv)
        n_new = carry_decay * nrow + n_add
        qn = jnp.sum(qc.astype(F32) * nrow, axis=-1, keepdims=True)
        den = jnp.sum(s, axis=-1, keepdims=True) + inter * qn
        h = num / jnp.maximum(jnp.abs(den), jnp.exp(-m_t))
        ng = ng_ref[:, hd * dv:(hd + 1) * dv]
        hn = h * lax.rsqrt(jnp.mean(h * h, axis=-1, keepdims=True) + NORM_EPS) * ng
        og = o_ref[r, sl, hd * dv:(hd + 1) * dv].astype(F32)
        out = (hn * _sigmoid(og)).astype(out_ref.dtype)
        return (m_new, xq[L - TAIL:, :], xk[L - TAIL:, :]), c_new, n_new, out

    chains = [(r, hd) for r in range(R) for hd in range(H)]

    def body(c, carries):
        sl = pl.ds(pl.multiple_of(c * L, L), L)
        res = _lockstep(step(r, hd, c, sl, *carries[ci]) for ci, (r, hd) in enumerate(chains))
        for ci, (r, hd) in enumerate(chains):
            _, c_ref[ci], n_ref[ci], out_ref[r, sl, hd * dv:(hd + 1) * dv] = res[ci]
        return tuple(x[0] for x in res)

    init = tuple((m_ref[ci][0:1, 0:1], tq_ref[ci], tk_ref[ci]) for ci in range(len(chains)))
    final = lax.fori_loop(0, nc, body, init)
    for ci, (m, tq, tk) in enumerate(final):
        m_ref[ci] = jnp.broadcast_to(m, m_ref.shape[1:])
        tq_ref[ci] = tq
        tk_ref[ci] = tk


def mlstm_scan(proj, gates, conv_w, ng):
    b, t, _ = proj.shape
    h, dk, dv, L = ML_HEADS, ML_DQK, ML_DV, ML_CHUNK
    rows = min(ML_ROWS, b)
    tb = min(ML_TIME_BLOCK, t)
    wqk, wv = 2 * h * dk, h * dv
    nchains = rows * h
    seq = lambda w, col: pl.BlockSpec((rows, tb, w), lambda i, j: (i, j, col))
    return pl.pallas_call(
        _mlstm_kernel,
        grid=(b // rows, t // tb),
        in_specs=[seq(wqk, 0), seq(wv, wqk // wv), seq(wv, wqk // wv + 1),
                  pl.BlockSpec((rows, 2 * h, tb), lambda i, j: (i, 0, j)),
                  _resident(conv_w.shape), _resident(ng.shape)],
        out_specs=pl.BlockSpec((rows, tb, wv), lambda i, j: (i, j, 0)),
        out_shape=jax.ShapeDtypeStruct((b, t, wv), BF16),
        scratch_shapes=[pltpu.VMEM((nchains, dk, dv), F32), pltpu.VMEM((nchains, 1, dk), F32),
                        pltpu.VMEM((nchains, SUBLANES, LANES), F32), pltpu.VMEM((nchains, SUBLANES, dk), F32),
                        pltpu.VMEM((nchains, SUBLANES, dk), F32)],
        compiler_params=_cparams(("parallel", "arbitrary")),
        name="mlstm_scan",
    )(proj, proj, proj, gates, conv_w, ng)


def mlstm_mixer(x, g_pre, w_in, b_if, conv_w, norm_g, b, t):
    h, dk, dv, L = ML_HEADS, ML_DQK, ML_DV, ML_CHUNK
    n_main = 2 * h * dk + 2 * h * dv
    w_main = w_in[:, :n_main].astype(BF16)
    w_if_t = w_in[:, n_main:].T.astype(BF16)
    proj, gates = mlstm_proj(x, g_pre, w_main, w_if_t, b_if.reshape(2 * h, 1))
    gates = jnp.transpose(gates.reshape(2 * h, b, t), (1, 0, 2))
    hs = mlstm_scan(proj.reshape(b, t, n_main), gates, conv_w, norm_g[None, :])
    return hs.reshape(b * t, h * dv)


def _retention_kernel(lg_ref, q_ref, k_ref, v_ref, g_ref, cos_ref, sin_ref, out_ref, r_ref):
    L, H, dk, dv = RT_CHUNK, RT_HEADS, RT_DK, RT_DV
    half = dk // 2
    nc = q_ref.shape[1] // L

    @pl.when(pl.program_id(1) == 0)
    def _():
        r_ref[...] = jnp.zeros_like(r_ref)

    row = lax.broadcasted_iota(jnp.int32, (L, L), 0)
    col = lax.broadcasted_iota(jnp.int32, (L, L), 1)
    diff = jnp.maximum(row - col, 0).astype(F32)
    idx = lax.broadcasted_iota(jnp.int32, (L, 1), 0).astype(F32)

    def rotate(u, cos, sin):
        u1, u2 = u[:, :half], u[:, half:]
        return jnp.concatenate([u1 * cos - u2 * sin, u1 * sin + u2 * cos], axis=-1)

    def step(hd, sl, cos, sin):
        log_gamma = lg_ref[hd]
        qr = rotate(q_ref[0, sl, hd * dk:(hd + 1) * dk].astype(F32), cos, sin)
        kr = rotate(k_ref[0, sl, hd * dk:(hd + 1) * dk].astype(F32), cos, sin) * (dk ** -0.5)
        vc = v_ref[0, sl, hd * dv:(hd + 1) * dv]
        qb = qr.astype(BF16)
        qk = _dot_nt(qb, kr.astype(BF16))
        rmat = r_ref[hd]
        k_decay = jnp.exp(log_gamma * (L - 1.0 - idx))
        r_add = _dot_tn((kr * k_decay).astype(BF16), vc)
        cross = _dot(qb, rmat.astype(BF16))
        yield
        d_mask = jnp.where(row >= col, jnp.exp(log_gamma * diff), 0.0)
        q_decay = jnp.exp(log_gamma * (idx + 1.0))
        y = _dot((qk * d_mask).astype(BF16), vc) + cross * q_decay
        yield
        r_new = jnp.exp(jnp.full((1, 1), L, F32) * log_gamma) * rmat + r_add
        yn = y * lax.rsqrt(jnp.mean(y * y, axis=-1, keepdims=True) + NORM_EPS)
        gc = g_ref[0, sl, hd * dv:(hd + 1) * dv].astype(F32)
        return r_new, (gc * _sigmoid(gc) * yn).astype(out_ref.dtype)

    def body(c, carry):
        sl = pl.ds(pl.multiple_of(c * L, L), L)
        cos = cos_ref[0, sl, :]
        sin = sin_ref[0, sl, :]
        res = _lockstep(step(hd, sl, cos, sin) for hd in range(H))
        for hd in range(H):
            r_ref[hd], out_ref[0, sl, hd * dv:(hd + 1) * dv] = res[hd]
        return carry

    lax.fori_loop(0, nc, body, 0)


def retention_scan(proj, cos, sin, log_gamma):
    b, t, _ = proj.shape
    h, dk, dv = RT_HEADS, RT_DK, RT_DV
    tb = min(RT_TIME_BLOCK, t)
    wq, wv = h * dk, h * dv
    seq = lambda w, col: pl.BlockSpec((1, tb, w), lambda i, j: (i, j, col))
    return pl.pallas_call(
        _retention_kernel,
        grid=(b, t // tb),
        in_specs=[pl.BlockSpec(memory_space=pltpu.SMEM),
                  seq(wq, 0), seq(wq, 1), seq(wv, 2 * wq // wv), seq(wv, 2 * wq // wv + 1),
                  seq(dk // 2, 0), seq(dk // 2, 0)],
        out_specs=seq(wv, 0),
        out_shape=jax.ShapeDtypeStruct((b, t, wv), BF16),
        scratch_shapes=[pltpu.VMEM((h, dk, dv), F32)],
        compiler_params=_cparams(("parallel", "arbitrary")),
        name="retention_scan",
    )(log_gamma, proj, proj, proj, proj, cos, sin)


def retention_mixer(x, positions, g_pre, w_in, b, t):
    h, dk, dv = RT_HEADS, RT_DK, RT_DV
    proj = norm_matmul(x, g_pre, w_in.astype(BF16), BF16).reshape(b, t, -1)
    inv_freq = 1.0 / (RT_ROPE_BASE ** jnp.linspace(0.0, 1.0, dk // 2, dtype=F32))
    ang = positions.astype(F32)[:, :, None] * inv_freq
    log_gamma = jnp.log(1.0 - 2.0 ** (-5.0 - jnp.arange(h, dtype=F32)))
    y = retention_scan(proj, jnp.cos(ang), jnp.sin(ang), log_gamma)
    return y.reshape(b * t, h * dv)


def _rwkv_proj_kernel(x_ref, xp_ref, g_ref, mu_ref, wrkv_ref, w0_ref, w1_ref, w2_ref,
                      a0_ref, a1_ref, a2_ref, g1_ref, g2_ref,
                      r_ref, k_ref, v_ref, gc_ref, a_ref, gate_ref, *, seq_tiles):
    i = pl.program_id(0)
    tm = x_ref.shape[0]
    g = g_ref[...]
    h = _rms(x_ref[...], g)
    prev = _rms(xp_ref[...], g)[SUBLANES - 1:SUBLANES, :]
    prev = prev * (i % seq_tiles != 0).astype(F32)
    rid = lax.broadcasted_iota(jnp.int32, (tm, 1), 0)
    h_prev = jnp.where(rid == 0, prev, pltpu.roll(h, 1, axis=0))
    span = min(tm, MXU_DEPTH)
    rr = lax.broadcasted_iota(jnp.int32, (span, span), 0)
    cc = lax.broadcasted_iota(jnp.int32, (span, span), 1)
    chunk_tril = ((rr // RW_CHUNK == cc // RW_CHUNK) & (rr >= cc)).astype(BF16)

    def rows(lo):
        hh = h[lo:lo + span, :]
        xx = h_prev[lo:lo + span, :] - hh
        mix = lambda j: (hh + xx * mu_ref[j:j + 1, :]).astype(BF16)

        def put(ref, val):
            for p in range(ref.shape[0]):
                ref[p, lo:lo + span, :] = val[:, p * LANES:(p + 1) * LANES].astype(ref.dtype)

        put(r_ref, _dot(mix(0), wrkv_ref[0]))
        yield
        put(k_ref, _dot(mix(1), wrkv_ref[1]))
        yield
        put(v_ref, _dot(mix(2), wrkv_ref[2]))
        yield
        wl = jnp.tanh(_dot(mix(3), w1_ref[...])).astype(BF16)
        al = _dot(mix(4), a1_ref[...]).astype(BF16)
        gl = _sigmoid(_dot(mix(5), g1_ref[...])).astype(BF16)
        yield
        w_log = -_softplus(-(w0_ref[...] + _dot(wl, w2_ref[...]))) - 0.5
        put(a_ref, _sigmoid(a0_ref[...] + _dot(al, a2_ref[...])))
        put(gate_ref, _dot(gl, g2_ref[...]))
        yield
        lh, lm, ll = _split3(-jnp.exp(w_log))
        put(gc_ref, _dot(chunk_tril, lh) + _dot(chunk_tril, lm) + _dot(chunk_tril, ll))

    _lockstep(rows(lo) for lo in range(0, tm, span))


def rwkv_proj(x, g_pre, mu, w_rkv, w0, w1, w2, a0, a1, a2, g1, g2, t, *, tm=512):
    n, d = x.shape
    tm = min(tm, t)
    p = d // LANES
    outs = pl.BlockSpec((p, tm, LANES), lambda i: (0, i, 0))
    args = (g_pre, mu, w_rkv, w0, w1, w2, a0, a1, a2, g1, g2)
    shp = lambda dt: jax.ShapeDtypeStruct((p, n, LANES), dt)
    return pl.pallas_call(
        functools.partial(_rwkv_proj_kernel, seq_tiles=t // tm),
        grid=(n // tm,),
        in_specs=[
            pl.BlockSpec((tm, d), lambda i: (i, 0)),
            pl.BlockSpec((SUBLANES, d), lambda i: (jnp.maximum(i * (tm // SUBLANES) - 1, 0), 0)),
        ] + [_resident(a.shape) for a in args],
        out_specs=[outs] * 6,
        out_shape=[shp(BF16), shp(BF16), shp(BF16), shp(F32), shp(BF16), shp(BF16)],
        compiler_params=_cparams(("parallel",)),
        name="rwkv_proj",
    )(x, x, *args)


def _rwkv_scan_kernel(r_ref, k_ref, v_ref, gc_ref, a_ref, gate_ref, kk_ref, ka_ref, rk_ref,
                      lng_ref, lnb_ref, out_ref, s_ref):
    C = RW_CHUNK
    N = RW_HEAD
    C2 = 2 * C
    nc = r_ref.shape[2] // C

    @pl.when(pl.program_id(1) == 0)
    def _():
        s_ref[...] = jnp.zeros_like(s_ref)

    lane = lax.broadcasted_iota(jnp.int32, (1, LANES), 1)
    in_a = lane < N
    m_a = in_a.astype(F32)
    m_b = 1.0 - m_a
    rr = lax.broadcasted_iota(jnp.int32, (C2, C2), 0)
    cc = lax.broadcasted_iota(jnp.int32, (C2, C2), 1)
    same = (rr // C) == (cc // C)
    strict = same & ((rr % C) > (cc % C))
    incl = same & ((rr % C) >= (cc % C))
    eye2 = (rr == cc).astype(F32)
    level_masks = []
    for lvl in range(int(math.log2(C))):
        rb, cb = (rr % C) >> lvl, (cc % C) >> lvl
        level_masks.append(same & ((rb & 1) == 1) & (cb == rb - 1))
    incl2 = jnp.concatenate([incl, incl], axis=1)
    first_row = lax.broadcasted_iota(jnp.int32, (C, 1), 0) == 0

    def seg_sum(z):
        sa = jnp.sum(z * m_a, axis=-1, keepdims=True)
        sb = jnp.sum(z * m_b, axis=-1, keepdims=True)
        return jnp.where(in_a, sa, sb)

    def stack(z):
        return jnp.concatenate([z * m_a, z * m_b], axis=0)

    def load(row, gi, sl):
        f32 = lambda ref: ref[gi, row, sl, :].astype(F32)
        return (f32(r_ref), f32(k_ref), f32(v_ref), gc_ref[gi, row, sl, :], f32(a_ref), f32(gate_ref),
                s_ref[row, gi], kk_ref[gi], ka_ref[gi], rk_ref[gi], lng_ref[gi], lnb_ref[gi])

    def step(r, k, v, gcum, a_sig, gate, st0, kk_w, ka_w, rk_w, ln_g, ln_b):
        kk = k * kk_w
        kk = kk / jnp.maximum(jnp.sqrt(seg_sum(kk * kk)), 1e-12)
        k = k * (1.0 + (a_sig - 1.0) * ka_w)
        a_ = -kk
        b_ = kk * a_sig
        g_end = gcum[C - 1:C, :]
        g_excl = jnp.where(first_row, 0.0, pltpu.roll(gcum, 1, axis=0))
        e_neg = jnp.exp(-gcum)
        e_rest = jnp.exp(g_end - gcum)
        a_h = stack(a_ * jnp.exp(g_excl))
        r_h = stack(r * jnp.exp(gcum))
        b_t = stack(b_ * e_rest)
        k_t = stack(k * e_rest)
        twice = lambda z: jnp.concatenate([z, z], axis=0)
        v_s = twice(v).astype(BF16)
        lhs = jnp.concatenate([a_h, r_h], axis=0).astype(BF16)
        rhs = jnp.concatenate([twice(b_ * e_neg), twice(k * e_neg)], axis=0).astype(BF16)
        sc = _dot_nt(lhs, rhs)
        yield
        a_ab = jnp.where(strict, sc[:C2, :C2], 0.0)
        a_ak = jnp.where(strict, sc[:C2, C2:], 0.0)
        a_r = jnp.where(incl2, sc[C2:, :], 0.0).astype(BF16)
        w1 = _dot(a_ak.astype(BF16), v_s)
        tinv = eye2 + jnp.where(level_masks[0], a_ab, 0.0)
        for mk in level_masks[1:]:
            tb = tinv.astype(BF16)
            l21 = jnp.where(mk, a_ab, 0.0).astype(BF16)
            tl = _dot(tb, l21)
            yield
            tinv = tinv + _dot(tl.astype(BF16), tb)
            yield
        au = _dot(tinv.astype(BF16), jnp.concatenate([a_h, w1], axis=1).astype(BF16))
        yield
        amat = jnp.concatenate([au.astype(BF16),
                                jnp.concatenate([jnp.zeros((C2, LANES), BF16), v_s], axis=1)], axis=0)
        pz = _dot_tn(jnp.concatenate([b_t, k_t], axis=0).astype(BF16), amat)
        ry = _dot(a_r, amat)
        yield
        r_bar = r_h + ry[:, :LANES]
        sy = _dot(jnp.concatenate([r_bar, pz[:, :LANES]], axis=0).astype(BF16), st0.astype(BF16))
        yield
        st_new = st0 * jnp.broadcast_to(jnp.exp(g_end), (LANES, LANES)).T + sy[C2:, :] + pz[:, LANES:]
        ys = sy[:C2, :] + ry[:, LANES:]
        y = jnp.where(in_a, ys[:C, :], ys[C:, :])
        mean = seg_sum(y) * (1.0 / N)
        yc = y - mean
        var = seg_sum(yc * yc) * (1.0 / N)
        yn = yc * lax.rsqrt(var + RW_LN_EPS) * ln_g + ln_b
        bonus = seg_sum(r * k * rk_w) * v
        return st_new, ((yn + bonus) * gate).astype(out_ref.dtype)

    def body(c, carry):
        sl = pl.ds(pl.multiple_of(c * C, C), C)
        chains = [(row, gi) for row in range(r_ref.shape[1]) for gi in range(r_ref.shape[0])]
        results = _lockstep(step(*load(row, gi, sl)) for row, gi in chains)
        for (row, gi), res in zip(chains, results):
            s_ref[row, gi], out_ref[gi, row, sl, :] = res
        return carry

    lax.fori_loop(0, nc, body, 0)


def rwkv_scan(r, k, v, gcum, a, gate, k_k, k_a, r_k, ln_g, ln_b, b, t):
    p, n, l = r.shape
    tb = min(RW_TIME_BLOCK, t)
    rows = min(RW_ROWS, b)
    seq = lambda: pl.BlockSpec((p, rows, tb, l), lambda i, j: (0, i, j, 0))
    par = lambda: pl.BlockSpec((p, 1, l), lambda i, j: (0, 0, 0))
    view = lambda a: a.reshape(p, b, t, l)
    y = pl.pallas_call(
        _rwkv_scan_kernel,
        grid=(b // rows, t // tb),
        in_specs=[seq()] * 6 + [par()] * 5,
        out_specs=seq(),
        out_shape=jax.ShapeDtypeStruct((p, b, t, l), BF16),
        scratch_shapes=[pltpu.VMEM((rows, p, l, l), F32)],
        compiler_params=_cparams(("parallel", "arbitrary")),
        name="rwkv_scan",
    )(view(r), view(k), view(v), view(gcum), view(a), view(gate), k_k, k_a, r_k, ln_g, ln_b)
    return y.reshape(p, n, l)


def rwkv_mixer(x, g_pre, mu, w_rkv, w0, w1, w2, a0, a1, a2, g1, g2,
               k_k, k_a, r_k, ln_g, ln_b, b, t):
    bf = lambda a: a.astype(BF16)
    row = lambda a: a.reshape(1, -1)
    r, k, v, gcum, a, gate = rwkv_proj(x, g_pre, mu, bf(w_rkv), row(w0), bf(w1), bf(w2),
                                     row(a0), bf(a1), bf(a2), bf(g1), bf(g2), t)
    par = lambda a: a.reshape(RW_PAIRS, 1, LANES)
    return rwkv_scan(r, k, v, gcum, a, gate, par(k_k), par(k_a), par(r_k), par(ln_g), par(ln_b), b, t)


def kernel(x, positions, norm_g, ffn_w_gu, ffn_w_down, ml_w_in, ml_b_if, ml_conv_w, ml_norm_g, ml_w_out,
           rw_mu, rw_w_rkv, rw_w0, rw_w1, rw_w2, rw_a0, rw_a1, rw_a2, rw_g1, rw_g2,
           rw_k_k, rw_k_a, rw_r_k, rw_ln_g, rw_ln_b, rw_w_out, rt_w_in, rt_w_out):
    b, t, d = x.shape
    depth = norm_g.shape[0]
    xf = x.reshape(b * t, d)
    w_gu, w_down = ffn_w_gu[0, 0].astype(BF16), ffn_w_down[0, 0].astype(BF16)
    w_down_view = ffn_w_down.reshape(depth, 2, -1, ffn_w_gu.shape[3])

    def ffn(k, xf, g_pre, g_post, mix=None):
        nxt = ()
        if k + 1 < 2 * depth:
            pick = divmod(k + 1, 2)
            nxt = ((ffn_w_gu, pick), (w_down_view, pick))
        outs = ffn_sublayer(xf, g_pre, w_gu, w_down, g_post, mix=mix, cast_next=nxt)
        if nxt:
            return outs[0], outs[1], outs[2].reshape(ffn_w_down.shape[2:])
        return outs[0], None, None

    for layer in range(depth):
        g = norm_g[layer][:, None, :]
        xf, w_gu, w_down = ffn(2 * layer, xf, g[0], g[1])
        kind, j = layer % 3, layer // 3
        if kind == 0:
            h = mlstm_mixer(xf, g[2], ml_w_in[j], ml_b_if[j], ml_conv_w[j], ml_norm_g[j], b, t)
            w_out = ml_w_out[j]
        elif kind == 1:
            h = rwkv_mixer(xf, g[2], rw_mu[j], rw_w_rkv[j], rw_w0[j], rw_w1[j], rw_w2[j],
                           rw_a0[j], rw_a1[j], rw_a2[j], rw_g1[j], rw_g2[j],
                           rw_k_k[j], rw_k_a[j], rw_r_k[j].reshape(-1), rw_ln_g[j], rw_ln_b[j], b, t)
            w_out = rw_w_out[j]
        else:
            h = retention_mixer(xf, positions, g[2], rt_w_in[j], b, t)
            w_out = rt_w_out[j]
        xf, w_gu, w_down = ffn(2 * layer + 1, xf, g[4], g[5], mix=(h, w_out.astype(BF16), g[3]))
    return xf.reshape(b, t, d)
```

```python
import functools
import math

import jax
import jax.numpy as jnp
from jax import lax
from jax.experimental import pallas as pl
from jax.experimental.pallas import tpu as pltpu

F32 = jnp.float32
BF16 = jnp.bfloat16

D_MODEL = 1024
NORM_EPS = 1e-6
HALF_STEP = 0.5

ML_HEADS = 4
ML_DQK = 128
ML_DV = 256
ML_CHUNK = 256
ML_GATE_CAP = 15.0
ML_ROWS = 2
ML_TIME_BLOCK = 1024

RW_HEAD = 64
RW_PAIRS = D_MODEL // (2 * RW_HEAD)
RW_CHUNK = 64
RW_TIME_BLOCK = 256
RW_ROWS = 2
RW_LN_EPS = 64e-5

RT_HEADS = 4
RT_DK = 256
RT_DV = 512
RT_CHUNK = 256
RT_TIME_BLOCK = 512
RT_ROPE_BASE = 10000.0

LANES = 128
SUBLANES = 8
MXU_DEPTH = 256
VMEM_LIMIT = 56 * 1024 * 1024


def _cparams(sem):
    return pltpu.CompilerParams(dimension_semantics=sem, vmem_limit_bytes=VMEM_LIMIT)


def _lockstep(gens):
    gens = list(gens)
    results = [None] * len(gens)
    live = list(range(len(gens)))
    while live:
        for i in list(live):
            try:
                next(gens[i])
            except StopIteration as done:
                results[i] = done.value
                live.remove(i)
    return results


def _rms(x, g):
    return x * lax.rsqrt(jnp.mean(x * x, axis=-1, keepdims=True) + NORM_EPS) * g


def _dot(a, b):
    return jnp.dot(a, b, preferred_element_type=F32)


def _dot_nt(a, b):
    return lax.dot_general(a, b, (((1,), (1,)), ((), ())), preferred_element_type=F32)


def _dot_tn(a, b):
    return lax.dot_general(a, b, (((0,), (0,)), ((), ())), preferred_element_type=F32)


def _split3(x):
    hi = x.astype(BF16)
    r1 = x - hi.astype(F32)
    mid = r1.astype(BF16)
    lo = (r1 - mid.astype(F32)).astype(BF16)
    return hi, mid, lo


def _sigmoid(x):
    return 1.0 / (1.0 + jnp.exp(-x))


def _softplus(x):
    return jnp.maximum(x, 0.0) + jnp.log(1.0 + jnp.exp(-jnp.abs(x)))


def _ffn_kernel(*refs, mixer, ncast):
    nin = len(refs) - 1 - 2 * ncast
    cast_in, o_ref, cast_out = refs[nin:nin + ncast], refs[nin + ncast], refs[nin + ncast + 1:]
    if mixer is None:
        x_ref, gpre_ref, wgu_ref, wd_ref, gpost_ref = refs[:nin]
    else:
        h_ref, wo_ref, gmix_ref, x_ref, gpre_ref, wgu_ref, wd_ref, gpost_ref = refs[:nin]
    for src, dst in zip(cast_in, cast_out):
        dst[...] = src[...].astype(dst.dtype)
    f = wd_ref.shape[0]
    half = x_ref.shape[0] // 2

    def rows(lo):
        x = x_ref[lo:lo + half, :]
        if mixer is not None:
            if mixer == "pairs":
                h = jnp.concatenate([h_ref[p, lo:lo + half, :] for p in range(h_ref.shape[0])], axis=-1)
            else:
                h = h_ref[lo:lo + half, :]
            y = _dot(h, wo_ref[...])
            yield
            x = x + _rms(y, gmix_ref[...])
        xn = _rms(x, gpre_ref[...]).astype(BF16)
        yield
        gate = _dot(xn, wgu_ref[:, :f])
        up = _dot(xn, wgu_ref[:, f:])
        yield
        act = (gate * _sigmoid(gate) * up).astype(BF16)
        h2 = _dot(act, wd_ref[...])
        yield
        o_ref[lo:lo + half, :] = x + HALF_STEP * _rms(h2, gpost_ref[...])

    _lockstep(rows(lo) for lo in (0, half))


def _resident(shape):
    return pl.BlockSpec(shape, lambda *_: (0,) * len(shape), pipeline_mode=pl.Buffered(1))


def _cast_tiling(rows, steps):
    nblocks = steps
    while nblocks > 1 and (rows % nblocks or (rows // nblocks) % (2 * SUBLANES)):
        nblocks //= 2
    return rows // nblocks, steps // nblocks


def ffn_sublayer(x, g_pre, w_gu, w_down, g_post, mix=None, cast_next=(), *, tm=512):
    n, d = x.shape
    tm = min(tm, n)
    steps = n // tm
    ffn_args = (x, g_pre, w_gu, w_down, g_post)
    ffn_specs = [pl.BlockSpec((tm, d), lambda i: (i, 0)), _resident(g_pre.shape), _resident(w_gu.shape),
                 _resident(w_down.shape), _resident(g_post.shape)]
    mixer, mix_args, mix_specs = None, (), []
    if mix is not None:
        h, w_out, g_mix = mix
        if h.ndim == 3:
            mixer = "pairs"
            h_spec = pl.BlockSpec((h.shape[0], tm, h.shape[2]), lambda i: (0, i, 0))
        else:
            mixer = "rows"
            h_spec = pl.BlockSpec((tm, h.shape[1]), lambda i: (i, 0))
        mix_args = (h, w_out, g_mix)
        mix_specs = [h_spec, _resident(w_out.shape), _resident(g_mix.shape)]
    cast_in_specs, cast_out_specs, cast_shapes = [], [], []
    for w, (layer, idx) in cast_next:
        rows, cols = w.shape[2:]
        rb, per = _cast_tiling(rows, steps)
        cast_in_specs.append(pl.BlockSpec((None, None, rb, cols),
                                          lambda i, per=per, layer=layer, idx=idx: (layer, idx, i // per, 0)))
        cast_out_specs.append(pl.BlockSpec((rb, cols), lambda i, per=per: (i // per, 0)))
        cast_shapes.append(jax.ShapeDtypeStruct((rows, cols), BF16))
    return pl.pallas_call(
        functools.partial(_ffn_kernel, mixer=mixer, ncast=len(cast_next)),
        grid=(steps,),
        in_specs=mix_specs + ffn_specs + cast_in_specs,
        out_specs=[pl.BlockSpec((tm, d), lambda i: (i, 0))] + cast_out_specs,
        out_shape=[jax.ShapeDtypeStruct((n, d), F32)] + cast_shapes,
        compiler_params=_cparams(("arbitrary",) if cast_next else ("parallel",)),
        name="ffn" if mix is None else "mix_ffn",
    )(*mix_args, *ffn_args, *[w for w, _ in cast_next])


def _norm_matmul_kernel(x_ref, g_ref, w_ref, o_ref):
    half = x_ref.shape[0] // 2

    def rows(lo):
        xn = _rms(x_ref[lo:lo + half, :], g_ref[...]).astype(BF16)
        yield
        o_ref[lo:lo + half, :] = _dot(xn, w_ref[...]).astype(o_ref.dtype)

    _lockstep(rows(lo) for lo in (0, half))


def norm_matmul(x, g, w, out_dtype, *, tm=512):
    n, d = x.shape
    nout = w.shape[1]
    tm = min(tm, n)
    return pl.pallas_call(
        _norm_matmul_kernel,
        grid=(n // tm,),
        in_specs=[
            pl.BlockSpec((tm, d), lambda i: (i, 0)),
            _resident(g.shape),
            _resident(w.shape),
        ],
        out_specs=pl.BlockSpec((tm, nout), lambda i: (i, 0)),
        out_shape=jax.ShapeDtypeStruct((n, nout), out_dtype),
        compiler_params=_cparams(("parallel",)),
        name="norm_matmul",
    )(x, g, w)


def _mlstm_proj_kernel(x_ref, g_ref, w_ref, wif_ref, bif_ref, proj_ref, gates_ref):
    H = ML_HEADS
    half = x_ref.shape[0] // 2

    def rows(lo):
        xn = _rms(x_ref[lo:lo + half, :], g_ref[...]).astype(BF16)
        yield
        proj_ref[lo:lo + half, :] = _dot(xn, w_ref[...]).astype(proj_ref.dtype)
        pre = _dot_nt(wif_ref[...], xn) + bif_ref[...]
        pre = ML_GATE_CAP * jnp.tanh(pre * (1.0 / ML_GATE_CAP))
        is_input_gate = lax.broadcasted_iota(jnp.int32, pre.shape, 0) < H
        gates_ref[:, lo:lo + half] = jnp.where(is_input_gate, pre, -_softplus(-pre))

    _lockstep(rows(lo) for lo in (0, half))


def mlstm_proj(x, g, w_main, w_if_t, b_if, *, tm=512):
    n, d = x.shape
    tm = min(tm, n)
    nout = w_main.shape[1]
    ng = w_if_t.shape[0]
    return pl.pallas_call(
        _mlstm_proj_kernel,
        grid=(n // tm,),
        in_specs=[pl.BlockSpec((tm, d), lambda i: (i, 0)), _resident(g.shape), _resident(w_main.shape),
                  _resident(w_if_t.shape), _resident(b_if.shape)],
        out_specs=[pl.BlockSpec((tm, nout), lambda i: (i, 0)), pl.BlockSpec((ng, tm), lambda i: (0, i))],
        out_shape=[jax.ShapeDtypeStruct((n, nout), BF16), jax.ShapeDtypeStruct((ng, n), F32)],
        compiler_params=_cparams(("parallel",)),
        name="mlstm_proj",
    )(x, g, w_main, w_if_t, b_if)


def _mlstm_kernel(qk_ref, v_ref, o_ref, gates_ref, cw_ref, ng_ref, out_ref,
                  c_ref, n_ref, m_ref, tq_ref, tk_ref):
    L, H, dk, dv = ML_CHUNK, ML_HEADS, ML_DQK, ML_DV
    R = qk_ref.shape[0]
    nc = qk_ref.shape[1] // L
    KW = cw_ref.shape[0]
    TAIL = SUBLANES

    @pl.when(pl.program_id(1) == 0)
    def _():
        for ref in (c_ref, n_ref, m_ref, tq_ref, tk_ref):
            ref[...] = jnp.zeros_like(ref)

    row = lax.broadcasted_iota(jnp.int32, (L, L), 0)
    col = lax.broadcasted_iota(jnp.int32, (L, L), 1)
    tri = row >= col
    eye = row == col
    ones = jnp.ones((L, L), BF16)

    def conv_silu(x, tail, w):
        ext = jnp.concatenate([tail, x], axis=0)
        acc = x * w[KW - 1:KW, :]
        for i in range(KW - 1):
            off = TAIL - (KW - 1) + i
            acc = acc + ext[off:off + L, :] * w[i:i + 1, :]
        return acc * _sigmoid(acc)

    def step(r, hd, c, sl, m, tq, tk):
        ci = r * H + hd
        xq = qk_ref[r, sl, hd * dk:(hd + 1) * dk].astype(F32)
        xk = qk_ref[r, sl, (H + hd) * dk:(H + hd + 1) * dk].astype(F32)
        qc = (conv_silu(xq, tq, cw_ref[:, hd * dk:(hd + 1) * dk]) * (dk ** -0.5)).astype(BF16)
        kc = conv_silu(xk, tk, cw_ref[:, (H + hd) * dk:(H + hd + 1) * dk]).astype(BF16)
        vc = v_ref[r, sl, hd * dv:(hd + 1) * dv]
        irow = gates_ref[r, hd:hd + 1, sl]
        frow = gates_ref[r, H + hd:H + hd + 1, sl]
        fh, fm, fl = _split3(jnp.where(tri, frow, 0.0))
        bmat = _dot(fh, ones) + _dot(fm, ones) + _dot(fl, ones)
        brow = _dot_nt(ones, fh) + _dot_nt(ones, fm) + _dot_nt(ones, fl)
        qk = _dot_nt(qc, kc)
        cmat = c_ref[ci]
        nrow = n_ref[ci]
        yield
        log_d = jnp.where(tri, bmat - brow + irow, -jnp.inf)
        bcol = bmat[:, 0:1]
        log_inter = bcol + m
        m_t = jnp.maximum(log_inter, jnp.max(log_d, axis=-1, keepdims=True))
        s = qk * jnp.exp(log_d - m_t)
        inter = jnp.exp(log_inter - m_t)
        b_end = bmat[L - 1:L, 0:1]
        log_w = b_end - brow[0:1, :] + irow
        m_new = jnp.maximum(b_end + m, jnp.max(log_w, axis=-1, keepdims=True))
        w_row = jnp.exp(log_w - m_new)
        carry_decay = jnp.exp(b_end + m - m_new)
        wdiag = jnp.where(eye, w_row, 0.0).astype(BF16)
        wv = _dot(wdiag, vc).astype(BF16)
        n_add = _dot(jnp.broadcast_to(w_row, (SUBLANES, L)).astype(BF16), kc)[0:1, :]
        num = _dot(s.astype(BF16), vc) + inter * _dot(qc, cmat.astype(BF16))
        yield
        c_new = carry_decay * cmat + _dot_tn(kc, wv)
        n_new = carry_decay * nrow + n_add
        qn = jnp.sum(qc.astype(F32) * nrow, axis=-1, keepdims=True)
        den = jnp.sum(s, axis=-1, keepdims=True) + inter * qn
        h = num / jnp.maximum(jnp.abs(den), jnp.exp(-m_t))
        ng = ng_ref[:, hd * dv:(hd + 1) * dv]
        hn = h * lax.rsqrt(jnp.mean(h * h, axis=-1, keepdims=True) + NORM_EPS) * ng
        og = o_ref[r, sl, hd * dv:(hd + 1) * dv].astype(F32)
        out = (hn * _sigmoid(og)).astype(out_ref.dtype)
        return (m_new, xq[L - TAIL:, :], xk[L - TAIL:, :]), c_new, n_new, out

    chains = [(r, hd) for r in range(R) for hd in range(H)]

    def body(c, carries):
        sl = pl.ds(pl.multiple_of(c * L, L), L)
        res = _lockstep(step(r, hd, c, sl, *carries[ci]) for ci, (r, hd) in enumerate(chains))
        for ci, (r, hd) in enumerate(chains):
            _, c_ref[ci], n_ref[ci], out_ref[r, sl, hd * dv:(hd + 1) * dv] = res[ci]
        return tuple(x[0] for x in res)

    init = tuple((m_ref[ci][0:1, 0:1], tq_ref[ci], tk_ref[ci]) for ci in range(len(chains)))
    final = lax.fori_loop(0, nc, body, init)
    for ci, (m, tq, tk) in enumerate(final):
        m_ref[ci] = jnp.broadcast_to(m, m_ref.shape[1:])
        tq_ref[ci] = tq
        tk_ref[ci] = tk


def mlstm_scan(proj, gates, conv_w, ng):
    b, t, _ = proj.shape
    h, dk, dv, L = ML_HEADS, ML_DQK, ML_DV, ML_CHUNK
    rows = min(ML_ROWS, b)
    tb = min(ML_TIME_BLOCK, t)
    wqk, wv = 2 * h * dk, h * dv
    nchains = rows * h
    seq = lambda w, col: pl.BlockSpec((rows, tb, w), lambda i, j: (i, j, col))
    return pl.pallas_call(
        _mlstm_kernel,
        grid=(b // rows, t // tb),
        in_specs=[seq(wqk, 0), seq(wv, wqk // wv), seq(wv, wqk // wv + 1),
                  pl.BlockSpec((rows, 2 * h, tb), lambda i, j: (i, 0, j)),
                  _resident(conv_w.shape), _resident(ng.shape)],
        out_specs=pl.BlockSpec((rows, tb, wv), lambda i, j: (i, j, 0)),
        out_shape=jax.ShapeDtypeStruct((b, t, wv), BF16),
        scratch_shapes=[pltpu.VMEM((nchains, dk, dv), F32), pltpu.VMEM((nchains, 1, dk), F32),
                        pltpu.VMEM((nchains, SUBLANES, LANES), F32), pltpu.VMEM((nchains, SUBLANES, dk), F32),
                        pltpu.VMEM((nchains, SUBLANES, dk), F32)],
        compiler_params=_cparams(("parallel", "arbitrary")),
        name="mlstm_scan",
    )(proj, proj, proj, gates, conv_w, ng)


def mlstm_mixer(x, g_pre, w_in, b_if, conv_w, norm_g, b, t):
    h, dk, dv, L = ML_HEADS, ML_DQK, ML_DV, ML_CHUNK
    n_main = 2 * h * dk + 2 * h * dv
    w_main = w_in[:, :n_main].astype(BF16)
    w_if_t = w_in[:, n_main:].T.astype(BF16)
    proj, gates = mlstm_proj(x, g_pre, w_main, w_if_t, b_if.reshape(2 * h, 1))
    gates = jnp.transpose(gates.reshape(2 * h, b, t), (1, 0, 2))
    hs = mlstm_scan(proj.reshape(b, t, n_main), gates, conv_w, norm_g[None, :])
    return hs.reshape(b * t, h * dv)


def _retention_kernel(lg_ref, q_ref, k_ref, v_ref, g_ref, cos_ref, sin_ref, out_ref, r_ref):
    L, H, dk, dv = RT_CHUNK, RT_HEADS, RT_DK, RT_DV
    half = dk // 2
    nc = q_ref.shape[1] // L

    @pl.when(pl.program_id(1) == 0)
    def _():
        r_ref[...] = jnp.zeros_like(r_ref)

    row = lax.broadcasted_iota(jnp.int32, (L, L), 0)
    col = lax.broadcasted_iota(jnp.int32, (L, L), 1)
    diff = jnp.maximum(row - col, 0).astype(F32)
    idx = lax.broadcasted_iota(jnp.int32, (L, 1), 0).astype(F32)

    def rotate(u, cos, sin):
        u1, u2 = u[:, :half], u[:, half:]
        return jnp.concatenate([u1 * cos - u2 * sin, u1 * sin + u2 * cos], axis=-1)

    def step(hd, sl, cos, sin):
        log_gamma = lg_ref[hd]
        qr = rotate(q_ref[0, sl, hd * dk:(hd + 1) * dk].astype(F32), cos, sin)
        kr = rotate(k_ref[0, sl, hd * dk:(hd + 1) * dk].astype(F32), cos, sin) * (dk ** -0.5)
        vc = v_ref[0, sl, hd * dv:(hd + 1) * dv]
        qb = qr.astype(BF16)
        qk = _dot_nt(qb, kr.astype(BF16))
        rmat = r_ref[hd]
        k_decay = jnp.exp(log_gamma * (L - 1.0 - idx))
        r_add = _dot_tn((kr * k_decay).astype(BF16), vc)
        cross = _dot(qb, rmat.astype(BF16))
        yield
        d_mask = jnp.where(row >= col, jnp.exp(log_gamma * diff), 0.0)
        q_decay = jnp.exp(log_gamma * (idx + 1.0))
        y = _dot((qk * d_mask).astype(BF16), vc) + cross * q_decay
        yield
        r_new = jnp.exp(jnp.full((1, 1), L, F32) * log_gamma) * rmat + r_add
        yn = y * lax.rsqrt(jnp.mean(y * y, axis=-1, keepdims=True) + NORM_EPS)
        gc = g_ref[0, sl, hd * dv:(hd + 1) * dv].astype(F32)
        return r_new, (gc * _sigmoid(gc) * yn).astype(out_ref.dtype)

    def body(c, carry):
        sl = pl.ds(pl.multiple_of(c * L, L), L)
        cos = cos_ref[0, sl, :]
        sin = sin_ref[0, sl, :]
        res = _lockstep(step(hd, sl, cos, sin) for hd in range(H))
        for hd in range(H):
            r_ref[hd], out_ref[0, sl, hd * dv:(hd + 1) * dv] = res[hd]
        return carry

    lax.fori_loop(0, nc, body, 0)


def retention_scan(proj, cos, sin, log_gamma):
    b, t, _ = proj.shape
    h, dk, dv = RT_HEADS, RT_DK, RT_DV
    tb = min(RT_TIME_BLOCK, t)
    wq, wv = h * dk, h * dv
    seq = lambda w, col: pl.BlockSpec((1, tb, w), lambda i, j: (i, j, col))
    return pl.pallas_call(
        _retention_kernel,
        grid=(b, t // tb),
        in_specs=[pl.BlockSpec(memory_space=pltpu.SMEM),
                  seq(wq, 0), seq(wq, 1), seq(wv, 2 * wq // wv), seq(wv, 2 * wq // wv + 1),
                  seq(dk // 2, 0), seq(dk // 2, 0)],
        out_specs=seq(wv, 0),
        out_shape=jax.ShapeDtypeStruct((b, t, wv), BF16),
        scratch_shapes=[pltpu.VMEM((h, dk, dv), F32)],
        compiler_params=_cparams(("parallel", "arbitrary")),
        name="retention_scan",
    )(log_gamma, proj, proj, proj, proj, cos, sin)


def retention_mixer(x, positions, g_pre, w_in, b, t):
    h, dk, dv = RT_HEADS, RT_DK, RT_DV
    proj = norm_matmul(x, g_pre, w_in.astype(BF16), BF16).reshape(b, t, -1)
    inv_freq = 1.0 / (RT_ROPE_BASE ** jnp.linspace(0.0, 1.0, dk // 2, dtype=F32))
    ang = positions.astype(F32)[:, :, None] * inv_freq
    log_gamma = jnp.log(1.0 - 2.0 ** (-5.0 - jnp.arange(h, dtype=F32)))
    y = retention_scan(proj, jnp.cos(ang), jnp.sin(ang), log_gamma)
    return y.reshape(b * t, h * dv)


def _rwkv_proj_kernel(x_ref, xp_ref, g_ref, mu_ref, wrkv_ref, w0_ref, w1_ref, w2_ref,
                      a0_ref, a1_ref, a2_ref, g1_ref, g2_ref,
                      r_ref, k_ref, v_ref, gc_ref, a_ref, gate_ref, *, seq_tiles):
    i = pl.program_id(0)
    tm = x_ref.shape[0]
    g = g_ref[...]
    h = _rms(x_ref[...], g)
    prev = _rms(xp_ref[...], g)[SUBLANES - 1:SUBLANES, :]
    prev = prev * (i % seq_tiles != 0).astype(F32)
    rid = lax.broadcasted_iota(jnp.int32, (tm, 1), 0)
    h_prev = jnp.where(rid == 0, prev, pltpu.roll(h, 1, axis=0))
    span = min(tm, MXU_DEPTH)
    rr = lax.broadcasted_iota(jnp.int32, (span, span), 0)
    cc = lax.broadcasted_iota(jnp.int32, (span, span), 1)
    chunk_tril = ((rr // RW_CHUNK == cc // RW_CHUNK) & (rr >= cc)).astype(BF16)

    def rows(lo):
        hh = h[lo:lo + span, :]
        xx = h_prev[lo:lo + span, :] - hh
        mix = lambda j: (hh + xx * mu_ref[j:j + 1, :]).astype(BF16)

        def put(ref, val):
            for p in range(ref.shape[0]):
                ref[p, lo:lo + span, :] = val[:, p * LANES:(p + 1) * LANES].astype(ref.dtype)

        put(r_ref, _dot(mix(0), wrkv_ref[0]))
        yield
        put(k_ref, _dot(mix(1), wrkv_ref[1]))
        yield
        put(v_ref, _dot(mix(2), wrkv_ref[2]))
        yield
        wl = jnp.tanh(_dot(mix(3), w1_ref[...])).astype(BF16)
        al = _dot(mix(4), a1_ref[...]).astype(BF16)
        gl = _sigmoid(_dot(mix(5), g1_ref[...])).astype(BF16)
        yield
        w_log = -_softplus(-(w0_ref[...] + _dot(wl, w2_ref[...]))) - 0.5
        put(a_ref, _sigmoid(a0_ref[...] + _dot(al, a2_ref[...])))
        put(gate_ref, _dot(gl, g2_ref[...]))
        yield
        lh, lm, ll = _split3(-jnp.exp(w_log))
        put(gc_ref, _dot(chunk_tril, lh) + _dot(chunk_tril, lm) + _dot(chunk_tril, ll))

    _lockstep(rows(lo) for lo in range(0, tm, span))


def rwkv_proj(x, g_pre, mu, w_rkv, w0, w1, w2, a0, a1, a2, g1, g2, t, *, tm=512):
    n, d = x.shape
    tm = min(tm, t)
    p = d // LANES
    outs = pl.BlockSpec((p, tm, LANES), lambda i: (0, i, 0))
    args = (g_pre, mu, w_rkv, w0, w1, w2, a0, a1, a2, g1, g2)
    shp = lambda dt: jax.ShapeDtypeStruct((p, n, LANES), dt)
    return pl.pallas_call(
        functools.partial(_rwkv_proj_kernel, seq_tiles=t // tm),
        grid=(n // tm,),
        in_specs=[
            pl.BlockSpec((tm, d), lambda i: (i, 0)),
            pl.BlockSpec((SUBLANES, d), lambda i: (jnp.maximum(i * (tm // SUBLANES) - 1, 0), 0)),
        ] + [_resident(a.shape) for a in args],
        out_specs=[outs] * 6,
        out_shape=[shp(BF16), shp(BF16), shp(BF16), shp(F32), shp(BF16), shp(BF16)],
        compiler_params=_cparams(("parallel",)),
        name="rwkv_proj",
    )(x, x, *args)


def _rwkv_scan_kernel(r_ref, k_ref, v_ref, gc_ref, a_ref, gate_ref, kk_ref, ka_ref, rk_ref,
                      lng_ref, lnb_ref, out_ref, s_ref):
    C = RW_CHUNK
    N = RW_HEAD
    C2 = 2 * C
    nc = r_ref.shape[2] // C

    @pl.when(pl.program_id(1) == 0)
    def _():
        s_ref[...] = jnp.zeros_like(s_ref)

    lane = lax.broadcasted_iota(jnp.int32, (1, LANES), 1)
    in_a = lane < N
    m_a = in_a.astype(F32)
    m_b = 1.0 - m_a
    rr = lax.broadcasted_iota(jnp.int32, (C2, C2), 0)
    cc = lax.broadcasted_iota(jnp.int32, (C2, C2), 1)
    same = (rr // C) == (cc // C)
    strict = same & ((rr % C) > (cc % C))
    incl = same & ((rr % C) >= (cc % C))
    eye2 = (rr == cc).astype(F32)
    level_masks = []
    for lvl in range(int(math.log2(C))):
        rb, cb = (rr % C) >> lvl, (cc % C) >> lvl
        level_masks.append(same & ((rb & 1) == 1) & (cb == rb - 1))
    incl2 = jnp.concatenate([incl, incl], axis=1)
    first_row = lax.broadcasted_iota(jnp.int32, (C, 1), 0) == 0

    def seg_sum(z):
        sa = jnp.sum(z * m_a, axis=-1, keepdims=True)
        sb = jnp.sum(z * m_b, axis=-1, keepdims=True)
        return jnp.where(in_a, sa, sb)

    def stack(z):
        return jnp.concatenate([z * m_a, z * m_b], axis=0)

    def load(row, gi, sl):
        f32 = lambda ref: ref[gi, row, sl, :].astype(F32)
        return (f32(r_ref), f32(k_ref), f32(v_ref), gc_ref[gi, row, sl, :], f32(a_ref), f32(gate_ref),
                s_ref[row, gi], kk_ref[gi], ka_ref[gi], rk_ref[gi], lng_ref[gi], lnb_ref[gi])

    def step(r, k, v, gcum, a_sig, gate, st0, kk_w, ka_w, rk_w, ln_g, ln_b):
        kk = k * kk_w
        kk = kk / jnp.maximum(jnp.sqrt(seg_sum(kk * kk)), 1e-12)
        k = k * (1.0 + (a_sig - 1.0) * ka_w)
        a_ = -kk
        b_ = kk * a_sig
        g_end = gcum[C - 1:C, :]
        g_excl = jnp.where(first_row, 0.0, pltpu.roll(gcum, 1, axis=0))
        e_neg = jnp.exp(-gcum)
        e_rest = jnp.exp(g_end - gcum)
        a_h = stack(a_ * jnp.exp(g_excl))
        r_h = stack(r * jnp.exp(gcum))
        b_t = stack(b_ * e_rest)
        k_t = stack(k * e_rest)
        twice = lambda z: jnp.concatenate([z, z], axis=0)
        v_s = twice(v).astype(BF16)
        lhs = jnp.concatenate([a_h, r_h], axis=0).astype(BF16)
        rhs = jnp.concatenate([twice(b_ * e_neg), twice(k * e_neg)], axis=0).astype(BF16)
        sc = _dot_nt(lhs, rhs)
        yield
        a_ab = jnp.where(strict, sc[:C2, :C2], 0.0)
        a_ak = jnp.where(strict, sc[:C2, C2:], 0.0)
        a_r = jnp.where(incl2, sc[C2:, :], 0.0).astype(BF16)
        w1 = _dot(a_ak.astype(BF16), v_s)
        tinv = eye2 + jnp.where(level_masks[0], a_ab, 0.0)
        for mk in level_masks[1:]:
            tb = tinv.astype(BF16)
            l21 = jnp.where(mk, a_ab, 0.0).astype(BF16)
            tl = _dot(tb, l21)
            yield
            tinv = tinv + _dot(tl.astype(BF16), tb)
            yield
        au = _dot(tinv.astype(BF16), jnp.concatenate([a_h, w1], axis=1).astype(BF16))
        yield
        amat = jnp.concatenate([au.astype(BF16),
                                jnp.concatenate([jnp.zeros((C2, LANES), BF16), v_s], axis=1)], axis=0)
        pz = _dot_tn(jnp.concatenate([b_t, k_t], axis=0).astype(BF16), amat)
        ry = _dot(a_r, amat)
        yield
        r_bar = r_h + ry[:, :LANES]
        sy = _dot(jnp.concatenate([r_bar, pz[:, :LANES]], axis=0).astype(BF16), st0.astype(BF16))
        yield
        st_new = st0 * jnp.broadcast_to(jnp.exp(g_end), (LANES, LANES)).T + sy[C2:, :] + pz[:, LANES:]
        ys = sy[:C2, :] + ry[:, LANES:]
        y = jnp.where(in_a, ys[:C, :], ys[C:, :])
        mean = seg_sum(y) * (1.0 / N)
        yc = y - mean
        var = seg_sum(yc * yc) * (1.0 / N)
        yn = yc * lax.rsqrt(var + RW_LN_EPS) * ln_g + ln_b
        bonus = seg_sum(r * k * rk_w) * v
        return st_new, ((yn + bonus) * gate).astype(out_ref.dtype)

    def body(c, carry):
        sl = pl.ds(pl.multiple_of(c * C, C), C)
        chains = [(row, gi) for row in range(r_ref.shape[1]) for gi in range(r_ref.shape[0])]
        results = _lockstep(step(*load(row, gi, sl)) for row, gi in chains)
        for (row, gi), res in zip(chains, results):
            s_ref[row, gi], out_ref[gi, row, sl, :] = res
        return carry

    lax.fori_loop(0, nc, body, 0)


def rwkv_scan(r, k, v, gcum, a, gate, k_k, k_a, r_k, ln_g, ln_b, b, t):
    p, n, l = r.shape
    tb = min(RW_TIME_BLOCK, t)
    rows = min(RW_ROWS, b)
    seq = lambda: pl.BlockSpec((p, rows, tb, l), lambda i, j: (0, i, j, 0))
    par = lambda: pl.BlockSpec((p, 1, l), lambda i, j: (0, 0, 0))
    view = lambda a: a.reshape(p, b, t, l)
    y = pl.pallas_call(
        _rwkv_scan_kernel,
        grid=(b // rows, t // tb),
        in_specs=[seq()] * 6 + [par()] * 5,
        out_specs=seq(),
        out_shape=jax.ShapeDtypeStruct((p, b, t, l), BF16),
        scratch_shapes=[pltpu.VMEM((rows, p, l, l), F32)],
        compiler_params=_cparams(("parallel", "arbitrary")),
        name="rwkv_scan",
    )(view(r), view(k), view(v), view(gcum), view(a), view(gate), k_k, k_a, r_k, ln_g, ln_b)
    return y.reshape(p, n, l)


def rwkv_mixer(x, g_pre, mu, w_rkv, w0, w1, w2, a0, a1, a2, g1, g2,
               k_k, k_a, r_k, ln_g, ln_b, b, t):
    bf = lambda a: a.astype(BF16)
    row = lambda a: a.reshape(1, -1)
    r, k, v, gcum, a, gate = rwkv_proj(x, g_pre, mu, bf(w_rkv), row(w0), bf(w1), bf(w2),
                                     row(a0), bf(a1), bf(a2), bf(g1), bf(g2), t)
    par = lambda a: a.reshape(RW_PAIRS, 1, LANES)
    return rwkv_scan(r, k, v, gcum, a, gate, par(k_k), par(k_a), par(r_k), par(ln_g), par(ln_b), b, t)


def kernel(x, positions, norm_g, ffn_w_gu, ffn_w_down, ml_w_in, ml_b_if, ml_conv_w, ml_norm_g, ml_w_out,
           rw_mu, rw_w_rkv, rw_w0, rw_w1, rw_w2, rw_a0, rw_a1, rw_a2, rw_g1, rw_g2,
           rw_k_k, rw_k_a, rw_r_k, rw_ln_g, rw_ln_b, rw_w_out, rt_w_in, rt_w_out):
    b, t, d = x.shape
    depth = norm_g.shape[0]
    xf = x.reshape(b * t, d)
    w_gu, w_down = ffn_w_gu[0, 0].astype(BF16), ffn_w_down[0, 0].astype(BF16)

    def ffn(k, xf, g_pre, g_post, mix=None):
        nxt = ()
        if k + 1 < 2 * depth:
            pick = divmod(k + 1, 2)
            nxt = ((ffn_w_gu, pick), (ffn_w_down, pick))
        outs = ffn_sublayer(xf, g_pre, w_gu, w_down, g_post, mix=mix, cast_next=nxt)
        return tuple(outs) if nxt else (outs[0], None, None)

    for layer in range(depth):
        g = norm_g[layer][:, None, :]
        xf, w_gu, w_down = ffn(2 * layer, xf, g[0], g[1])
        kind, j = layer % 3, layer // 3
        if kind == 0:
            h = mlstm_mixer(xf, g[2], ml_w_in[j], ml_b_if[j], ml_conv_w[j], ml_norm_g[j], b, t)
            w_out = ml_w_out[j]
        elif kind == 1:
            h = rwkv_mixer(xf, g[2], rw_mu[j], rw_w_rkv[j], rw_w0[j], rw_w1[j], rw_w2[j],
                           rw_a0[j], rw_a1[j], rw_a2[j], rw_g1[j], rw_g2[j],
                           rw_k_k[j], rw_k_a[j], rw_r_k[j].reshape(-1), rw_ln_g[j], rw_ln_b[j], b, t)
            w_out = rw_w_out[j]
        else:
            h = retention_mixer(xf, positions, g[2], rt_w_in[j], b, t)
            w_out = rt_w_out[j]
        xf, w_gu, w_down = ffn(2 * layer + 1, xf, g[4], g[5], mix=(h, w_out.astype(BF16), g[3]))
    return xf.reshape(b, t, d)
```
